```python
import math
import jax
import jax.numpy as jnp
from jax import lax
import numpy as np

D_MODEL = 1024
BATCH = 8
SEQ = 2048
DEPTH = 2
DEC_BATCH = 128
DEC_SEQ = 1
PAST_LEN = 8192
PAGE_SIZE = 128

N_A_LAYERS = DEPTH // 2
N_B_LAYERS = DEPTH - N_A_LAYERS
A_HEADS = 4
A_INNER = 2 * D_MODEL
A_HEAD_DIM = A_INNER // A_HEADS
A_CONV_W = 4
A_CHUNK = 64
B_HEADS = 16
B_KV_HEADS = 4
B_GROUP = B_HEADS // B_KV_HEADS
B_HEAD_DIM = D_MODEL // B_HEADS
WINDOW = 128
N_BUCKETS = 32
MAX_DISTANCE = 128
P_HEADS = 8
N_KEYS = 128
N_EXPERTS = N_KEYS * N_KEYS
D_KEY = 256
P_TOPK = 16
P_BLOCK = 128
DN_ALPHA = (2 * DEPTH) ** 0.25
DN_BETA = (8 * DEPTH) ** -0.25
LN_EPS = 1e-5
NEG_INF = -1e30

kernel_name = 'yoco_mlstm_swa_sink_peer_step'


def layer_norm(x, g, b):
    xf = x.astype(jnp.float32)
    mu = xf.mean(-1, keepdims=True)
    var = jnp.square(xf - mu).mean(-1, keepdims=True)
    y = (xf - mu) * lax.rsqrt(var + LN_EPS)
    return (y * g.astype(jnp.float32) + b.astype(jnp.float32)).astype(x.dtype)


def post_norm(x, sub, g, b):
    return layer_norm(DN_ALPHA * x + sub, g, b)


def t5_bucket(dist):
    max_exact = N_BUCKETS // 2
    d = jnp.maximum(dist, 0)
    df = jnp.maximum(d, 1).astype(jnp.float32)
    large = max_exact + (jnp.log(df / max_exact) / math.log(MAX_DISTANCE / max_exact)
                         * (N_BUCKETS - max_exact)).astype(jnp.int32)
    large = jnp.minimum(large, N_BUCKETS - 1)
    return jnp.where(d < max_exact, d, large)


def mlstm_chunkwise(q, k, v, log_i, log_f, C0, n0, m0, chunk):
    B, L, H, DH = q.shape
    nc = L // chunk

    def to_chunks(a):
        return jnp.moveaxis(a.reshape((B, nc, chunk) + a.shape[2:]), 1, 0)

    xs = tuple(to_chunks(a.astype(jnp.float32)) for a in (q, k, v, log_i, log_f))
    causal = jnp.tril(jnp.ones((chunk, chunk), dtype=bool))

    def step(carry, inp):
        C, n, m = carry
        qb, kb, vb, ib, fb = inp
        Fh = jnp.swapaxes(jnp.cumsum(fb, axis=1), 1, 2)
        ih = jnp.swapaxes(ib, 1, 2)
        d_intra = jnp.where(causal, Fh[..., :, None] - Fh[..., None, :] + ih[..., None, :], NEG_INF)
        b_inter = Fh + m[..., None]
        m_t = jnp.maximum(b_inter, d_intra.max(-1))
        s = jnp.einsum('bthd,bshd->bhts', qb, kb) * jnp.exp(d_intra - m_t[..., None])
        w_inter = jnp.swapaxes(jnp.exp(b_inter - m_t), 1, 2)[..., None]
        num = jnp.einsum('bhts,bshe->bthe', s, vb) + w_inter * jnp.einsum('bthd,bhde->bthe', qb, C)
        den = jnp.swapaxes(s.sum(-1), 1, 2) + w_inter[..., 0] * jnp.einsum('bthd,bhd->bth', qb, n)
        h = num / jnp.maximum(jnp.abs(den), jnp.exp(-jnp.swapaxes(m_t, 1, 2)))[..., None]
        F_L = Fh[..., -1]
        g = F_L[..., None] - Fh + ih
        m_new = jnp.maximum(F_L + m, g.max(-1))
        decay = jnp.exp(F_L + m - m_new)
        ws = jnp.swapaxes(jnp.exp(g - m_new[..., None]), 1, 2)[..., None]
        wk = ws * kb
        C_new = decay[..., None, None] * C + jnp.einsum('bshd,bshe->bhde', wk, vb)
        n_new = decay[..., None] * n + wk.sum(1)
        return (C_new, n_new, m_new), h

    init = (C0.astype(jnp.float32), n0.astype(jnp.float32), m0.astype(jnp.float32))
    (C, n, m), h = lax.scan(step, init, xs)
    h = jnp.moveaxis(h, 0, 1).reshape(B, L, H, DH)
    return h, C, n, m


def mlstm_mixer(x, conv_buf, C0, n0, m0, w_in, b_in, conv_w, conv_b, w_q, w_k, w_v, w_out, chunk):
    B, L, _ = x.shape
    proj = x @ w_in + b_in
    x_m, o_pre, i_pre, f_pre = jnp.split(
        proj, [A_INNER, 2 * A_INNER, 2 * A_INNER + A_HEADS], axis=-1)
    xp = jnp.concatenate([conv_buf.astype(x.dtype), x_m], axis=1)
    x_c = sum(xp[:, w:w + L] * conv_w[w] for w in range(A_CONV_W)) + conv_b
    x_c = jax.nn.silu(x_c)
    new_buf = xp[:, -(A_CONV_W - 1):]
    xch = x_c.reshape(B, L, A_HEADS, A_HEAD_DIM)
    xmh = x_m.reshape(B, L, A_HEADS, A_HEAD_DIM)
    q = jnp.einsum('blhd,hde->blhe', xch, w_q)
    k = jnp.einsum('blhd,hde->blhe', xch, w_k) * (A_HEAD_DIM ** -0.5)
    v = jnp.einsum('blhd,hde->blhe', xmh, w_v)
    log_i = i_pre.astype(jnp.float32)
    log_f = jax.nn.log_sigmoid(f_pre.astype(jnp.float32))
    h, C, n, m = mlstm_chunkwise(q, k, v, log_i, log_f, C0, n0, m0, chunk)
    h = jax.nn.sigmoid(o_pre) * h.reshape(B, L, A_INNER).astype(x.dtype)
    return h @ w_out, new_buf, C.astype(x.dtype), n.astype(x.dtype), m.astype(x.dtype)


def sink_attend(s, v, sinks, eq):
    sink = sinks.astype(jnp.float32).reshape(B_KV_HEADS, B_GROUP)[..., None]
    m = jnp.maximum(s.max(-1), sink)
    p = jnp.exp(s - m[..., None])
    den = p.sum(-1) + jnp.exp(sink - m)
    p = p / den[..., None]
    return jnp.einsum(eq, p.astype(v.dtype), v)


def swa_prompt(q, k, v, sinks, rel_bias):
    B, S = q.shape[:2]
    nb = S // WINDOW
    qb = q.reshape(B, nb, WINDOW, B_KV_HEADS, B_GROUP, B_HEAD_DIM)
    pad = jnp.zeros((B, WINDOW, B_KV_HEADS, B_HEAD_DIM), k.dtype)

    def band(a):
        ab = jnp.concatenate([pad, a], axis=1).reshape(B, nb + 1, WINDOW, B_KV_HEADS, B_HEAD_DIM)
        return jnp.concatenate([ab[:, :-1], ab[:, 1:]], axis=2)

    kb, vb = band(k), band(v)
    qi = jnp.arange(WINDOW)[:, None]
    kj = jnp.arange(2 * WINDOW)[None, :]
    dist = qi + WINDOW - kj
    blk = jnp.arange(nb)[:, None, None]
    valid = (dist >= 0) & (dist < WINDOW) & (blk * WINDOW - WINDOW + kj >= 0)
    bias = rel_bias[t5_bucket(dist)].astype(jnp.float32)
    bias = bias.transpose(2, 0, 1).reshape(B_KV_HEADS, B_GROUP, WINDOW, 2 * WINDOW)
    s = jnp.einsum('bnqkgd,bnskd->bnkgqs', qb, kb).astype(jnp.float32) * (B_HEAD_DIM ** -0.5) + bias
    s = jnp.where(valid[None, :, None, None], s, NEG_INF)
    o = sink_attend(s, vb, sinks, 'bnkgqs,bnskd->bnqkgd')
    return o.reshape(B, S, B_HEADS * B_HEAD_DIM)


def swa_decode(q, k_all, v_all, sinks, rel_bias):
    B, L = q.shape[:2]
    qpos = jnp.arange(L)[:, None] + WINDOW
    kpos = jnp.arange(WINDOW + L)[None, :]
    dist = qpos - kpos
    valid = (dist >= 0) & (dist < WINDOW)
    bias = rel_bias[t5_bucket(dist)].astype(jnp.float32)
    bias = bias.transpose(2, 0, 1).reshape(B_KV_HEADS, B_GROUP, L, WINDOW + L)
    s = jnp.einsum('bqkgd,bskd->bkgqs', q, k_all).astype(jnp.float32) * (B_HEAD_DIM ** -0.5) + bias
    s = jnp.where(valid, s, NEG_INF)
    o = sink_attend(s, v_all, sinks, 'bkgqs,bskd->bqkgd')
    return o.reshape(B, L, B_HEADS * B_HEAD_DIM)


def peer(x, w_q, subkeys, u, v):
    B, L, D = x.shape
    T = B * L
    xt = x.reshape(T, D)
    q = (xt @ w_q).reshape(T, P_HEADS, 2, D_KEY // 2)
    sc = jnp.einsum('thpc,hpnc->thpn', q, subkeys).astype(jnp.float32)
    top_s, top_i = lax.top_k(sc, P_TOPK)
    cand = top_s[:, :, 0, :, None] + top_s[:, :, 1, None, :]
    cand_idx = top_i[:, :, 0, :, None] * N_KEYS + top_i[:, :, 1, None, :]
    best_s, best_pos = lax.top_k(cand.reshape(T, P_HEADS, P_TOPK * P_TOPK), P_TOPK)
    idx = jnp.take_along_axis(cand_idx.reshape(T, P_HEADS, P_TOPK * P_TOPK), best_pos, axis=-1)
    gate = jax.nn.softmax(best_s, axis=-1)
    idx = idx.reshape(T, P_HEADS * P_TOPK)
    gate = gate.reshape(T, P_HEADS * P_TOPK).astype(x.dtype)
    pad = (-T) % P_BLOCK
    xt = jnp.pad(xt, ((0, pad), (0, 0)))
    idx = jnp.pad(idx, ((0, pad), (0, 0)))
    gate = jnp.pad(gate, ((0, pad), (0, 0)))
    nblk = (T + pad) // P_BLOCK

    def block(args):
        xb, ib, gb = args
        ub = jnp.take(u, ib, axis=0)
        a = jax.nn.gelu(jnp.einsum('td,ted->te', xb, ub), approximate=False)
        vb = jnp.take(v, ib, axis=0)
        return jnp.einsum('te,ted->td', a * gb, vb)

    out = lax.map(block, (xt.reshape(nblk, P_BLOCK, D),
                          idx.reshape(nblk, P_BLOCK, -1),
                          gate.reshape(nblk, P_BLOCK, -1)))
    return out.reshape(-1, D)[:T].reshape(B, L, D)


def trunk(x, conv0, C0, n0, m0, k_buf, v_buf, p, prompt):
    B, L, _ = x.shape
    if prompt:
        conv0 = jnp.zeros((N_A_LAYERS, B, A_CONV_W - 1, A_INNER), x.dtype)
        C0 = jnp.zeros((N_A_LAYERS, B, A_HEADS, A_HEAD_DIM, A_HEAD_DIM), x.dtype)
        n0 = jnp.zeros((N_A_LAYERS, B, A_HEADS, A_HEAD_DIM), x.dtype)
        m0 = jnp.zeros((N_A_LAYERS, B, A_HEADS), x.dtype)
    new_conv, new_C, new_n, new_m = [], [], [], []
    k_all = v_all = new_k_win = new_v_win = None
    for layer in range(DEPTH):
        if layer < N_A_LAYERS:
            a = layer
            h, cb, C, n, m = mlstm_mixer(
                x, conv0[a], C0[a], n0[a], m0[a], p['a_w_in'][a], p['a_b_in'][a],
                p['a_conv_w'][a], p['a_conv_b'][a], p['a_w_q'][a], p['a_w_k'][a],
                p['a_w_v'][a], p['a_w_out'][a], A_CHUNK if prompt else L)
            new_conv.append(cb)
            new_C.append(C)
            new_n.append(n)
            new_m.append(m)
        else:
            if layer == N_A_LAYERS:
                k_new, v_new = jnp.split(x @ p['kv_w'], 2, axis=-1)
                k_new = k_new.reshape(B, L, B_KV_HEADS, B_HEAD_DIM)
                v_new = v_new.reshape(B, L, B_KV_HEADS, B_HEAD_DIM)
                if prompt:
                    k_all, v_all = k_new, v_new
                else:
                    k_all = jnp.concatenate([k_buf.astype(x.dtype), k_new], axis=1)
                    v_all = jnp.concatenate([v_buf.astype(x.dtype), v_new], axis=1)
                new_k_win = k_all[:, -WINDOW:]
                new_v_win = v_all[:, -WINDOW:]
            bl = layer - N_A_LAYERS
            q = (x @ p['b_w_q'][bl]).reshape(B, L, B_KV_HEADS, B_GROUP, B_HEAD_DIM)
            if prompt:
                o = swa_prompt(q, k_all, v_all, p['b_sinks'][bl], p['rel_bias'])
            else:
                o = swa_decode(q, k_all, v_all, p['b_sinks'][bl], p['rel_bias'])
            h = o @ p['b_w_o'][bl]
        x = post_norm(x, h, p['ln_mix_g'][layer], p['ln_mix_b'][layer])
        f = peer(x, p['peer_w_q'][layer], p['peer_subkeys'][layer], p['peer_u'][layer], p['peer_v'][layer])
        x = post_norm(x, f, p['ln_ffn_g'][layer], p['ln_ffn_b'][layer])
    return (x, jnp.stack(new_C), jnp.stack(new_n), jnp.stack(new_m), jnp.stack(new_conv),
            new_k_win, new_v_win)


def setup_inputs(seed: int = 0) -> dict:
    key = jax.random.key(seed)
    ks = iter(jax.random.split(key, 40))

    def nrm(shape, scale):
        return jax.random.normal(next(ks), shape, jnp.float32) * scale

    NA, NB = N_A_LAYERS, N_B_LAYERS
    kvw = B_KV_HEADS * B_HEAD_DIM
    a_b_in = jnp.concatenate([
        nrm((NA, 2 * A_INNER), 0.02),
        nrm((NA, A_HEADS), 0.1),
        jnp.linspace(3.0, 6.0, A_HEADS)[None, :] + nrm((NA, A_HEADS), 0.1),
    ], axis=-1)
    kv_w = jnp.concatenate([nrm((D_MODEL, kvw), D_MODEL ** -0.5),
                            nrm((D_MODEL, kvw), D_MODEL ** -0.5 * DN_BETA)], axis=-1)
    return {
        'x_prompt': nrm((BATCH, SEQ, D_MODEL), 1.0),
        'x_sample': nrm((DEC_BATCH, DEC_SEQ, D_MODEL), 1.0),
        'state_mlstm_C': nrm((NA, DEC_BATCH, A_HEADS, A_HEAD_DIM, A_HEAD_DIM), 0.05),
        'state_mlstm_n': nrm((NA, DEC_BATCH, A_HEADS, A_HEAD_DIM), 0.05),
        'state_mlstm_m': nrm((NA, DEC_BATCH, A_HEADS), 0.5),
        'state_mlstm_conv': nrm((NA, DEC_BATCH, A_CONV_W - 1, A_INNER), 1.0),
        'cache_k_win': nrm((DEC_BATCH, WINDOW, B_KV_HEADS, B_HEAD_DIM), 1.0),
        'cache_v_win': nrm((DEC_BATCH, WINDOW, B_KV_HEADS, B_HEAD_DIM), DN_BETA),
        'a_w_in': nrm((NA, D_MODEL, 2 * A_INNER + 2 * A_HEADS), D_MODEL ** -0.5),
        'a_b_in': a_b_in,
        'a_conv_w': nrm((NA, A_CONV_W, A_INNER), A_CONV_W ** -0.5),
        'a_conv_b': nrm((NA, A_INNER), 0.02),
        'a_w_q': nrm((NA, A_HEADS, A_HEAD_DIM, A_HEAD_DIM), A_HEAD_DIM ** -0.5),
        'a_w_k': nrm((NA, A_HEADS, A_HEAD_DIM, A_HEAD_DIM), A_HEAD_DIM ** -0.5),
        'a_w_v': nrm((NA, A_HEADS, A_HEAD_DIM, A_HEAD_DIM), A_HEAD_DIM ** -0.5 * DN_BETA),
        'a_w_out': nrm((NA, A_INNER, D_MODEL), A_INNER ** -0.5 * DN_BETA),
        'kv_w': kv_w,
        'b_w_q': nrm((NB, D_MODEL, B_HEADS * B_HEAD_DIM), D_MODEL ** -0.5),
        'b_w_o': nrm((NB, B_HEADS * B_HEAD_DIM, D_MODEL), (B_HEADS * B_HEAD_DIM) ** -0.5 * DN_BETA),
        'b_sinks': nrm((NB, B_HEADS), 0.5),
        'rel_bias': nrm((N_BUCKETS, B_HEADS), 0.1),
        'ln_mix_g': 1.0 + nrm((DEPTH, D_MODEL), 0.02),
        'ln_mix_b': nrm((DEPTH, D_MODEL), 0.02),
        'ln_ffn_g': 1.0 + nrm((DEPTH, D_MODEL), 0.02),
        'ln_ffn_b': nrm((DEPTH, D_MODEL), 0.02),
        'peer_w_q': nrm((DEPTH, D_MODEL, P_HEADS * D_KEY), D_MODEL ** -0.5),
        'peer_subkeys': nrm((DEPTH, P_HEADS, 2, N_KEYS, D_KEY // 2), (D_KEY // 2) ** -0.5),
        'peer_u': nrm((DEPTH, N_EXPERTS, D_MODEL), D_MODEL ** -0.5),
        'peer_v': nrm((DEPTH, N_EXPERTS, D_MODEL), D_MODEL ** -0.5 * DN_BETA),
    }


def reference(x_prompt, x_sample, state_mlstm_C, state_mlstm_n, state_mlstm_m, state_mlstm_conv,
              cache_k_win, cache_v_win, a_w_in, a_b_in, a_conv_w, a_conv_b, a_w_q, a_w_k, a_w_v,
              a_w_out, kv_w, b_w_q, b_w_o, b_sinks, rel_bias, ln_mix_g, ln_mix_b, ln_ffn_g,
              ln_ffn_b, peer_w_q, peer_subkeys, peer_u, peer_v):
    p = {'a_w_in': a_w_in, 'a_b_in': a_b_in, 'a_conv_w': a_conv_w, 'a_conv_b': a_conv_b,
         'a_w_q': a_w_q, 'a_w_k': a_w_k, 'a_w_v': a_w_v, 'a_w_out': a_w_out, 'kv_w': kv_w,
         'b_w_q': b_w_q, 'b_w_o': b_w_o, 'b_sinks': b_sinks, 'rel_bias': rel_bias,
         'ln_mix_g': ln_mix_g, 'ln_mix_b': ln_mix_b, 'ln_ffn_g': ln_ffn_g, 'ln_ffn_b': ln_ffn_b,
         'peer_w_q': peer_w_q, 'peer_subkeys': peer_subkeys, 'peer_u': peer_u, 'peer_v': peer_v}
    y_prompt, p_C, p_n, p_m, p_conv, p_k, p_v = trunk(
        x_prompt, None, None, None, None, None, None, p, True)
    y_sample, s_C, s_n, s_m, s_conv, s_k, s_v = trunk(
        x_sample, state_mlstm_conv, state_mlstm_C, state_mlstm_n, state_mlstm_m,
        cache_k_win, cache_v_win, p, False)
    return (y_prompt, y_sample, p_C, p_n, p_m, p_conv, p_k, p_v, s_C, s_n, s_m, s_conv, s_k, s_v)
```

```python
import functools
import math

import jax
import jax.numpy as jnp
from jax import lax
from jax.experimental import pallas as pl
from jax.experimental.pallas import tpu as pltpu

D_MODEL = 1024
DEPTH = 2
A_HEADS = 4
A_INNER = 2 * D_MODEL
A_HEAD_DIM = A_INNER // A_HEADS
A_CONV_W = 4
B_HEADS = 16
B_KV_HEADS = 4
B_GROUP = B_HEADS // B_KV_HEADS
B_HEAD_DIM = D_MODEL // B_HEADS
WINDOW = 128
N_BUCKETS = 32
MAX_DISTANCE = 128
P_HEADS = 8
N_KEYS = 128
N_EXPERTS = N_KEYS * N_KEYS
D_KEY = 256
P_TOPK = 16
DN_ALPHA = (2 * DEPTH) ** 0.25
LN_EPS = 1e-5
NEG_INF = -1e30

LANES = 128
SUBLANES = 8
VMEM_LIMIT_BYTES = 48 * 1024 * 1024

MLSTM_CHUNK = 256
PEER_TOKEN_TILE = 512
PEER_ROWS_PER_STEP = 4
NOT_SELECTED_RANK = float(P_TOPK)

F32 = jnp.float32
BF16 = jnp.bfloat16


def _params(*sem):
    return pltpu.CompilerParams(dimension_semantics=sem, vmem_limit_bytes=VMEM_LIMIT_BYTES)


def _tree(op, vals):
    vals = list(vals)
    while len(vals) > 1:
        nxt = [op(vals[i], vals[i + 1]) for i in range(0, len(vals) - 1, 2)]
        if len(vals) % 2:
            nxt.append(vals[-1])
        vals = nxt
    return vals[0]


def _layer_norm(y, g, b):
    mu = jnp.mean(y, axis=-1, keepdims=True)
    yc = y - mu
    var = jnp.mean(yc * yc, axis=-1, keepdims=True)
    return yc * lax.rsqrt(var + LN_EPS) * g + b


def _linear_kernel(x_ref, w_ref, b_ref, o_ref):
    x = x_ref[...].astype(BF16)
    o_ref[...] = jnp.dot(x, w_ref[...], preferred_element_type=F32) + b_ref[...]


def _linear(x, w, b, tm, tn):
    t, k = x.shape
    n = w.shape[1]
    return pl.pallas_call(
        _linear_kernel,
        grid=(t // tm, n // tn),
        in_specs=[pl.BlockSpec((tm, k), lambda i, j: (i, 0)),
                  pl.BlockSpec((k, tn), lambda i, j: (0, j)),
                  pl.BlockSpec((1, tn), lambda i, j: (0, j))],
        out_specs=pl.BlockSpec((tm, tn), lambda i, j: (i, j)),
        out_shape=jax.ShapeDtypeStruct((t, n), F32),
        compiler_params=_params("parallel", "arbitrary"),
        name="linear",
    )(x, w, b)


def _outproj_ln_kernel(gated, *refs):
    if gated:
        a_ref, o_ref, w_ref, res_ref, g_ref, b_ref, out_ref = refs
        act = jax.nn.sigmoid(o_ref[...]) * a_ref[...]
    else:
        a_ref, w_ref, res_ref, g_ref, b_ref, out_ref = refs
        act = a_ref[...]
    sub = jnp.dot(act.astype(BF16), w_ref[...], preferred_element_type=F32)
    out_ref[...] = _layer_norm(DN_ALPHA * res_ref[...] + sub, g_ref[...], b_ref[...])


def _outproj_ln(act, gate_src, w, res, g, b, tm):
    t, k = act.shape
    d = w.shape[1]
    gated = gate_src is not None
    in_specs = [pl.BlockSpec((tm, k), lambda i: (i, 0))]
    args = [act]
    if gated:
        in_specs.append(pl.BlockSpec((tm, k), lambda i: (i, 1)))
        args.append(gate_src)
    in_specs += [pl.BlockSpec((k, d), lambda i: (0, 0)),
                 pl.BlockSpec((tm, d), lambda i: (i, 0)),
                 pl.BlockSpec((1, d), lambda i: (0, 0)),
                 pl.BlockSpec((1, d), lambda i: (0, 0))]
    args += [w, res, g, b]
    return pl.pallas_call(
        functools.partial(_outproj_ln_kernel, gated),
        grid=(t // tm,),
        in_specs=in_specs,
        out_specs=pl.BlockSpec((tm, d), lambda i: (i, 0)),
        out_shape=jax.ShapeDtypeStruct((t, d), F32),
        compiler_params=_params("parallel"),
        name="outproj_ln",
    )(*args)


def _log_sigmoid(x):
    return jnp.minimum(x, 0.0) - jnp.log1p(jnp.exp(-jnp.abs(x)))


def _gates_kernel(chunk, g_ref, row_ref, col_ref):
    length = g_ref.shape[0]
    gt = g_ref[...].T
    top = gt[0:SUBLANES]
    row = lax.broadcasted_iota(jnp.int32, top.shape, 0)
    pos = lax.broadcasted_iota(jnp.int32, top.shape, 1) % chunk
    is_f = row >= A_HEADS
    x = jnp.where(is_f, _log_sigmoid(top), 0.0)
    shift = 1
    while shift < chunk:
        x = x + jnp.where(pos >= shift, pltpu.roll(x, shift, 1), 0.0)
        shift *= 2
    table = jnp.where(is_f, x, top)
    row_ref[...] = table
    padded = jnp.concatenate([table, jnp.zeros((LANES - SUBLANES, length), F32)], axis=0)
    col_ref[...] = padded.T


def _gate_tables(gates, batch, length, chunk):
    return pl.pallas_call(
        functools.partial(_gates_kernel, chunk),
        grid=(batch,),
        in_specs=[pl.BlockSpec((length, LANES), lambda b: (b, 0))],
        out_specs=[pl.BlockSpec((None, SUBLANES, length), lambda b: (b, 0, 0)),
                   pl.BlockSpec((length, LANES), lambda b: (b, 0))],
        out_shape=[jax.ShapeDtypeStruct((batch, SUBLANES, length), F32),
                   jax.ShapeDtypeStruct((batch * length, LANES), F32)],
        compiler_params=_params("parallel"),
        name="gate_tables",
    )(gates)


CONV_ROW_CHUNK = 512
CONV_PAD = SUBLANES


def _conv_qkv_kernel(xm_ref, cw_ref, cb_ref, wq_ref, wk_ref, wv_ref,
                     q_ref, k_ref, v_ref, pad_ref):
    length = xm_ref.shape[0]
    pad_ref[0:CONV_PAD, :] = jnp.zeros((CONV_PAD, A_HEAD_DIM), F32)
    pad_ref[CONV_PAD:CONV_PAD + length, :] = xm_ref[...]
    first = CONV_PAD - (A_CONV_W - 1)
    for c0 in range(0, length, CONV_ROW_CHUNK):
        acc = cb_ref[...]
        for w in range(A_CONV_W):
            acc = acc + pad_ref[c0 + first + w:c0 + first + w + CONV_ROW_CHUNK, :] * cw_ref[w:w + 1, :]
        xc = (acc * jax.nn.sigmoid(acc)).astype(BF16)
        xm = xm_ref[c0:c0 + CONV_ROW_CHUNK, :].astype(BF16)
        rows = slice(c0, c0 + CONV_ROW_CHUNK)
        q_ref[rows, :] = jnp.dot(xc, wq_ref[...], preferred_element_type=F32).astype(BF16)
        k = jnp.dot(xc, wk_ref[...], preferred_element_type=F32) * (A_HEAD_DIM ** -0.5)
        k_ref[rows, :] = k.astype(BF16)
        v_ref[rows, :] = jnp.dot(xm, wv_ref[...], preferred_element_type=F32).astype(BF16)


def _conv_qkv(proj, cw, cb, wq, wk, wv, batch, length):
    t = batch * length
    hd = A_HEAD_DIM
    tok = pl.BlockSpec((length, hd), lambda b, h: (b, h))
    wspec = pl.BlockSpec((None, hd, hd), lambda b, h: (h, 0, 0))
    out = jax.ShapeDtypeStruct((t, A_INNER), BF16)
    return pl.pallas_call(
        _conv_qkv_kernel,
        grid=(batch, A_HEADS),
        in_specs=[tok,
                  pl.BlockSpec((A_CONV_W, hd), lambda b, h: (0, h)),
                  pl.BlockSpec((1, hd), lambda b, h: (0, h)),
                  wspec, wspec, wspec],
        out_specs=[tok, tok, tok],
        out_shape=[out, out, out],
        scratch_shapes=[pltpu.VMEM((length + CONV_PAD, hd), F32)],
        compiler_params=_params("parallel", "arbitrary"),
        name="conv_qkv",
    )(proj, cw, cb, wq, wk, wv)


def _mlstm_kernel(q_ref, k_ref, v_ref, row_ref, col_ref,
                  h_ref, c_out_ref, n_out_ref, m_out_ref,
                  c_scr, n_scr, m_scr):
    ci = pl.program_id(1)
    chunk = q_ref.shape[0]
    hd = A_HEAD_DIM

    @pl.when(ci == 0)
    def _():
        c_scr[...] = jnp.zeros(c_scr.shape, F32)
        n_scr[...] = jnp.zeros(n_scr.shape, F32)
        m_scr[...] = jnp.zeros(m_scr.shape, F32)

    t_idx = lax.broadcasted_iota(jnp.int32, (chunk, chunk), 0)
    s_idx = lax.broadcasted_iota(jnp.int32, (chunk, chunk), 1)
    causal = s_idx <= t_idx
    for h in range(A_HEADS):
        cols = slice(h * hd, (h + 1) * hd)
        qh, kh, vh = q_ref[:, cols], k_ref[:, cols], v_ref[:, cols]
        i_col = col_ref[:, h:h + 1]
        f_col = col_ref[:, A_HEADS + h:A_HEADS + h + 1]
        i_row = row_ref[h:h + 1, :]
        f_row = row_ref[A_HEADS + h:A_HEADS + h + 1, :]
        m_prev = m_scr[h][:, 0:1]
        d = jnp.where(causal, f_col - f_row + i_row, NEG_INF)
        b_inter = f_col + m_prev
        m_t = jnp.maximum(b_inter, jnp.max(d, axis=1, keepdims=True))
        qk = lax.dot_general(qh, kh, (((1,), (1,)), ((), ())), preferred_element_type=F32)
        s = qk * jnp.exp(d - m_t)
        w_inter = jnp.exp(b_inter - m_t)
        q_c = jnp.dot(qh, c_scr[h].astype(BF16), preferred_element_type=F32)
        num = jnp.dot(s.astype(BF16), vh, preferred_element_type=F32) + w_inter * q_c
        q_n = jnp.sum(qh.astype(F32) * n_scr[h], axis=1, keepdims=True)
        den = jnp.sum(s, axis=1, keepdims=True) + w_inter * q_n
        h_ref[:, cols] = num / jnp.maximum(jnp.abs(den), jnp.exp(-m_t))
        f_last = f_col[chunk - 1:chunk, :]
        g = f_last - f_col + i_col
        m_new = jnp.maximum(f_last + m_prev, jnp.max(g, axis=0, keepdims=True))
        decay = jnp.exp(f_last + m_prev - m_new)
        wk = jnp.exp(g - m_new) * kh.astype(F32)
        kv = lax.dot_general(wk.astype(BF16), vh, (((0,), (0,)), ((), ())),
                             preferred_element_type=F32)
        c_scr[h] = decay * c_scr[h] + kv
        n_scr[h] = decay * n_scr[h] + jnp.sum(wk, axis=0, keepdims=True)
        m_scr[h] = jnp.broadcast_to(m_new, (1, LANES))

    @pl.when(ci == pl.num_programs(1) - 1)
    def _():
        c_out_ref[...] = c_scr[...]
        n_out_ref[...] = n_scr[...]
        m_out_ref[...] = m_scr[...]


def _mlstm_prompt(q, k, v, row_tab, col_tab, batch, length, chunk):
    nc = length // chunk
    t = batch * length
    tok = pl.BlockSpec((chunk, A_INNER), lambda b, c: (b * nc + c, 0))
    return pl.pallas_call(
        _mlstm_kernel,
        grid=(batch, nc),
        in_specs=[tok, tok, tok,
                  pl.BlockSpec((None, SUBLANES, chunk), lambda b, c: (b, 0, c)),
                  pl.BlockSpec((chunk, LANES), lambda b, c: (b * nc + c, 0))],
        out_specs=[tok,
                   pl.BlockSpec((None, A_HEADS, A_HEAD_DIM, A_HEAD_DIM), lambda b, c: (b, 0, 0, 0)),
                   pl.BlockSpec((None, A_HEADS, 1, A_HEAD_DIM), lambda b, c: (b, 0, 0, 0)),
                   pl.BlockSpec((None, A_HEADS, 1, LANES), lambda b, c: (b, 0, 0, 0))],
        out_shape=[jax.ShapeDtypeStruct((t, A_INNER), F32),
                   jax.ShapeDtypeStruct((batch, A_HEADS, A_HEAD_DIM, A_HEAD_DIM), F32),
                   jax.ShapeDtypeStruct((batch, A_HEADS, 1, A_HEAD_DIM), F32),
                   jax.ShapeDtypeStruct((batch, A_HEADS, 1, LANES), F32)],
        scratch_shapes=[pltpu.VMEM((A_HEADS, A_HEAD_DIM, A_HEAD_DIM), F32),
                        pltpu.VMEM((A_HEADS, 1, A_HEAD_DIM), F32),
                        pltpu.VMEM((A_HEADS, 1, LANES), F32)],
        compiler_params=_params("parallel", "arbitrary"),
        name="mlstm_prompt",
    )(q, k, v, row_tab, col_tab)


def _sample_conv_qkv_kernel(proj_ref, buf_ref, cw_ref, cb_ref, wq_ref, wk_ref, wv_ref,
                            q_ref, k_ref, v_ref, qt_ref, kt_ref):
    xm = proj_ref[:, 0:A_INNER]
    acc = cb_ref[...] + xm * cw_ref[A_CONV_W - 1:A_CONV_W, :]
    for w in range(A_CONV_W - 1):
        acc = acc + buf_ref[:, w * A_INNER:(w + 1) * A_INNER] * cw_ref[w:w + 1, :]
    xc = (acc * jax.nn.sigmoid(acc)).astype(BF16)
    xmb = xm.astype(BF16)
    for h in range(A_HEADS):
        cols = slice(h * A_HEAD_DIM, (h + 1) * A_HEAD_DIM)
        q_ref[:, cols] = jnp.dot(xc[:, cols], wq_ref[h], preferred_element_type=F32)
        k_ref[:, cols] = (jnp.dot(xc[:, cols], wk_ref[h], preferred_element_type=F32)
                          * (A_HEAD_DIM ** -0.5))
        v_ref[:, cols] = jnp.dot(xmb[:, cols], wv_ref[h], preferred_element_type=F32)
    qt_ref[...] = q_ref[...].T
    kt_ref[...] = k_ref[...].T


def _sample_conv_qkv(proj, conv_buf_flat, cw, cb, wq, wk, wv):
    b = proj.shape[0]
    row = jax.ShapeDtypeStruct((b, A_INNER), F32)
    col = jax.ShapeDtypeStruct((A_INNER, b), F32)
    return pl.pallas_call(
        _sample_conv_qkv_kernel,
        out_shape=[row, row, row, col, col],
        compiler_params=pltpu.CompilerParams(vmem_limit_bytes=VMEM_LIMIT_BYTES),
        name="sample_conv_qkv",
    )(proj, conv_buf_flat, cw, cb, wq, wk, wv)


def _sample_mlstm_kernel(q_ref, k_ref, v_ref, qt_ref, kt_ref, g_ref, m0_ref, n0_ref, c0_ref,
                         h_ref, c_ref, n_ref, m_ref):
    b = pl.program_id(0)
    h = pl.program_id(1)
    hd = A_HEAD_DIM
    pick = (lax.broadcasted_iota(jnp.int32, (LANES, LANES), 0) == b).astype(F32)
    q_col = jnp.dot(qt_ref[...], pick, precision=lax.Precision.HIGHEST, preferred_element_type=F32)
    k_col = jnp.dot(kt_ref[...], pick, precision=lax.Precision.HIGHEST, preferred_element_type=F32)
    reps = hd // LANES
    q_mat = jnp.concatenate([q_col] * reps, axis=1)
    k_mat = jnp.concatenate([k_col] * reps, axis=1)
    q_row, k_row, v_row = q_ref[...], k_ref[...], v_ref[...]
    lane = lax.broadcasted_iota(jnp.int32, (1, LANES), 1)
    gates = g_ref[...]
    log_i = jnp.sum(jnp.where(lane == h, gates, 0.0), axis=1, keepdims=True)
    f_pre = jnp.sum(jnp.where(lane == h + A_HEADS, gates, 0.0), axis=1, keepdims=True)
    log_f = _log_sigmoid(f_pre)
    lane_h = lax.broadcasted_iota(jnp.int32, (1, A_HEADS), 1)
    m0 = jnp.sum(jnp.where(lane_h == h, m0_ref[...], 0.0), axis=1, keepdims=True)
    m_t = jnp.maximum(log_f + m0, log_i)
    w_inter = jnp.exp(log_f + m0 - m_t)
    w_new = jnp.exp(log_i - m_t)
    c0 = c0_ref[...]
    n0 = n0_ref[...]
    s = jnp.sum(q_row * k_row, axis=1, keepdims=True) * w_new
    q_c = jnp.sum(q_mat * c0, axis=0, keepdims=True)
    q_n = jnp.sum(q_row * n0, axis=1, keepdims=True)
    num = s * v_row + w_inter * q_c
    den = s + w_inter * q_n
    h_ref[...] = num / jnp.maximum(jnp.abs(den), jnp.exp(-m_t))
    c_ref[...] = w_inter * c0 + (w_new * k_mat) * v_row
    n_ref[...] = w_inter * n0 + w_new * k_row
    m_ref[...] = jnp.broadcast_to(m_t, (1, LANES))


def _sample_mlstm(q, k, v, qt, kt, gates, m0, n0, c0):
    b = q.shape[0]
    hd = A_HEAD_DIM
    row3 = lambda a: a.reshape(b, 1, a.shape[-1])
    rspec = pl.BlockSpec((None, 1, hd), lambda i, h: (i, 0, h))
    cspec = pl.BlockSpec((hd, LANES), lambda i, h: (h, 0))
    nspec = pl.BlockSpec((None, None, 1, hd), lambda i, h: (i, h, 0, 0))
    mspec = pl.BlockSpec((None, None, 1, LANES), lambda i, h: (i, h, 0, 0))
    big = pl.BlockSpec((None, None, hd, hd), lambda i, h: (i, h, 0, 0))
    return pl.pallas_call(
        _sample_mlstm_kernel,
        grid=(b, A_HEADS),
        in_specs=[rspec, rspec, rspec, cspec, cspec,
                  pl.BlockSpec((None, 1, LANES), lambda i, h: (i, 0, 0)),
                  pl.BlockSpec((None, 1, A_HEADS), lambda i, h: (i, 0, 0)),
                  nspec, big],
        out_specs=[rspec, big, nspec, mspec],
        out_shape=[jax.ShapeDtypeStruct((b, 1, A_INNER), F32),
                   jax.ShapeDtypeStruct((b, A_HEADS, hd, hd), F32),
                   jax.ShapeDtypeStruct((b, A_HEADS, 1, hd), F32),
                   jax.ShapeDtypeStruct((b, A_HEADS, 1, LANES), F32)],
        compiler_params=_params("parallel", "arbitrary"),
        name="sample_mlstm",
    )(row3(q), row3(k), row3(v), qt, kt, row3(gates), row3(m0),
      n0.reshape(b, A_HEADS, 1, hd), c0)


def _bias_kernel(bucket_ref, rel_ref, bias_ref, bias_t_ref):
    h = pl.program_id(0)
    bucket = bucket_ref[...]
    acc = jnp.zeros(bucket.shape, F32)
    for n in range(N_BUCKETS):
        acc = jnp.where(bucket == n, rel_ref[n, h], acc)
    bias_ref[...] = acc
    bias_t_ref[...] = acc.T


def _bias_tables(bucket, rel_bias):
    w, w2 = bucket.shape
    return pl.pallas_call(
        _bias_kernel,
        grid=(B_HEADS,),
        in_specs=[pl.BlockSpec((w, w2), lambda h: (0, 0)),
                  pl.BlockSpec(memory_space=pltpu.SMEM)],
        out_specs=[pl.BlockSpec((None, w, w2), lambda h: (h, 0, 0)),
                   pl.BlockSpec((None, w2, w), lambda h: (h, 0, 0))],
        out_shape=[jax.ShapeDtypeStruct((B_HEADS, w, w2), F32),
                   jax.ShapeDtypeStruct((B_HEADS, w2, w), F32)],
        compiler_params=_params("arbitrary"),
        name="bias_tables",
    )(bucket, rel_bias)


def _swa_prompt_kernel(q_ref, kp_ref, kc_ref, vp_ref, vc_ref, bias_ref, sink_ref, o_ref):
    n = pl.program_id(1)
    w = WINDOW
    q = q_ref[...].astype(BF16)
    kk = jnp.concatenate([kp_ref[...], kc_ref[...]], axis=0).astype(BF16)
    vv = jnp.concatenate([vp_ref[...], vc_ref[...]], axis=0).astype(BF16)
    qi = lax.broadcasted_iota(jnp.int32, (w, 2 * w), 0)
    kj = lax.broadcasted_iota(jnp.int32, (w, 2 * w), 1)
    dist = qi + w - kj
    valid = (dist >= 0) & (dist < w) & ((kj >= w) | (n > 0))
    for h in range(B_HEADS):
        kvh = h // B_GROUP
        qh = q[:, h * B_HEAD_DIM:(h + 1) * B_HEAD_DIM]
        kh = kk[:, kvh * B_HEAD_DIM:(kvh + 1) * B_HEAD_DIM]
        vh = vv[:, kvh * B_HEAD_DIM:(kvh + 1) * B_HEAD_DIM]
        s = lax.dot_general(qh, kh, (((1,), (1,)), ((), ())), preferred_element_type=F32)
        s = jnp.where(valid, s * (B_HEAD_DIM ** -0.5) + bias_ref[h], NEG_INF)
        sink = sink_ref[0, h]
        m = jnp.maximum(jnp.max(s, axis=1, keepdims=True), sink)
        p = jnp.exp(s - m)
        den = jnp.sum(p, axis=1, keepdims=True) + jnp.exp(sink - m)
        p = (p / den).astype(BF16)
        o_ref[:, h * B_HEAD_DIM:(h + 1) * B_HEAD_DIM] = jnp.dot(p, vh, preferred_element_type=F32)


def _swa_prompt(qkv, bias, sinks, batch, length):
    nb = length // WINDOW
    kvw = B_KV_HEADS * B_HEAD_DIM
    kcol = D_MODEL // kvw
    cur = lambda b, n: b * nb + n
    prev = lambda b, n: b * nb + jnp.maximum(n - 1, 0)
    return pl.pallas_call(
        _swa_prompt_kernel,
        grid=(batch, nb),
        in_specs=[pl.BlockSpec((WINDOW, D_MODEL), lambda b, n: (cur(b, n), 0)),
                  pl.BlockSpec((WINDOW, kvw), lambda b, n: (prev(b, n), kcol)),
                  pl.BlockSpec((WINDOW, kvw), lambda b, n: (cur(b, n), kcol)),
                  pl.BlockSpec((WINDOW, kvw), lambda b, n: (prev(b, n), kcol + 1)),
                  pl.BlockSpec((WINDOW, kvw), lambda b, n: (cur(b, n), kcol + 1)),
                  pl.BlockSpec((B_HEADS, WINDOW, 2 * WINDOW), lambda b, n: (0, 0, 0)),
                  pl.BlockSpec(memory_space=pltpu.SMEM)],
        out_specs=pl.BlockSpec((WINDOW, D_MODEL), lambda b, n: (cur(b, n), 0)),
        out_shape=jax.ShapeDtypeStruct((batch * length, D_MODEL), F32),
        compiler_params=_params("parallel", "arbitrary"),
        name="swa_prompt",
    )(qkv, qkv, qkv, qkv, qkv, bias, sinks)


DECODE_BATCH_BLOCK = 8


def _swa_decode_kernel(qkv_ref, ck_ref, cv_ref, bias_t_ref, sink_ref, o_ref):
    w = WINDOW
    kvw = B_KV_HEADS * B_HEAD_DIM
    pos = lax.broadcasted_iota(jnp.int32, (w, 1), 0)
    for bi in range(DECODE_BATCH_BLOCK):
        row = qkv_ref[bi:bi + 1, :]
        for h in range(B_HEADS):
            kvh = h // B_GROUP
            kv_cols = slice(kvh * B_HEAD_DIM, (kvh + 1) * B_HEAD_DIM)
            q = row[:, h * B_HEAD_DIM:(h + 1) * B_HEAD_DIM]
            k_new = row[:, D_MODEL + kvh * B_HEAD_DIM:D_MODEL + (kvh + 1) * B_HEAD_DIM]
            v_new = row[:, D_MODEL + kvw + kvh * B_HEAD_DIM:D_MODEL + kvw + (kvh + 1) * B_HEAD_DIM]
            kc = ck_ref[bi, :, kv_cols]
            vc = cv_ref[bi, :, kv_cols]
            scale = B_HEAD_DIM ** -0.5
            s_c = jnp.sum(kc * q, axis=1, keepdims=True) * scale + bias_t_ref[h, 0:w, 0:1]
            s_c = jnp.where(pos >= 1, s_c, NEG_INF)
            s_n = jnp.sum(k_new * q, axis=1, keepdims=True) * scale + bias_t_ref[h, w:w + 1, 0:1]
            sink = sink_ref[0, h]
            m = jnp.maximum(jnp.maximum(jnp.max(s_c, axis=0, keepdims=True), s_n), sink)
            p_c = jnp.exp(s_c - m)
            p_n = jnp.exp(s_n - m)
            den = jnp.sum(p_c, axis=0, keepdims=True) + p_n + jnp.exp(sink - m)
            o = (jnp.sum(p_c * vc, axis=0, keepdims=True) + p_n * v_new) / den
            o_ref[bi:bi + 1, h * B_HEAD_DIM:(h + 1) * B_HEAD_DIM] = o


def _swa_decode(qkv, cache_k, cache_v, bias_t, sinks):
    b = qkv.shape[0]
    kvw = B_KV_HEADS * B_HEAD_DIM
    bb = DECODE_BATCH_BLOCK
    return pl.pallas_call(
        _swa_decode_kernel,
        grid=(b // bb,),
        in_specs=[pl.BlockSpec((bb, qkv.shape[1]), lambda i: (i, 0)),
                  pl.BlockSpec((bb, WINDOW, kvw), lambda i: (i, 0, 0)),
                  pl.BlockSpec((bb, WINDOW, kvw), lambda i: (i, 0, 0)),
                  pl.BlockSpec((B_HEADS, 2 * WINDOW, WINDOW), lambda i: (0, 0, 0)),
                  pl.BlockSpec(memory_space=pltpu.SMEM)],
        out_specs=pl.BlockSpec((bb, D_MODEL), lambda i: (i, 0)),
        out_shape=jax.ShapeDtypeStruct((b, D_MODEL), F32),
        compiler_params=_params("parallel"),
        name="swa_decode",
    )(qkv, cache_k, cache_v, bias_t, sinks)


_CANDIDATES = [(i, j) for i in range(P_TOPK) for j in range(P_TOPK)
               if (i + 1) * (j + 1) <= P_TOPK]
_BIG_POS = float(4 * P_TOPK * P_TOPK)
_HK = P_HEADS * N_KEYS


def _extract_top(s_ref, r_ref, top_ref):
    ph = P_HEADS
    for k in range(N_KEYS):
        r_ref[k * ph:(k + 1) * ph, :] = jnp.full((ph, LANES), NOT_SELECTED_RANK, F32)

    def body(r, carry):
        best = _tree(jnp.maximum, [s_ref[k * ph:(k + 1) * ph, :] for k in range(N_KEYS)])
        first = _tree(jnp.minimum,
                      [jnp.where(s_ref[k * ph:(k + 1) * ph, :] == best, float(k), float(N_KEYS))
                       for k in range(N_KEYS)])
        rank = r.astype(F32)
        for k in range(N_KEYS):
            rows = slice(k * ph, (k + 1) * ph)
            hit = first == float(k)
            s_ref[rows, :] = jnp.where(hit, -jnp.inf, s_ref[rows, :])
            r_ref[rows, :] = jnp.where(hit, rank, r_ref[rows, :])
        top_ref[r] = best
        return carry

    lax.fori_loop(0, P_TOPK, body, 0)


def _peer_topk_kernel(x_ref, wq_ref, wbd_ref, l1_ref, w1_ref, r2_ref, e2_ref,
                      s_scr, sc_scr, r1_scr, r2_scr, a_scr, b_scr):
    ph = P_HEADS
    xb = x_ref[...].astype(BF16)
    qt = lax.dot_general(wq_ref[...], xb, (((1,), (1,)), ((), ())), preferred_element_type=F32)
    for p in range(2):
        s_scr[p] = jnp.dot(wbd_ref[p], qt[p * _HK:(p + 1) * _HK].astype(BF16),
                           preferred_element_type=F32)
    sc_scr[...] = s_scr[0]
    _extract_top(sc_scr, r1_scr, a_scr)
    sc_scr[...] = s_scr[1]
    _extract_top(sc_scr, r2_scr, b_scr)

    a = [a_scr[i] for i in range(P_TOPK)]
    b = [b_scr[j] for j in range(P_TOPK)]
    cand = {ij: a[ij[0]] + b[ij[1]] for ij in _CANDIDATES}
    pos = {ij: float(ij[0] * P_TOPK + ij[1]) for ij in _CANDIDATES}
    work = dict(cand)
    tau = tau_pos = None
    for r in range(P_TOPK):
        tau = _tree(jnp.maximum, [work[ij] for ij in _CANDIDATES])
        tau_pos = _tree(jnp.minimum,
                        [jnp.where(work[ij] == tau, pos[ij], _BIG_POS) for ij in _CANDIDATES])
        if r < P_TOPK - 1:
            for ij in _CANDIDATES:
                work[ij] = jnp.where(tau_pos == pos[ij], -jnp.inf, work[ij])
    ea = [jnp.exp(a[i] - a[0]) for i in range(P_TOPK)]
    eb = [jnp.exp(b[j] - b[0]) for j in range(P_TOPK)]
    count = [jnp.zeros((ph, LANES), F32) for _ in range(P_TOPK)]
    z = jnp.zeros((ph, LANES), F32)
    for ij in _CANDIDATES:
        i, j = ij
        chosen = jnp.where(cand[ij] > tau, 1.0,
                           jnp.where(cand[ij] == tau,
                                     jnp.where(tau_pos >= pos[ij], 1.0, 0.0), 0.0))
        count[i] = count[i] + chosen
        z = z + chosen * (ea[i] * eb[j])
    inv_z = 1.0 / z
    for k in range(N_KEYS):
        rows = slice(k * ph, (k + 1) * ph)
        r1 = r1_scr[rows, :]
        reach = jnp.zeros((ph, LANES), F32)
        for i in range(P_TOPK):
            reach = jnp.where(r1 == float(i), count[i], reach)
        l1_ref[rows, :] = reach
        w1_ref[rows, :] = jnp.exp(s_scr[0, rows, :] - a[0]) * inv_z
        sc_scr[rows, :] = jnp.exp(s_scr[1, rows, :] - b[0])
    for h in range(ph):
        r2_ref[h] = r2_scr[pl.ds(h, N_KEYS, stride=ph), :]
        e2_ref[h] = sc_scr[pl.ds(h, N_KEYS, stride=ph), :]


def _peer_topk(x, wq_t, wbd):
    t = x.shape[0]
    tile = pl.BlockSpec((_HK, LANES), lambda i: (0, i))
    tile3 = pl.BlockSpec((P_HEADS, N_KEYS, LANES), lambda i: (0, 0, i))
    flat = jax.ShapeDtypeStruct((_HK, t), F32)
    cube = jax.ShapeDtypeStruct((P_HEADS, N_KEYS, t), F32)
    return pl.pallas_call(
        _peer_topk_kernel,
        grid=(t // LANES,),
        in_specs=[pl.BlockSpec((LANES, D_MODEL), lambda i: (i, 0)),
                  pl.BlockSpec((2 * _HK, D_MODEL), lambda i: (0, 0)),
                  pl.BlockSpec((2, _HK, _HK), lambda i: (0, 0, 0))],
        out_specs=[tile, tile, tile3, tile3],
        out_shape=[flat, flat, cube, cube],
        scratch_shapes=[pltpu.VMEM((2, _HK, LANES), F32),
                        pltpu.VMEM((_HK, LANES), F32),
                        pltpu.VMEM((_HK, LANES), F32),
                        pltpu.VMEM((_HK, LANES), F32),
                        pltpu.VMEM((P_TOPK, P_HEADS, LANES), F32),
                        pltpu.VMEM((P_TOPK, P_HEADS, LANES), F32)],
        compiler_params=_params("parallel"),
        name="peer_topk",
    )(x, wq_t, wbd)


def _peer_mix_kernel(x_ref, u_ref, vt_ref, l1_ref, w1_ref, r2_ref, e2_ref, g_ref, b_ref,
                     out_ref, xt_scr, acc_scr, p_scr):
    e = pl.program_id(1)
    tm = x_ref.shape[0]

    @pl.when(e == 0)
    def _():
        xt_scr[...] = x_ref[...].T.astype(BF16)
        acc_scr[...] = jnp.zeros(acc_scr.shape, F32)

    act = jnp.dot(u_ref[...], xt_scr[...], preferred_element_type=F32)
    for jj in range(PEER_ROWS_PER_STEP):
        rows = slice(jj * N_KEYS, (jj + 1) * N_KEYS)
        for c0 in range(0, tm, LANES):
            cols = slice(c0, c0 + LANES)
            gate = jnp.zeros((N_KEYS, LANES), F32)
            for h in range(P_HEADS):
                r = jj * P_HEADS + h
                reach = l1_ref[r:r + 1, cols]
                gate = gate + jnp.where(r2_ref[h, :, cols] < reach, e2_ref[h, :, cols], 0.0) \
                    * w1_ref[r:r + 1, cols]
            a = act[rows, cols]
            gelu = 0.5 * a * (1.0 + lax.erf(a * (2.0 ** -0.5)))
            p_scr[rows, cols] = (gelu * gate).astype(BF16)
    acc_scr[...] += jnp.dot(vt_ref[...], p_scr[...], preferred_element_type=F32)

    @pl.when(e == pl.num_programs(1) - 1)
    def _():
        y = DN_ALPHA * x_ref[...] + acc_scr[...].T
        out_ref[...] = _layer_norm(y, g_ref[...], b_ref[...])


def _peer_mix(x, u, vt, l1, w1, r2, e2, g, b, tm):
    t = x.shape[0]
    te = PEER_ROWS_PER_STEP * N_KEYS
    tr = PEER_ROWS_PER_STEP * P_HEADS
    return pl.pallas_call(
        _peer_mix_kernel,
        grid=(t // tm, N_EXPERTS // te),
        in_specs=[pl.BlockSpec((tm, D_MODEL), lambda i, e: (i, 0)),
                  pl.BlockSpec((te, D_MODEL), lambda i, e: (e, 0)),
                  pl.BlockSpec((D_MODEL, te), lambda i, e: (0, e)),
                  pl.BlockSpec((tr, tm), lambda i, e: (e, i)),
                  pl.BlockSpec((tr, tm), lambda i, e: (e, i)),
                  pl.BlockSpec((P_HEADS, N_KEYS, tm), lambda i, e: (0, 0, i)),
                  pl.BlockSpec((P_HEADS, N_KEYS, tm), lambda i, e: (0, 0, i)),
                  pl.BlockSpec((1, D_MODEL), lambda i, e: (0, 0)),
                  pl.BlockSpec((1, D_MODEL), lambda i, e: (0, 0))],
        out_specs=pl.BlockSpec((tm, D_MODEL), lambda i, e: (i, 0)),
        out_shape=jax.ShapeDtypeStruct((t, D_MODEL), F32),
        scratch_shapes=[pltpu.VMEM((D_MODEL, tm), BF16),
                        pltpu.VMEM((D_MODEL, tm), F32),
                        pltpu.VMEM((te, tm), BF16)],
        compiler_params=_params("parallel", "arbitrary"),
        name="peer_mix",
    )(x, u, vt, l1, w1, r2, e2, g, b)


def _peer_layer(x, pw, g, b):
    t = x.shape[0]
    l1, w1, r2, e2 = _peer_topk(x, pw["wq_t"], pw["wbd"])
    tm = PEER_TOKEN_TILE if t % PEER_TOKEN_TILE == 0 else LANES
    return _peer_mix(x, pw["u"], pw["vt"], l1, w1, r2, e2, g, b, tm)


def _t5_bucket(dist):
    max_exact = N_BUCKETS // 2
    d = jnp.maximum(dist, 0)
    df = jnp.maximum(d, 1).astype(F32)
    large = max_exact + (jnp.log(df / max_exact) / math.log(MAX_DISTANCE / max_exact)
                         * (N_BUCKETS - max_exact)).astype(jnp.int32)
    large = jnp.minimum(large, N_BUCKETS - 1)
    return jnp.where(d < max_exact, d, large)


def _row(a):
    return a.reshape(1, -1)


def _prepare(w):
    n_gate = 2 * A_HEADS
    w_in = w["a_w_in"][0]
    b_in = w["a_b_in"][0]
    prep = {
        "w_in_main": w_in[:, :2 * A_INNER].astype(BF16),
        "b_in_main": _row(b_in[:2 * A_INNER]),
        "w_in_gate": jnp.pad(w_in[:, 2 * A_INNER:], ((0, 0), (0, LANES - n_gate))).astype(BF16),
        "b_in_gate": _row(jnp.pad(b_in[2 * A_INNER:], (0, LANES - n_gate))),
        "conv_w": w["a_conv_w"][0],
        "conv_b": _row(w["a_conv_b"][0]),
        "wq": w["a_w_q"][0].astype(BF16),
        "wk": w["a_w_k"][0].astype(BF16),
        "wv": w["a_w_v"][0].astype(BF16),
        "w_out": w["a_w_out"][0].astype(BF16),
        "w_qkv": jnp.concatenate([w["b_w_q"][0], w["kv_w"]], axis=1).astype(BF16),
        "w_o": w["b_w_o"][0].astype(BF16),
        "sinks": _row(w["b_sinks"][0]),
        "rel_bias": w["rel_bias"],
    }
    peer = []
    eye = jnp.eye(P_HEADS, dtype=F32)
    half = D_KEY // 2
    for layer in range(DEPTH):
        wq = w["peer_w_q"][layer].reshape(D_MODEL, P_HEADS, 2, half)
        sk = w["peer_subkeys"][layer]
        peer.append({
            "wq_t": wq.transpose(2, 1, 3, 0).reshape(2 * _HK, D_MODEL).astype(BF16),
            "wbd": jnp.einsum("hpkc,hg->pkhgc", sk, eye).reshape(2, _HK, _HK).astype(BF16),
            "u": w["peer_u"][layer].astype(BF16),
            "vt": w["peer_v"][layer].T.astype(BF16),
        })
    prep["peer"] = peer
    return prep


def _zero_bias(n):
    return jnp.zeros((1, n), F32)


def _attention_tables(rel_bias):
    qi = jnp.arange(WINDOW)[:, None]
    kj = jnp.arange(2 * WINDOW)[None, :]
    bucket = _t5_bucket(qi + WINDOW - kj).astype(jnp.int32)
    return _bias_tables(bucket, rel_bias)


def _prompt_trunk(x, w, p, bias):
    batch, length, d = x.shape
    t = batch * length
    xt = x.reshape(t, d)
    proj = _linear(xt, p["w_in_main"], p["b_in_main"], 512, 512)
    gates = _linear(xt, p["w_in_gate"], p["b_in_gate"], 512, LANES)
    row_tab, col_tab = _gate_tables(gates, batch, length, MLSTM_CHUNK)
    q, k, v = _conv_qkv(proj, p["conv_w"], p["conv_b"], p["wq"], p["wk"], p["wv"], batch, length)
    h, c_new, n_new, m_new = _mlstm_prompt(q, k, v, row_tab, col_tab, batch, length, MLSTM_CHUNK)
    x1 = _outproj_ln(h, proj, p["w_out"], xt, _row(w["ln_mix_g"][0]), _row(w["ln_mix_b"][0]), 256)
    x2 = _peer_layer(x1, p["peer"][0], _row(w["ln_ffn_g"][0]), _row(w["ln_ffn_b"][0]))
    qkv = _linear(x2, p["w_qkv"], _zero_bias(p["w_qkv"].shape[1]), 512, 512)
    o = _swa_prompt(qkv, bias, p["sinks"], batch, length)
    x3 = _outproj_ln(o, None, p["w_o"], x2, _row(w["ln_mix_g"][1]), _row(w["ln_mix_b"][1]), 256)
    x4 = _peer_layer(x3, p["peer"][1], _row(w["ln_ffn_g"][1]), _row(w["ln_ffn_b"][1]))

    kvw = B_KV_HEADS * B_HEAD_DIM
    qkv3 = qkv.reshape(batch, length, -1)
    k_win = qkv3[:, -WINDOW:, D_MODEL:D_MODEL + kvw].reshape(batch, WINDOW, B_KV_HEADS, B_HEAD_DIM)
    v_win = qkv3[:, -WINDOW:, D_MODEL + kvw:].reshape(batch, WINDOW, B_KV_HEADS, B_HEAD_DIM)
    conv = proj.reshape(batch, length, -1)[:, -(A_CONV_W - 1):, :A_INNER]
    return (x4.reshape(batch, length, d),
            c_new[None],
            n_new.reshape(1, batch, A_HEADS, A_HEAD_DIM),
            m_new[:, :, 0, 0][None],
            conv[None], k_win, v_win)


def _sample_trunk(x, conv0, c0, n0, m0, k_buf, v_buf, w, p, bias_t):
    batch, length, d = x.shape
    xt = x.reshape(batch, d)
    proj = _linear(xt, p["w_in_main"], p["b_in_main"], batch, 512)
    gates = _linear(xt, p["w_in_gate"], p["b_in_gate"], batch, LANES)
    buf = conv0[0]
    q, k, v, qt, kt = _sample_conv_qkv(proj, buf.reshape(batch, -1), p["conv_w"], p["conv_b"],
                                       p["wq"], p["wk"], p["wv"])
    h, c_new, n_new, m_new = _sample_mlstm(q, k, v, qt, kt, gates, m0[0], n0[0], c0[0])
    x1 = _outproj_ln(h.reshape(batch, A_INNER), proj, p["w_out"], xt,
                     _row(w["ln_mix_g"][0]), _row(w["ln_mix_b"][0]), batch)
    x2 = _peer_layer(x1, p["peer"][0], _row(w["ln_ffn_g"][0]), _row(w["ln_ffn_b"][0]))
    qkv = _linear(x2, p["w_qkv"], _zero_bias(p["w_qkv"].shape[1]), batch, 512)
    kvw = B_KV_HEADS * B_HEAD_DIM
    o = _swa_decode(qkv, k_buf.reshape(batch, WINDOW, kvw), v_buf.reshape(batch, WINDOW, kvw),
                    bias_t, p["sinks"])
    x3 = _outproj_ln(o, None, p["w_o"], x2, _row(w["ln_mix_g"][1]), _row(w["ln_mix_b"][1]), batch)
    x4 = _peer_layer(x3, p["peer"][1], _row(w["ln_ffn_g"][1]), _row(w["ln_ffn_b"][1]))

    k_new = qkv[:, D_MODEL:D_MODEL + kvw].reshape(batch, 1, B_KV_HEADS, B_HEAD_DIM)
    v_new = qkv[:, D_MODEL + kvw:].reshape(batch, 1, B_KV_HEADS, B_HEAD_DIM)
    k_win = jnp.concatenate([k_buf[:, 1:], k_new], axis=1)
    v_win = jnp.concatenate([v_buf[:, 1:], v_new], axis=1)
    conv = jnp.concatenate([buf[:, 1:], proj[:, None, :A_INNER]], axis=1)
    return (x4.reshape(batch, length, d),
            c_new[None],
            n_new.reshape(1, batch, A_HEADS, A_HEAD_DIM),
            m_new[:, :, 0, 0][None],
            conv[None], k_win, v_win)


def kernel(x_prompt, x_sample, state_mlstm_C, state_mlstm_n, state_mlstm_m, state_mlstm_conv,
           cache_k_win, cache_v_win, a_w_in, a_b_in, a_conv_w, a_conv_b, a_w_q, a_w_k, a_w_v,
           a_w_out, kv_w, b_w_q, b_w_o, b_sinks, rel_bias, ln_mix_g, ln_mix_b, ln_ffn_g,
           ln_ffn_b, peer_w_q, peer_subkeys, peer_u, peer_v):
    w = {"a_w_in": a_w_in, "a_b_in": a_b_in, "a_conv_w": a_conv_w, "a_conv_b": a_conv_b,
         "a_w_q": a_w_q, "a_w_k": a_w_k, "a_w_v": a_w_v, "a_w_out": a_w_out, "kv_w": kv_w,
         "b_w_q": b_w_q, "b_w_o": b_w_o, "b_sinks": b_sinks, "rel_bias": rel_bias,
         "ln_mix_g": ln_mix_g, "ln_mix_b": ln_mix_b, "ln_ffn_g": ln_ffn_g, "ln_ffn_b": ln_ffn_b,
         "peer_w_q": peer_w_q, "peer_subkeys": peer_subkeys, "peer_u": peer_u, "peer_v": peer_v}
    p = _prepare(w)
    bias, bias_t = _attention_tables(rel_bias)
    prompt = _prompt_trunk(x_prompt, w, p, bias)
    sample = _sample_trunk(x_sample, state_mlstm_conv, state_mlstm_C, state_mlstm_n,
                           state_mlstm_m, cache_k_win, cache_v_win, w, p, bias_t)
    return (prompt[0], sample[0]) + prompt[1:] + sample[1:]
```

```python
import functools
import math

import jax
import jax.numpy as jnp
from jax import lax
from jax.experimental import pallas as pl
from jax.experimental.pallas import tpu as pltpu

D_MODEL = 1024
DEPTH = 2
A_HEADS = 4
A_INNER = 2 * D_MODEL
A_HEAD_DIM = A_INNER // A_HEADS
A_CONV_W = 4
B_HEADS = 16
B_KV_HEADS = 4
B_GROUP = B_HEADS // B_KV_HEADS
B_HEAD_DIM = D_MODEL // B_HEADS
WINDOW = 128
N_BUCKETS = 32
MAX_DISTANCE = 128
P_HEADS = 8
N_KEYS = 128
N_EXPERTS = N_KEYS * N_KEYS
D_KEY = 256
P_TOPK = 16
DN_ALPHA = (2 * DEPTH) ** 0.25
LN_EPS = 1e-5
NEG_INF = -1e30

LANES = 128
SUBLANES = 8
BF16_ROWS = 2 * SUBLANES
VMEM_LIMIT_BYTES = 48 * 1024 * 1024

MLSTM_CHUNK = 256
PEER_TOKEN_TILE = 1024
PEER_ROWS_PER_STEP = 4
PEER_COLUMN_GROUP = 256
PEER_KEY_BLOCK = 32
NOT_SELECTED_RANK = float(P_TOPK)

F32 = jnp.float32
BF16 = jnp.bfloat16


def _params(*sem):
    return pltpu.CompilerParams(dimension_semantics=sem, vmem_limit_bytes=VMEM_LIMIT_BYTES)


def _tree(op, vals):
    vals = list(vals)
    while len(vals) > 1:
        nxt = [op(vals[i], vals[i + 1]) for i in range(0, len(vals) - 1, 2)]
        if len(vals) % 2:
            nxt.append(vals[-1])
        vals = nxt
    return vals[0]


def _layer_norm(y, g, b):
    mu = jnp.mean(y, axis=-1, keepdims=True)
    yc = y - mu
    var = jnp.mean(yc * yc, axis=-1, keepdims=True)
    return yc * lax.rsqrt(var + LN_EPS) * g + b


def _linear_kernel(x_ref, w_ref, b_ref, o_ref):
    x = x_ref[...].astype(BF16)
    o_ref[...] = jnp.dot(x, w_ref[...], preferred_element_type=F32) + b_ref[...]


def _linear(x, w, b, tm, tn):
    t, k = x.shape
    n = w.shape[1]
    return pl.pallas_call(
        _linear_kernel,
        grid=(t // tm, n // tn),
        in_specs=[pl.BlockSpec((tm, k), lambda i, j: (i, 0)),
                  pl.BlockSpec((k, tn), lambda i, j: (0, j)),
                  pl.BlockSpec((1, tn), lambda i, j: (0, j))],
        out_specs=pl.BlockSpec((tm, tn), lambda i, j: (i, j)),
        out_shape=jax.ShapeDtypeStruct((t, n), F32),
        compiler_params=_params("parallel", "arbitrary"),
        name="linear",
    )(x, w, b)


def _outproj_ln_kernel(gated, *refs):
    if gated:
        a_ref, o_ref, w_ref, res_ref, g_ref, b_ref, out_ref = refs
        act = jax.nn.sigmoid(o_ref[...]) * a_ref[...]
    else:
        a_ref, w_ref, res_ref, g_ref, b_ref, out_ref = refs
        act = a_ref[...]
    sub = jnp.dot(act.astype(BF16), w_ref[...], preferred_element_type=F32)
    out_ref[...] = _layer_norm(DN_ALPHA * res_ref[...] + sub, g_ref[...], b_ref[...])


def _outproj_ln(act, gate_src, w, res, g, b, tm):
    t, k = act.shape
    d = w.shape[1]
    gated = gate_src is not None
    in_specs = [pl.BlockSpec((tm, k), lambda i: (i, 0))]
    args = [act]
    if gated:
        in_specs.append(pl.BlockSpec((tm, k), lambda i: (i, 1)))
        args.append(gate_src)
    in_specs += [pl.BlockSpec((k, d), lambda i: (0, 0)),
                 pl.BlockSpec((tm, d), lambda i: (i, 0)),
                 pl.BlockSpec((1, d), lambda i: (0, 0)),
                 pl.BlockSpec((1, d), lambda i: (0, 0))]
    args += [w, res, g, b]
    return pl.pallas_call(
        functools.partial(_outproj_ln_kernel, gated),
        grid=(t // tm,),
        in_specs=in_specs,
        out_specs=pl.BlockSpec((tm, d), lambda i: (i, 0)),
        out_shape=jax.ShapeDtypeStruct((t, d), F32),
        compiler_params=_params("parallel"),
        name="outproj_ln",
    )(*args)


def _log_sigmoid(x):
    return jnp.minimum(x, 0.0) - jnp.log1p(jnp.exp(-jnp.abs(x)))


def _gates_kernel(chunk, g_ref, row_ref, col_ref):
    length = g_ref.shape[0]
    gt = g_ref[...].T
    top = gt[0:SUBLANES]
    row = lax.broadcasted_iota(jnp.int32, top.shape, 0)
    pos = lax.broadcasted_iota(jnp.int32, top.shape, 1) % chunk
    is_f = row >= A_HEADS
    x = jnp.where(is_f, _log_sigmoid(top), 0.0)
    shift = 1
    while shift < chunk:
        x = x + jnp.where(pos >= shift, pltpu.roll(x, shift, 1), 0.0)
        shift *= 2
    table = jnp.where(is_f, x, top)
    row_ref[...] = table
    padded = jnp.concatenate([table, jnp.zeros((LANES - SUBLANES, length), F32)], axis=0)
    col_ref[...] = padded.T


def _gate_tables(gates, batch, length, chunk):
    return pl.pallas_call(
        functools.partial(_gates_kernel, chunk),
        grid=(batch,),
        in_specs=[pl.BlockSpec((length, LANES), lambda b: (b, 0))],
        out_specs=[pl.BlockSpec((None, SUBLANES, length), lambda b: (b, 0, 0)),
                   pl.BlockSpec((length, LANES), lambda b: (b, 0))],
        out_shape=[jax.ShapeDtypeStruct((batch, SUBLANES, length), F32),
                   jax.ShapeDtypeStruct((batch * length, LANES), F32)],
        compiler_params=_params("parallel"),
        name="gate_tables",
    )(gates)


CONV_ROW_CHUNK = 512
CONV_PAD = SUBLANES


def _conv_qkv_kernel(xm_ref, cw_ref, cb_ref, wq_ref, wk_ref, wv_ref,
                     q_ref, k_ref, v_ref, pad_ref):
    length = xm_ref.shape[0]
    pad_ref[0:CONV_PAD, :] = jnp.zeros((CONV_PAD, A_HEAD_DIM), F32)
    pad_ref[CONV_PAD:CONV_PAD + length, :] = xm_ref[...]
    first = CONV_PAD - (A_CONV_W - 1)
    for c0 in range(0, length, CONV_ROW_CHUNK):
        acc = cb_ref[...]
        for w in range(A_CONV_W):
            acc = acc + pad_ref[c0 + first + w:c0 + first + w + CONV_ROW_CHUNK, :] * cw_ref[w:w + 1, :]
        xc = (acc * jax.nn.sigmoid(acc)).astype(BF16)
        xm = xm_ref[c0:c0 + CONV_ROW_CHUNK, :].astype(BF16)
        rows = slice(c0, c0 + CONV_ROW_CHUNK)
        q_ref[rows, :] = jnp.dot(xc, wq_ref[...], preferred_element_type=F32).astype(BF16)
        k = jnp.dot(xc, wk_ref[...], preferred_element_type=F32) * (A_HEAD_DIM ** -0.5)
        k_ref[rows, :] = k.astype(BF16)
        v_ref[rows, :] = jnp.dot(xm, wv_ref[...], preferred_element_type=F32).astype(BF16)


def _conv_qkv(proj, cw, cb, wq, wk, wv, batch, length):
    t = batch * length
    hd = A_HEAD_DIM
    tok = pl.BlockSpec((length, hd), lambda b, h: (b, h))
    wspec = pl.BlockSpec((None, hd, hd), lambda b, h: (h, 0, 0))
    out = jax.ShapeDtypeStruct((t, A_INNER), BF16)
    return pl.pallas_call(
        _conv_qkv_kernel,
        grid=(batch, A_HEADS),
        in_specs=[tok,
                  pl.BlockSpec((A_CONV_W, hd), lambda b, h: (0, h)),
                  pl.BlockSpec((1, hd), lambda b, h: (0, h)),
                  wspec, wspec, wspec],
        out_specs=[tok, tok, tok],
        out_shape=[out, out, out],
        scratch_shapes=[pltpu.VMEM((length + CONV_PAD, hd), F32)],
        compiler_params=_params("parallel", "arbitrary"),
        name="conv_qkv",
    )(proj, cw, cb, wq, wk, wv)


def _mlstm_kernel(q_ref, k_ref, v_ref, row_ref, col_ref,
                  h_ref, c_out_ref, n_out_ref, m_out_ref,
                  c_scr, n_scr, m_scr):
    ci = pl.program_id(1)
    chunk = q_ref.shape[0]
    hd = A_HEAD_DIM

    @pl.when(ci == 0)
    def _():
        c_scr[...] = jnp.zeros(c_scr.shape, F32)
        n_scr[...] = jnp.zeros(n_scr.shape, F32)
        m_scr[...] = jnp.zeros(m_scr.shape, F32)

    t_idx = lax.broadcasted_iota(jnp.int32, (chunk, chunk), 0)
    s_idx = lax.broadcasted_iota(jnp.int32, (chunk, chunk), 1)
    causal = s_idx <= t_idx
    for h in range(A_HEADS):
        cols = slice(h * hd, (h + 1) * hd)
        qh, kh, vh = q_ref[:, cols], k_ref[:, cols], v_ref[:, cols]
        i_col = col_ref[:, h:h + 1]
        f_col = col_ref[:, A_HEADS + h:A_HEADS + h + 1]
        i_row = row_ref[h:h + 1, :]
        f_row = row_ref[A_HEADS + h:A_HEADS + h + 1, :]
        m_prev = m_scr[h][:, 0:1]
        d = jnp.where(causal, f_col - f_row + i_row, NEG_INF)
        b_inter = f_col + m_prev
        m_t = jnp.maximum(b_inter, jnp.max(d, axis=1, keepdims=True))
        qk = lax.dot_general(qh, kh, (((1,), (1,)), ((), ())), preferred_element_type=F32)
        s = qk * jnp.exp(d - m_t)
        w_inter = jnp.exp(b_inter - m_t)
        q_c = jnp.dot(qh, c_scr[h].astype(BF16), preferred_element_type=F32)
        num = jnp.dot(s.astype(BF16), vh, preferred_element_type=F32) + w_inter * q_c
        q_n = jnp.sum(qh.astype(F32) * n_scr[h], axis=1, keepdims=True)
        den = jnp.sum(s, axis=1, keepdims=True) + w_inter * q_n
        h_ref[:, cols] = num / jnp.maximum(jnp.abs(den), jnp.exp(-m_t))
        f_last = f_col[chunk - 1:chunk, :]
        g = f_last - f_col + i_col
        m_new = jnp.maximum(f_last + m_prev, jnp.max(g, axis=0, keepdims=True))
        decay = jnp.exp(f_last + m_prev - m_new)
        wk = jnp.exp(g - m_new) * kh.astype(F32)
        kv = lax.dot_general(wk.astype(BF16), vh, (((0,), (0,)), ((), ())),
                             preferred_element_type=F32)
        c_scr[h] = decay * c_scr[h] + kv
        n_scr[h] = decay * n_scr[h] + jnp.sum(wk, axis=0, keepdims=True)
        m_scr[h] = jnp.broadcast_to(m_new, (1, LANES))

    @pl.when(ci == pl.num_programs(1) - 1)
    def _():
        c_out_ref[...] = c_scr[...]
        n_out_ref[...] = n_scr[...]
        m_out_ref[...] = m_scr[...]


def _mlstm_prompt(q, k, v, row_tab, col_tab, batch, length, chunk):
    nc = length // chunk
    t = batch * length
    tok = pl.BlockSpec((chunk, A_INNER), lambda b, c: (b * nc + c, 0))
    return pl.pallas_call(
        _mlstm_kernel,
        grid=(batch, nc),
        in_specs=[tok, tok, tok,
                  pl.BlockSpec((None, SUBLANES, chunk), lambda b, c: (b, 0, c)),
                  pl.BlockSpec((chunk, LANES), lambda b, c: (b * nc + c, 0))],
        out_specs=[tok,
                   pl.BlockSpec((None, A_HEADS, A_HEAD_DIM, A_HEAD_DIM), lambda b, c: (b, 0, 0, 0)),
                   pl.BlockSpec((None, A_HEADS, 1, A_HEAD_DIM), lambda b, c: (b, 0, 0, 0)),
                   pl.BlockSpec((None, A_HEADS, 1, LANES), lambda b, c: (b, 0, 0, 0))],
        out_shape=[jax.ShapeDtypeStruct((t, A_INNER), F32),
                   jax.ShapeDtypeStruct((batch, A_HEADS, A_HEAD_DIM, A_HEAD_DIM), F32),
                   jax.ShapeDtypeStruct((batch, A_HEADS, 1, A_HEAD_DIM), F32),
                   jax.ShapeDtypeStruct((batch, A_HEADS, 1, LANES), F32)],
        scratch_shapes=[pltpu.VMEM((A_HEADS, A_HEAD_DIM, A_HEAD_DIM), F32),
                        pltpu.VMEM((A_HEADS, 1, A_HEAD_DIM), F32),
                        pltpu.VMEM((A_HEADS, 1, LANES), F32)],
        compiler_params=_params("parallel", "arbitrary"),
        name="mlstm_prompt",
    )(q, k, v, row_tab, col_tab)


def _sample_conv_qkv_kernel(proj_ref, buf_ref, cw_ref, cb_ref, wq_ref, wk_ref, wv_ref,
                            q_ref, k_ref, v_ref, qt_ref, kt_ref):
    xm = proj_ref[:, 0:A_INNER]
    acc = cb_ref[...] + xm * cw_ref[A_CONV_W - 1:A_CONV_W, :]
    for w in range(A_CONV_W - 1):
        acc = acc + buf_ref[:, w * A_INNER:(w + 1) * A_INNER] * cw_ref[w:w + 1, :]
    xc = (acc * jax.nn.sigmoid(acc)).astype(BF16)
    xmb = xm.astype(BF16)
    for h in range(A_HEADS):
        cols = slice(h * A_HEAD_DIM, (h + 1) * A_HEAD_DIM)
        q_ref[:, cols] = jnp.dot(xc[:, cols], wq_ref[h], preferred_element_type=F32)
        k_ref[:, cols] = (jnp.dot(xc[:, cols], wk_ref[h], preferred_element_type=F32)
                          * (A_HEAD_DIM ** -0.5))
        v_ref[:, cols] = jnp.dot(xmb[:, cols], wv_ref[h], preferred_element_type=F32)
    qt_ref[...] = q_ref[...].T
    kt_ref[...] = k_ref[...].T


def _sample_conv_qkv(proj, conv_buf_flat, cw, cb, wq, wk, wv):
    b = proj.shape[0]
    row = jax.ShapeDtypeStruct((b, A_INNER), F32)
    col = jax.ShapeDtypeStruct((A_INNER, b), F32)
    return pl.pallas_call(
        _sample_conv_qkv_kernel,
        out_shape=[row, row, row, col, col],
        compiler_params=pltpu.CompilerParams(vmem_limit_bytes=VMEM_LIMIT_BYTES),
        name="sample_conv_qkv",
    )(proj, conv_buf_flat, cw, cb, wq, wk, wv)


def _sample_mlstm_kernel(q_ref, k_ref, v_ref, qt_ref, kt_ref, g_ref, m0_ref, n0_ref, c0_ref,
                         h_ref, c_ref, n_ref, m_ref):
    b = pl.program_id(0)
    h = pl.program_id(1)
    hd = A_HEAD_DIM
    pick = (lax.broadcasted_iota(jnp.int32, (LANES, LANES), 0) == b).astype(F32)
    q_col = jnp.dot(qt_ref[...], pick, precision=lax.Precision.HIGHEST, preferred_element_type=F32)
    k_col = jnp.dot(kt_ref[...], pick, precision=lax.Precision.HIGHEST, preferred_element_type=F32)
    reps = hd // LANES
    q_mat = jnp.concatenate([q_col] * reps, axis=1)
    k_mat = jnp.concatenate([k_col] * reps, axis=1)
    q_row, k_row, v_row = q_ref[...], k_ref[...], v_ref[...]
    lane = lax.broadcasted_iota(jnp.int32, (1, LANES), 1)
    gates = g_ref[...]
    log_i = jnp.sum(jnp.where(lane == h, gates, 0.0), axis=1, keepdims=True)
    f_pre = jnp.sum(jnp.where(lane == h + A_HEADS, gates, 0.0), axis=1, keepdims=True)
    log_f = _log_sigmoid(f_pre)
    lane_h = lax.broadcasted_iota(jnp.int32, (1, A_HEADS), 1)
    m0 = jnp.sum(jnp.where(lane_h == h, m0_ref[...], 0.0), axis=1, keepdims=True)
    m_t = jnp.maximum(log_f + m0, log_i)
    w_inter = jnp.exp(log_f + m0 - m_t)
    w_new = jnp.exp(log_i - m_t)
    c0 = c0_ref[...]
    n0 = n0_ref[...]
    s = jnp.sum(q_row * k_row, axis=1, keepdims=True) * w_new
    q_c = jnp.sum(q_mat * c0, axis=0, keepdims=True)
    q_n = jnp.sum(q_row * n0, axis=1, keepdims=True)
    num = s * v_row + w_inter * q_c
    den = s + w_inter * q_n
    h_ref[...] = num / jnp.maximum(jnp.abs(den), jnp.exp(-m_t))
    c_ref[...] = w_inter * c0 + (w_new * k_mat) * v_row
    n_ref[...] = w_inter * n0 + w_new * k_row
    m_ref[...] = jnp.broadcast_to(m_t, (1, LANES))


def _sample_mlstm(q, k, v, qt, kt, gates, m0, n0, c0):
    b = q.shape[0]
    hd = A_HEAD_DIM
    row3 = lambda a: a.reshape(b, 1, a.shape[-1])
    rspec = pl.BlockSpec((None, 1, hd), lambda i, h: (i, 0, h))
    cspec = pl.BlockSpec((hd, LANES), lambda i, h: (h, 0))
    nspec = pl.BlockSpec((None, None, 1, hd), lambda i, h: (i, h, 0, 0))
    mspec = pl.BlockSpec((None, None, 1, LANES), lambda i, h: (i, h, 0, 0))
    big = pl.BlockSpec((None, None, hd, hd), lambda i, h: (i, h, 0, 0))
    return pl.pallas_call(
        _sample_mlstm_kernel,
        grid=(b, A_HEADS),
        in_specs=[rspec, rspec, rspec, cspec, cspec,
                  pl.BlockSpec((None, 1, LANES), lambda i, h: (i, 0, 0)),
                  pl.BlockSpec((None, 1, A_HEADS), lambda i, h: (i, 0, 0)),
                  nspec, big],
        out_specs=[rspec, big, nspec, mspec],
        out_shape=[jax.ShapeDtypeStruct((b, 1, A_INNER), F32),
                   jax.ShapeDtypeStruct((b, A_HEADS, hd, hd), F32),
                   jax.ShapeDtypeStruct((b, A_HEADS, 1, hd), F32),
                   jax.ShapeDtypeStruct((b, A_HEADS, 1, LANES), F32)],
        compiler_params=_params("parallel", "arbitrary"),
        name="sample_mlstm",
    )(row3(q), row3(k), row3(v), qt, kt, row3(gates), row3(m0),
      n0.reshape(b, A_HEADS, 1, hd), c0)


def _bias_kernel(bucket_ref, rel_ref, bias_ref, bias_t_ref):
    h = pl.program_id(0)
    bucket = bucket_ref[...]
    acc = jnp.zeros(bucket.shape, F32)
    for n in range(N_BUCKETS):
        acc = jnp.where(bucket == n, rel_ref[n, h], acc)
    bias_ref[...] = acc
    bias_t_ref[...] = acc.T


def _bias_tables(bucket, rel_bias):
    w, w2 = bucket.shape
    return pl.pallas_call(
        _bias_kernel,
        grid=(B_HEADS,),
        in_specs=[pl.BlockSpec((w, w2), lambda h: (0, 0)),
                  pl.BlockSpec(memory_space=pltpu.SMEM)],
        out_specs=[pl.BlockSpec((None, w, w2), lambda h: (h, 0, 0)),
                   pl.BlockSpec((None, w2, w), lambda h: (h, 0, 0))],
        out_shape=[jax.ShapeDtypeStruct((B_HEADS, w, w2), F32),
                   jax.ShapeDtypeStruct((B_HEADS, w2, w), F32)],
        compiler_params=_params("arbitrary"),
        name="bias_tables",
    )(bucket, rel_bias)


def _swa_prompt_kernel(q_ref, kp_ref, kc_ref, vp_ref, vc_ref, bias_ref, sink_ref, o_ref):
    n = pl.program_id(1)
    w = WINDOW
    q = q_ref[...].astype(BF16)
    kk = jnp.concatenate([kp_ref[...], kc_ref[...]], axis=0).astype(BF16)
    vv = jnp.concatenate([vp_ref[...], vc_ref[...]], axis=0).astype(BF16)
    qi = lax.broadcasted_iota(jnp.int32, (w, 2 * w), 0)
    kj = lax.broadcasted_iota(jnp.int32, (w, 2 * w), 1)
    dist = qi + w - kj
    valid = (dist >= 0) & (dist < w) & ((kj >= w) | (n > 0))
    for h in range(B_HEADS):
        kvh = h // B_GROUP
        qh = q[:, h * B_HEAD_DIM:(h + 1) * B_HEAD_DIM]
        kh = kk[:, kvh * B_HEAD_DIM:(kvh + 1) * B_HEAD_DIM]
        vh = vv[:, kvh * B_HEAD_DIM:(kvh + 1) * B_HEAD_DIM]
        s = lax.dot_general(qh, kh, (((1,), (1,)), ((), ())), preferred_element_type=F32)
        s = jnp.where(valid, s * (B_HEAD_DIM ** -0.5) + bias_ref[h], NEG_INF)
        sink = sink_ref[0, h]
        m = jnp.maximum(jnp.max(s, axis=1, keepdims=True), sink)
        p = jnp.exp(s - m)
        den = jnp.sum(p, axis=1, keepdims=True) + jnp.exp(sink - m)
        p = (p / den).astype(BF16)
        o_ref[:, h * B_HEAD_DIM:(h + 1) * B_HEAD_DIM] = jnp.dot(p, vh, preferred_element_type=F32)


def _swa_prompt(qkv, bias, sinks, batch, length):
    nb = length // WINDOW
    kvw = B_KV_HEADS * B_HEAD_DIM
    kcol = D_MODEL // kvw
    cur = lambda b, n: b * nb + n
    prev = lambda b, n: b * nb + jnp.maximum(n - 1, 0)
    return pl.pallas_call(
        _swa_prompt_kernel,
        grid=(batch, nb),
        in_specs=[pl.BlockSpec((WINDOW, D_MODEL), lambda b, n: (cur(b, n), 0)),
                  pl.BlockSpec((WINDOW, kvw), lambda b, n: (prev(b, n), kcol)),
                  pl.BlockSpec((WINDOW, kvw), lambda b, n: (cur(b, n), kcol)),
                  pl.BlockSpec((WINDOW, kvw), lambda b, n: (prev(b, n), kcol + 1)),
                  pl.BlockSpec((WINDOW, kvw), lambda b, n: (cur(b, n), kcol + 1)),
                  pl.BlockSpec((B_HEADS, WINDOW, 2 * WINDOW), lambda b, n: (0, 0, 0)),
                  pl.BlockSpec(memory_space=pltpu.SMEM)],
        out_specs=pl.BlockSpec((WINDOW, D_MODEL), lambda b, n: (cur(b, n), 0)),
        out_shape=jax.ShapeDtypeStruct((batch * length, D_MODEL), F32),
        compiler_params=_params("parallel", "arbitrary"),
        name="swa_prompt",
    )(qkv, qkv, qkv, qkv, qkv, bias, sinks)


DECODE_BATCH_BLOCK = 8


def _swa_decode_kernel(qkv_ref, ck_ref, cv_ref, bias_t_ref, sink_ref, o_ref):
    w = WINDOW
    kvw = B_KV_HEADS * B_HEAD_DIM
    pos = lax.broadcasted_iota(jnp.int32, (w, 1), 0)
    for bi in range(DECODE_BATCH_BLOCK):
        row = qkv_ref[bi:bi + 1, :]
        for h in range(B_HEADS):
            kvh = h // B_GROUP
            kv_cols = slice(kvh * B_HEAD_DIM, (kvh + 1) * B_HEAD_DIM)
            q = row[:, h * B_HEAD_DIM:(h + 1) * B_HEAD_DIM]
            k_new = row[:, D_MODEL + kvh * B_HEAD_DIM:D_MODEL + (kvh + 1) * B_HEAD_DIM]
            v_new = row[:, D_MODEL + kvw + kvh * B_HEAD_DIM:D_MODEL + kvw + (kvh + 1) * B_HEAD_DIM]
            kc = ck_ref[bi, :, kv_cols]
            vc = cv_ref[bi, :, kv_cols]
            scale = B_HEAD_DIM ** -0.5
            s_c = jnp.sum(kc * q, axis=1, keepdims=True) * scale + bias_t_ref[h, 0:w, 0:1]
            s_c = jnp.where(pos >= 1, s_c, NEG_INF)
            s_n = jnp.sum(k_new * q, axis=1, keepdims=True) * scale + bias_t_ref[h, w:w + 1, 0:1]
            sink = sink_ref[0, h]
            m = jnp.maximum(jnp.maximum(jnp.max(s_c, axis=0, keepdims=True), s_n), sink)
            p_c = jnp.exp(s_c - m)
            p_n = jnp.exp(s_n - m)
            den = jnp.sum(p_c, axis=0, keepdims=True) + p_n + jnp.exp(sink - m)
            o = (jnp.sum(p_c * vc, axis=0, keepdims=True) + p_n * v_new) / den
            o_ref[bi:bi + 1, h * B_HEAD_DIM:(h + 1) * B_HEAD_DIM] = o


def _swa_decode(qkv, cache_k, cache_v, bias_t, sinks):
    b = qkv.shape[0]
    kvw = B_KV_HEADS * B_HEAD_DIM
    bb = DECODE_BATCH_BLOCK
    return pl.pallas_call(
        _swa_decode_kernel,
        grid=(b // bb,),
        in_specs=[pl.BlockSpec((bb, qkv.shape[1]), lambda i: (i, 0)),
                  pl.BlockSpec((bb, WINDOW, kvw), lambda i: (i, 0, 0)),
                  pl.BlockSpec((bb, WINDOW, kvw), lambda i: (i, 0, 0)),
                  pl.BlockSpec((B_HEADS, 2 * WINDOW, WINDOW), lambda i: (0, 0, 0)),
                  pl.BlockSpec(memory_space=pltpu.SMEM)],
        out_specs=pl.BlockSpec((bb, D_MODEL), lambda i: (i, 0)),
        out_shape=jax.ShapeDtypeStruct((b, D_MODEL), F32),
        compiler_params=_params("parallel"),
        name="swa_decode",
    )(qkv, cache_k, cache_v, bias_t, sinks)


_CANDIDATES = [(i, j) for i in range(P_TOPK) for j in range(P_TOPK)
               if (i + 1) * (j + 1) <= P_TOPK]
_BIG_POS = float(4 * P_TOPK * P_TOPK)
_HK = P_HEADS * N_KEYS


def _pack_bf16(x):
    return pltpu.bitcast(x.astype(BF16), jnp.uint32)


def _unpack_bf16(words):
    return pltpu.bitcast(words, BF16)


def _extract_top(s_ref, r_ref, top_ref):
    ph = P_HEADS
    for k in range(N_KEYS):
        r_ref[k * ph:(k + 1) * ph, :] = jnp.full((ph, LANES), NOT_SELECTED_RANK, F32)

    def body(r, carry):
        best = _tree(jnp.maximum, [s_ref[k * ph:(k + 1) * ph, :] for k in range(N_KEYS)])
        first = _tree(jnp.minimum,
                      [jnp.where(s_ref[k * ph:(k + 1) * ph, :] == best, float(k), float(N_KEYS))
                       for k in range(N_KEYS)])
        rank = r.astype(F32)
        for k in range(N_KEYS):
            rows = slice(k * ph, (k + 1) * ph)
            hit = first == float(k)
            s_ref[rows, :] = jnp.where(hit, -jnp.inf, s_ref[rows, :])
            r_ref[rows, :] = jnp.where(hit, rank, r_ref[rows, :])
        top_ref[r] = best
        return carry

    lax.fori_loop(0, P_TOPK, body, 0)


def _peer_topk_kernel(x_ref, wq_ref, wbd_ref, l1_ref, w1_ref, r2_ref, e2_ref,
                      s_scr, sc_scr, r1_scr, r2_scr, a_scr, b_scr):
    ph = P_HEADS
    xb = x_ref[...].astype(BF16)
    qt = lax.dot_general(wq_ref[...], xb, (((1,), (1,)), ((), ())), preferred_element_type=F32)
    for p in range(2):
        s_scr[p] = jnp.dot(wbd_ref[p], qt[p * _HK:(p + 1) * _HK].astype(BF16),
                           preferred_element_type=F32)
    sc_scr[...] = s_scr[0]
    _extract_top(sc_scr, r1_scr, a_scr)
    sc_scr[...] = s_scr[1]
    _extract_top(sc_scr, r2_scr, b_scr)

    a = [a_scr[i] for i in range(P_TOPK)]
    b = [b_scr[j] for j in range(P_TOPK)]
    cand = {ij: a[ij[0]] + b[ij[1]] for ij in _CANDIDATES}
    pos = {ij: float(ij[0] * P_TOPK + ij[1]) for ij in _CANDIDATES}
    work = dict(cand)
    tau = tau_pos = None
    for r in range(P_TOPK):
        tau = _tree(jnp.maximum, [work[ij] for ij in _CANDIDATES])
        tau_pos = _tree(jnp.minimum,
                        [jnp.where(work[ij] == tau, pos[ij], _BIG_POS) for ij in _CANDIDATES])
        if r < P_TOPK - 1:
            for ij in _CANDIDATES:
                work[ij] = jnp.where(tau_pos == pos[ij], -jnp.inf, work[ij])
    ea = [jnp.exp(a[i] - a[0]) for i in range(P_TOPK)]
    eb = [jnp.exp(b[j] - b[0]) for j in range(P_TOPK)]
    count = [jnp.zeros((ph, LANES), F32) for _ in range(P_TOPK)]
    z = jnp.zeros((ph, LANES), F32)
    for ij in _CANDIDATES:
        i, j = ij
        chosen = jnp.where(cand[ij] > tau, 1.0,
                           jnp.where(cand[ij] == tau,
                                     jnp.where(tau_pos >= pos[ij], 1.0, 0.0), 0.0))
        count[i] = count[i] + chosen
        z = z + chosen * (ea[i] * eb[j])
    inv_z = 1.0 / z
    for k in range(N_KEYS):
        rows = slice(k * ph, (k + 1) * ph)
        r1 = r1_scr[rows, :]
        reach = jnp.zeros((ph, LANES), F32)
        for i in range(P_TOPK):
            reach = jnp.where(r1 == float(i), count[i], reach)
        l1_ref[rows, :] = reach
        w1_ref[rows, :] = jnp.exp(s_scr[0, rows, :] - a[0]) * inv_z
        sc_scr[rows, :] = jnp.exp(s_scr[1, rows, :] - b[0])
    for h in range(ph):
        r2_ref[h] = _pack_bf16(r2_scr[pl.ds(h, N_KEYS, stride=ph), :])
        e2_ref[h] = _pack_bf16(sc_scr[pl.ds(h, N_KEYS, stride=ph), :])


def _peer_topk(x, wq_t, wbd):
    t = x.shape[0]
    nt = t // LANES
    tile = pl.BlockSpec((None, _HK, LANES), lambda i: (i, 0, 0))
    tile3 = pl.BlockSpec((None, P_HEADS, N_KEYS // 2, LANES), lambda i: (i, 0, 0, 0))
    flat = jax.ShapeDtypeStruct((nt, _HK, LANES), F32)
    cube = jax.ShapeDtypeStruct((nt, P_HEADS, N_KEYS // 2, LANES), jnp.uint32)
    return pl.pallas_call(
        _peer_topk_kernel,
        grid=(t // LANES,),
        in_specs=[pl.BlockSpec((LANES, D_MODEL), lambda i: (i, 0)),
                  pl.BlockSpec((2 * _HK, D_MODEL), lambda i: (0, 0)),
                  pl.BlockSpec((2, _HK, _HK), lambda i: (0, 0, 0))],
        out_specs=[tile, tile, tile3, tile3],
        out_shape=[flat, flat, cube, cube],
        scratch_shapes=[pltpu.VMEM((2, _HK, LANES), F32),
                        pltpu.VMEM((_HK, LANES), F32),
                        pltpu.VMEM((_HK, LANES), F32),
                        pltpu.VMEM((_HK, LANES), F32),
                        pltpu.VMEM((P_TOPK, P_HEADS, LANES), F32),
                        pltpu.VMEM((P_TOPK, P_HEADS, LANES), F32)],
        compiler_params=_params("parallel"),
        name="peer_topk",
    )(x, wq_t, wbd)


def _replicated_bf16(ref, tile, row):
    rep = ref[tile, pl.ds(row, SUBLANES, stride=0), :]
    return jnp.concatenate([rep, rep], axis=0).astype(BF16)


def _peer_mix_kernel(x_ref, u_ref, vt_ref, l1_ref, w1_ref, r2_ref, e2_ref, g_ref, b_ref,
                     out_ref, xt_scr, acc_scr, act_scr, p_scr):
    e = pl.program_id(1)
    n_groups, _, group = xt_scr.shape

    @pl.when(e == 0)
    def _():
        for gi in range(n_groups):
            xt_scr[gi] = x_ref[gi * group:(gi + 1) * group, :].T.astype(BF16)
        acc_scr[...] = jnp.zeros(acc_scr.shape, F32)

    def project(gi):
        act_scr[gi] = jnp.dot(u_ref[...], xt_scr[gi], preferred_element_type=F32)

    def mix(gi):
        acc_scr[gi] += jnp.dot(vt_ref[...], p_scr[gi], preferred_element_type=F32)

    lookahead = min(2, n_groups)
    for gi in range(lookahead):
        project(gi)
    for gi in range(n_groups):
        for c0 in range(0, group, LANES):
            cols = slice(c0, c0 + LANES)
            ct = (gi * group + c0) // LANES
            for kb in range(0, N_KEYS, PEER_KEY_BLOCK):
                subs = range(kb, kb + PEER_KEY_BLOCK, BF16_ROWS)
                gates = {(jj, k0): jnp.zeros((BF16_ROWS, LANES), BF16)
                         for jj in range(PEER_ROWS_PER_STEP) for k0 in subs}
                for h in range(P_HEADS):
                    rows_h = [jj * P_HEADS + h for jj in range(PEER_ROWS_PER_STEP)]
                    reach = [_replicated_bf16(l1_ref, ct, r) for r in rows_h]
                    weight = [_replicated_bf16(w1_ref, ct, r) for r in rows_h]
                    for k0 in subs:
                        words = slice(k0 // 2, k0 // 2 + SUBLANES)
                        r2 = _unpack_bf16(r2_ref[ct, h, words, :])
                        e2 = _unpack_bf16(e2_ref[ct, h, words, :])
                        for jj in range(PEER_ROWS_PER_STEP):
                            picked = jnp.where(r2 < reach[jj], e2, jnp.zeros_like(e2))
                            gates[jj, k0] = gates[jj, k0] + picked * weight[jj]
                for jj in range(PEER_ROWS_PER_STEP):
                    for k0 in subs:
                        rows = slice(jj * N_KEYS + k0, jj * N_KEYS + k0 + BF16_ROWS)
                        a = act_scr[gi, rows, cols]
                        gelu = 0.5 * a * (1.0 + lax.erf(a * (2.0 ** -0.5)))
                        p_scr[gi, rows, cols] = gelu.astype(BF16) * gates[jj, k0]
        if gi >= 1:
            mix(gi - 1)
        if gi + lookahead < n_groups:
            project(gi + lookahead)
    mix(n_groups - 1)

    @pl.when(e == pl.num_programs(1) - 1)
    def _():
        for gi in range(n_groups):
            rows = slice(gi * group, (gi + 1) * group)
            y = DN_ALPHA * x_ref[rows, :] + acc_scr[gi].T
            out_ref[rows, :] = _layer_norm(y, g_ref[...], b_ref[...])


def _peer_mix(x, u, vt, l1, w1, r2, e2, g, b, tm):
    t = x.shape[0]
    te = PEER_ROWS_PER_STEP * N_KEYS
    tr = PEER_ROWS_PER_STEP * P_HEADS
    nt = tm // LANES
    group = min(tm, PEER_COLUMN_GROUP)
    n_groups = tm // group
    return pl.pallas_call(
        _peer_mix_kernel,
        grid=(t // tm, N_EXPERTS // te),
        in_specs=[pl.BlockSpec((tm, D_MODEL), lambda i, e: (i, 0)),
                  pl.BlockSpec((te, D_MODEL), lambda i, e: (e, 0)),
                  pl.BlockSpec((D_MODEL, te), lambda i, e: (0, e)),
                  pl.BlockSpec((nt, tr, LANES), lambda i, e: (i, e, 0)),
                  pl.BlockSpec((nt, tr, LANES), lambda i, e: (i, e, 0)),
                  pl.BlockSpec((nt, P_HEADS, N_KEYS // 2, LANES), lambda i, e: (i, 0, 0, 0)),
                  pl.BlockSpec((nt, P_HEADS, N_KEYS // 2, LANES), lambda i, e: (i, 0, 0, 0)),
                  pl.BlockSpec((1, D_MODEL), lambda i, e: (0, 0)),
                  pl.BlockSpec((1, D_MODEL), lambda i, e: (0, 0))],
        out_specs=pl.BlockSpec((tm, D_MODEL), lambda i, e: (i, 0)),
        out_shape=jax.ShapeDtypeStruct((t, D_MODEL), F32),
        scratch_shapes=[pltpu.VMEM((n_groups, D_MODEL, group), BF16),
                        pltpu.VMEM((n_groups, D_MODEL, group), F32),
                        pltpu.VMEM((n_groups, te, group), F32),
                        pltpu.VMEM((n_groups, te, group), BF16)],
        compiler_params=_params("parallel", "arbitrary"),
        name="peer_mix",
    )(x, u, vt, l1, w1, r2, e2, g, b)


def _peer_layer(x, pw, g, b):
    t = x.shape[0]
    l1, w1, r2, e2 = _peer_topk(x, pw["wq_t"], pw["wbd"])
    tm = PEER_TOKEN_TILE if t % PEER_TOKEN_TILE == 0 else LANES
    return _peer_mix(x, pw["u"], pw["vt"], l1, w1, r2, e2, g, b, tm)


def _t5_bucket(dist):
    max_exact = N_BUCKETS // 2
    d = jnp.maximum(dist, 0)
    df = jnp.maximum(d, 1).astype(F32)
    large = max_exact + (jnp.log(df / max_exact) / math.log(MAX_DISTANCE / max_exact)
                         * (N_BUCKETS - max_exact)).astype(jnp.int32)
    large = jnp.minimum(large, N_BUCKETS - 1)
    return jnp.where(d < max_exact, d, large)


def _row(a):
    return a.reshape(1, -1)


def _prepare(w):
    n_gate = 2 * A_HEADS
    w_in = w["a_w_in"][0]
    b_in = w["a_b_in"][0]
    prep = {
        "w_in_main": w_in[:, :2 * A_INNER].astype(BF16),
        "b_in_main": _row(b_in[:2 * A_INNER]),
        "w_in_gate": jnp.pad(w_in[:, 2 * A_INNER:], ((0, 0), (0, LANES - n_gate))).astype(BF16),
        "b_in_gate": _row(jnp.pad(b_in[2 * A_INNER:], (0, LANES - n_gate))),
        "conv_w": w["a_conv_w"][0],
        "conv_b": _row(w["a_conv_b"][0]),
        "wq": w["a_w_q"][0].astype(BF16),
        "wk": w["a_w_k"][0].astype(BF16),
        "wv": w["a_w_v"][0].astype(BF16),
        "w_out": w["a_w_out"][0].astype(BF16),
        "w_qkv": jnp.concatenate([w["b_w_q"][0], w["kv_w"]], axis=1).astype(BF16),
        "w_o": w["b_w_o"][0].astype(BF16),
        "sinks": _row(w["b_sinks"][0]),
        "rel_bias": w["rel_bias"],
    }
    peer = []
    eye = jnp.eye(P_HEADS, dtype=F32)
    half = D_KEY // 2
    for layer in range(DEPTH):
        wq = w["peer_w_q"][layer].reshape(D_MODEL, P_HEADS, 2, half)
        sk = w["peer_subkeys"][layer]
        peer.append({
            "wq_t": wq.transpose(2, 1, 3, 0).reshape(2 * _HK, D_MODEL).astype(BF16),
            "wbd": jnp.einsum("hpkc,hg->pkhgc", sk, eye).reshape(2, _HK, _HK).astype(BF16),
            "u": w["peer_u"][layer].astype(BF16),
            "vt": w["peer_v"][layer].T.astype(BF16),
        })
    prep["peer"] = peer
    return prep


def _zero_bias(n):
    return jnp.zeros((1, n), F32)


def _attention_tables(rel_bias):
    qi = jnp.arange(WINDOW)[:, None]
    kj = jnp.arange(2 * WINDOW)[None, :]
    bucket = _t5_bucket(qi + WINDOW - kj).astype(jnp.int32)
    return _bias_tables(bucket, rel_bias)


def _prompt_trunk(x, w, p, bias):
    batch, length, d = x.shape
    t = batch * length
    xt = x.reshape(t, d)
    proj = _linear(xt, p["w_in_main"], p["b_in_main"], 512, 512)
    gates = _linear(xt, p["w_in_gate"], p["b_in_gate"], 512, LANES)
    row_tab, col_tab = _gate_tables(gates, batch, length, MLSTM_CHUNK)
    q, k, v = _conv_qkv(proj, p["conv_w"], p["conv_b"], p["wq"], p["wk"], p["wv"], batch, length)
    h, c_new, n_new, m_new = _mlstm_prompt(q, k, v, row_tab, col_tab, batch, length, MLSTM_CHUNK)
    x1 = _outproj_ln(h, proj, p["w_out"], xt, _row(w["ln_mix_g"][0]), _row(w["ln_mix_b"][0]), 256)
    x2 = _peer_layer(x1, p["peer"][0], _row(w["ln_ffn_g"][0]), _row(w["ln_ffn_b"][0]))
    qkv = _linear(x2, p["w_qkv"], _zero_bias(p["w_qkv"].shape[1]), 512, 512)
    o = _swa_prompt(qkv, bias, p["sinks"], batch, length)
    x3 = _outproj_ln(o, None, p["w_o"], x2, _row(w["ln_mix_g"][1]), _row(w["ln_mix_b"][1]), 256)
    x4 = _peer_layer(x3, p["peer"][1], _row(w["ln_ffn_g"][1]), _row(w["ln_ffn_b"][1]))

    kvw = B_KV_HEADS * B_HEAD_DIM
    qkv3 = qkv.reshape(batch, length, -1)
    k_win = qkv3[:, -WINDOW:, D_MODEL:D_MODEL + kvw].reshape(batch, WINDOW, B_KV_HEADS, B_HEAD_DIM)
    v_win = qkv3[:, -WINDOW:, D_MODEL + kvw:].reshape(batch, WINDOW, B_KV_HEADS, B_HEAD_DIM)
    conv = proj.reshape(batch, length, -1)[:, -(A_CONV_W - 1):, :A_INNER]
    return (x4.reshape(batch, length, d),
            c_new[None],
            n_new.reshape(1, batch, A_HEADS, A_HEAD_DIM),
            m_new[:, :, 0, 0][None],
            conv[None], k_win, v_win)


def _sample_trunk(x, conv0, c0, n0, m0, k_buf, v_buf, w, p, bias_t):
    batch, length, d = x.shape
    xt = x.reshape(batch, d)
    proj = _linear(xt, p["w_in_main"], p["b_in_main"], batch, 512)
    gates = _linear(xt, p["w_in_gate"], p["b_in_gate"], batch, LANES)
    buf = conv0[0]
    q, k, v, qt, kt = _sample_conv_qkv(proj, buf.reshape(batch, -1), p["conv_w"], p["conv_b"],
                                       p["wq"], p["wk"], p["wv"])
    h, c_new, n_new, m_new = _sample_mlstm(q, k, v, qt, kt, gates, m0[0], n0[0], c0[0])
    x1 = _outproj_ln(h.reshape(batch, A_INNER), proj, p["w_out"], xt,
                     _row(w["ln_mix_g"][0]), _row(w["ln_mix_b"][0]), batch)
    x2 = _peer_layer(x1, p["peer"][0], _row(w["ln_ffn_g"][0]), _row(w["ln_ffn_b"][0]))
    qkv = _linear(x2, p["w_qkv"], _zero_bias(p["w_qkv"].shape[1]), batch, 512)
    kvw = B_KV_HEADS * B_HEAD_DIM
    o = _swa_decode(qkv, k_buf.reshape(batch, WINDOW, kvw), v_buf.reshape(batch, WINDOW, kvw),
                    bias_t, p["sinks"])
    x3 = _outproj_ln(o, None, p["w_o"], x2, _row(w["ln_mix_g"][1]), _row(w["ln_mix_b"][1]), batch)
    x4 = _peer_layer(x3, p["peer"][1], _row(w["ln_ffn_g"][1]), _row(w["ln_ffn_b"][1]))

    k_new = qkv[:, D_MODEL:D_MODEL + kvw].reshape(batch, 1, B_KV_HEADS, B_HEAD_DIM)
    v_new = qkv[:, D_MODEL + kvw:].reshape(batch, 1, B_KV_HEADS, B_HEAD_DIM)
    k_win = jnp.concatenate([k_buf[:, 1:], k_new], axis=1)
    v_win = jnp.concatenate([v_buf[:, 1:], v_new], axis=1)
    conv = jnp.concatenate([buf[:, 1:], proj[:, None, :A_INNER]], axis=1)
    return (x4.reshape(batch, length, d),
            c_new[None],
            n_new.reshape(1, batch, A_HEADS, A_HEAD_DIM),
            m_new[:, :, 0, 0][None],
            conv[None], k_win, v_win)


def kernel(x_prompt, x_sample, state_mlstm_C, state_mlstm_n, state_mlstm_m, state_mlstm_conv,
           cache_k_win, cache_v_win, a_w_in, a_b_in, a_conv_w, a_conv_b, a_w_q, a_w_k, a_w_v,
           a_w_out, kv_w, b_w_q, b_w_o, b_sinks, rel_bias, ln_mix_g, ln_mix_b, ln_ffn_g,
           ln_ffn_b, peer_w_q, peer_subkeys, peer_u, peer_v):
    w = {"a_w_in": a_w_in, "a_b_in": a_b_in, "a_conv_w": a_conv_w, "a_conv_b": a_conv_b,
         "a_w_q": a_w_q, "a_w_k": a_w_k, "a_w_v": a_w_v, "a_w_out": a_w_out, "kv_w": kv_w,
         "b_w_q": b_w_q, "b_w_o": b_w_o, "b_sinks": b_sinks, "rel_bias": rel_bias,
         "ln_mix_g": ln_mix_g, "ln_mix_b": ln_mix_b, "ln_ffn_g": ln_ffn_g, "ln_ffn_b": ln_ffn_b,
         "peer_w_q": peer_w_q, "peer_subkeys": peer_subkeys, "peer_u": peer_u, "peer_v": peer_v}
    p = _prepare(w)
    bias, bias_t = _attention_tables(rel_bias)
    prompt = _prompt_trunk(x_prompt, w, p, bias)
    sample = _sample_trunk(x_sample, state_mlstm_conv, state_mlstm_C, state_mlstm_n,
                           state_mlstm_m, cache_k_win, cache_v_win, w, p, bias_t)
    return (prompt[0], sample[0]) + prompt[1:] + sample[1:]
```

```python
import functools
import math

import jax
import jax.numpy as jnp
from jax import lax
from jax.experimental import pallas as pl
from jax.experimental.pallas import tpu as pltpu

D_MODEL = 1024
DEPTH = 2
A_HEADS = 4
A_INNER = 2 * D_MODEL
A_HEAD_DIM = A_INNER // A_HEADS
A_CONV_W = 4
B_HEADS = 16
B_KV_HEADS = 4
B_GROUP = B_HEADS // B_KV_HEADS
B_HEAD_DIM = D_MODEL // B_HEADS
WINDOW = 128
N_BUCKETS = 32
MAX_DISTANCE = 128
P_HEADS = 8
N_KEYS = 128
N_EXPERTS = N_KEYS * N_KEYS
D_KEY = 256
P_TOPK = 16
DN_ALPHA = (2 * DEPTH) ** 0.25
LN_EPS = 1e-5
NEG_INF = -1e30

LANES = 128
SUBLANES = 8
BF16_ROWS = 2 * SUBLANES
VMEM_LIMIT_BYTES = 48 * 1024 * 1024

MLSTM_CHUNK = 256
PEER_TOPK_TOKENS = 256
PEER_TOKEN_TILE = 1024
PEER_ROWS_PER_STEP = 4
PEER_COLUMN_GROUP = 256
PEER_KEY_BLOCK = 32
NOT_SELECTED_RANK = float(P_TOPK)

F32 = jnp.float32
BF16 = jnp.bfloat16


def _params(*sem):
    return pltpu.CompilerParams(dimension_semantics=sem, vmem_limit_bytes=VMEM_LIMIT_BYTES)


def _tree(op, vals):
    vals = list(vals)
    while len(vals) > 1:
        nxt = [op(vals[i], vals[i + 1]) for i in range(0, len(vals) - 1, 2)]
        if len(vals) % 2:
            nxt.append(vals[-1])
        vals = nxt
    return vals[0]


def _layer_norm(y, g, b):
    mu = jnp.mean(y, axis=-1, keepdims=True)
    yc = y - mu
    var = jnp.mean(yc * yc, axis=-1, keepdims=True)
    return yc * lax.rsqrt(var + LN_EPS) * g + b


def _linear_kernel(x_ref, w_ref, b_ref, o_ref):
    x = x_ref[...].astype(BF16)
    o_ref[...] = jnp.dot(x, w_ref[...], preferred_element_type=F32) + b_ref[...]


def _linear(x, w, b, tm, tn):
    t, k = x.shape
    n = w.shape[1]
    return pl.pallas_call(
        _linear_kernel,
        grid=(t // tm, n // tn),
        in_specs=[pl.BlockSpec((tm, k), lambda i, j: (i, 0)),
                  pl.BlockSpec((k, tn), lambda i, j: (0, j)),
                  pl.BlockSpec((1, tn), lambda i, j: (0, j))],
        out_specs=pl.BlockSpec((tm, tn), lambda i, j: (i, j)),
        out_shape=jax.ShapeDtypeStruct((t, n), F32),
        compiler_params=_params("parallel", "arbitrary"),
        name="linear",
    )(x, w, b)


def _outproj_ln_kernel(gated, *refs):
    if gated:
        a_ref, o_ref, w_ref, res_ref, g_ref, b_ref, out_ref = refs
        act = jax.nn.sigmoid(o_ref[...]) * a_ref[...]
    else:
        a_ref, w_ref, res_ref, g_ref, b_ref, out_ref = refs
        act = a_ref[...]
    sub = jnp.dot(act.astype(BF16), w_ref[...], preferred_element_type=F32)
    out_ref[...] = _layer_norm(DN_ALPHA * res_ref[...] + sub, g_ref[...], b_ref[...])


def _outproj_ln(act, gate_src, w, res, g, b, tm):
    t, k = act.shape
    d = w.shape[1]
    gated = gate_src is not None
    in_specs = [pl.BlockSpec((tm, k), lambda i: (i, 0))]
    args = [act]
    if gated:
        in_specs.append(pl.BlockSpec((tm, k), lambda i: (i, 1)))
        args.append(gate_src)
    in_specs += [pl.BlockSpec((k, d), lambda i: (0, 0)),
                 pl.BlockSpec((tm, d), lambda i: (i, 0)),
                 pl.BlockSpec((1, d), lambda i: (0, 0)),
                 pl.BlockSpec((1, d), lambda i: (0, 0))]
    args += [w, res, g, b]
    return pl.pallas_call(
        functools.partial(_outproj_ln_kernel, gated),
        grid=(t // tm,),
        in_specs=in_specs,
        out_specs=pl.BlockSpec((tm, d), lambda i: (i, 0)),
        out_shape=jax.ShapeDtypeStruct((t, d), F32),
        compiler_params=_params("parallel"),
        name="outproj_ln",
    )(*args)


def _log_sigmoid(x):
    return jnp.minimum(x, 0.0) - jnp.log1p(jnp.exp(-jnp.abs(x)))


def _gates_kernel(chunk, g_ref, row_ref, col_ref):
    length = g_ref.shape[0]
    gt = g_ref[...].T
    top = gt[0:SUBLANES]
    row = lax.broadcasted_iota(jnp.int32, top.shape, 0)
    pos = lax.broadcasted_iota(jnp.int32, top.shape, 1) % chunk
    is_f = row >= A_HEADS
    x = jnp.where(is_f, _log_sigmoid(top), 0.0)
    shift = 1
    while shift < chunk:
        x = x + jnp.where(pos >= shift, pltpu.roll(x, shift, 1), 0.0)
        shift *= 2
    table = jnp.where(is_f, x, top)
    row_ref[...] = table
    padded = jnp.concatenate([table, jnp.zeros((LANES - SUBLANES, length), F32)], axis=0)
    col_ref[...] = padded.T


def _gate_tables(gates, batch, length, chunk):
    return pl.pallas_call(
        functools.partial(_gates_kernel, chunk),
        grid=(batch,),
        in_specs=[pl.BlockSpec((length, LANES), lambda b: (b, 0))],
        out_specs=[pl.BlockSpec((None, SUBLANES, length), lambda b: (b, 0, 0)),
                   pl.BlockSpec((length, LANES), lambda b: (b, 0))],
        out_shape=[jax.ShapeDtypeStruct((batch, SUBLANES, length), F32),
                   jax.ShapeDtypeStruct((batch * length, LANES), F32)],
        compiler_params=_params("parallel"),
        name="gate_tables",
    )(gates)


CONV_ROW_CHUNK = 512
CONV_PAD = SUBLANES


def _conv_qkv_kernel(xm_ref, cw_ref, cb_ref, wq_ref, wk_ref, wv_ref,
                     q_ref, k_ref, v_ref, pad_ref):
    length = xm_ref.shape[0]
    pad_ref[0:CONV_PAD, :] = jnp.zeros((CONV_PAD, A_HEAD_DIM), F32)
    pad_ref[CONV_PAD:CONV_PAD + length, :] = xm_ref[...]
    first = CONV_PAD - (A_CONV_W - 1)
    for c0 in range(0, length, CONV_ROW_CHUNK):
        acc = cb_ref[...]
        for w in range(A_CONV_W):
            acc = acc + pad_ref[c0 + first + w:c0 + first + w + CONV_ROW_CHUNK, :] * cw_ref[w:w + 1, :]
        xc = (acc * jax.nn.sigmoid(acc)).astype(BF16)
        xm = xm_ref[c0:c0 + CONV_ROW_CHUNK, :].astype(BF16)
        rows = slice(c0, c0 + CONV_ROW_CHUNK)
        q_ref[rows, :] = jnp.dot(xc, wq_ref[...], preferred_element_type=F32).astype(BF16)
        k = jnp.dot(xc, wk_ref[...], preferred_element_type=F32) * (A_HEAD_DIM ** -0.5)
        k_ref[rows, :] = k.astype(BF16)
        v_ref[rows, :] = jnp.dot(xm, wv_ref[...], preferred_element_type=F32).astype(BF16)


def _conv_qkv(proj, cw, cb, wq, wk, wv, batch, length):
    t = batch * length
    hd = A_HEAD_DIM
    tok = pl.BlockSpec((length, hd), lambda b, h: (b, h))
    wspec = pl.BlockSpec((None, hd, hd), lambda b, h: (h, 0, 0))
    out = jax.ShapeDtypeStruct((t, A_INNER), BF16)
    return pl.pallas_call(
        _conv_qkv_kernel,
        grid=(batch, A_HEADS),
        in_specs=[tok,
                  pl.BlockSpec((A_CONV_W, hd), lambda b, h: (0, h)),
                  pl.BlockSpec((1, hd), lambda b, h: (0, h)),
                  wspec, wspec, wspec],
        out_specs=[tok, tok, tok],
        out_shape=[out, out, out],
        scratch_shapes=[pltpu.VMEM((length + CONV_PAD, hd), F32)],
        compiler_params=_params("parallel", "arbitrary"),
        name="conv_qkv",
    )(proj, cw, cb, wq, wk, wv)


def _mlstm_kernel(q_ref, k_ref, v_ref, row_ref, col_ref,
                  h_ref, c_out_ref, n_out_ref, m_out_ref,
                  c_scr, n_scr, m_scr):
    ci = pl.program_id(1)
    chunk = q_ref.shape[0]
    hd = A_HEAD_DIM

    @pl.when(ci == 0)
    def _():
        c_scr[...] = jnp.zeros(c_scr.shape, F32)
        n_scr[...] = jnp.zeros(n_scr.shape, F32)
        m_scr[...] = jnp.zeros(m_scr.shape, F32)

    t_idx = lax.broadcasted_iota(jnp.int32, (chunk, chunk), 0)
    s_idx = lax.broadcasted_iota(jnp.int32, (chunk, chunk), 1)
    causal = s_idx <= t_idx
    for h in range(A_HEADS):
        cols = slice(h * hd, (h + 1) * hd)
        qh, kh, vh = q_ref[:, cols], k_ref[:, cols], v_ref[:, cols]
        i_col = col_ref[:, h:h + 1]
        f_col = col_ref[:, A_HEADS + h:A_HEADS + h + 1]
        i_row = row_ref[h:h + 1, :]
        f_row = row_ref[A_HEADS + h:A_HEADS + h + 1, :]
        m_prev = m_scr[h][:, 0:1]
        d = jnp.where(causal, f_col - f_row + i_row, NEG_INF)
        b_inter = f_col + m_prev
        m_t = jnp.maximum(b_inter, jnp.max(d, axis=1, keepdims=True))
        qk = lax.dot_general(qh, kh, (((1,), (1,)), ((), ())), preferred_element_type=F32)
        s = qk * jnp.exp(d - m_t)
        w_inter = jnp.exp(b_inter - m_t)
        q_c = jnp.dot(qh, c_scr[h].astype(BF16), preferred_element_type=F32)
        num = jnp.dot(s.astype(BF16), vh, preferred_element_type=F32) + w_inter * q_c
        q_n = jnp.sum(qh.astype(F32) * n_scr[h], axis=1, keepdims=True)
        den = jnp.sum(s, axis=1, keepdims=True) + w_inter * q_n
        h_ref[:, cols] = num / jnp.maximum(jnp.abs(den), jnp.exp(-m_t))
        f_last = f_col[chunk - 1:chunk, :]
        g = f_last - f_col + i_col
        m_new = jnp.maximum(f_last + m_prev, jnp.max(g, axis=0, keepdims=True))
        decay = jnp.exp(f_last + m_prev - m_new)
        wk = jnp.exp(g - m_new) * kh.astype(F32)
        kv = lax.dot_general(wk.astype(BF16), vh, (((0,), (0,)), ((), ())),
                             preferred_element_type=F32)
        c_scr[h] = decay * c_scr[h] + kv
        n_scr[h] = decay * n_scr[h] + jnp.sum(wk, axis=0, keepdims=True)
        m_scr[h] = jnp.broadcast_to(m_new, (1, LANES))

    @pl.when(ci == pl.num_programs(1) - 1)
    def _():
        c_out_ref[...] = c_scr[...]
        n_out_ref[...] = n_scr[...]
        m_out_ref[...] = m_scr[...]


def _mlstm_prompt(q, k, v, row_tab, col_tab, batch, length, chunk):
    nc = length // chunk
    t = batch * length
    tok = pl.BlockSpec((chunk, A_INNER), lambda b, c: (b * nc + c, 0))
    return pl.pallas_call(
        _mlstm_kernel,
        grid=(batch, nc),
        in_specs=[tok, tok, tok,
                  pl.BlockSpec((None, SUBLANES, chunk), lambda b, c: (b, 0, c)),
                  pl.BlockSpec((chunk, LANES), lambda b, c: (b * nc + c, 0))],
        out_specs=[tok,
                   pl.BlockSpec((None, A_HEADS, A_HEAD_DIM, A_HEAD_DIM), lambda b, c: (b, 0, 0, 0)),
                   pl.BlockSpec((None, A_HEADS, 1, A_HEAD_DIM), lambda b, c: (b, 0, 0, 0)),
                   pl.BlockSpec((None, A_HEADS, 1, LANES), lambda b, c: (b, 0, 0, 0))],
        out_shape=[jax.ShapeDtypeStruct((t, A_INNER), F32),
                   jax.ShapeDtypeStruct((batch, A_HEADS, A_HEAD_DIM, A_HEAD_DIM), F32),
                   jax.ShapeDtypeStruct((batch, A_HEADS, 1, A_HEAD_DIM), F32),
                   jax.ShapeDtypeStruct((batch, A_HEADS, 1, LANES), F32)],
        scratch_shapes=[pltpu.VMEM((A_HEADS, A_HEAD_DIM, A_HEAD_DIM), F32),
                        pltpu.VMEM((A_HEADS, 1, A_HEAD_DIM), F32),
                        pltpu.VMEM((A_HEADS, 1, LANES), F32)],
        compiler_params=_params("parallel", "arbitrary"),
        name="mlstm_prompt",
    )(q, k, v, row_tab, col_tab)


def _sample_conv_qkv_kernel(proj_ref, buf_ref, cw_ref, cb_ref, wq_ref, wk_ref, wv_ref,
                            q_ref, k_ref, v_ref, qt_ref, kt_ref):
    xm = proj_ref[:, 0:A_INNER]
    acc = cb_ref[...] + xm * cw_ref[A_CONV_W - 1:A_CONV_W, :]
    for w in range(A_CONV_W - 1):
        acc = acc + buf_ref[:, w * A_INNER:(w + 1) * A_INNER] * cw_ref[w:w + 1, :]
    xc = (acc * jax.nn.sigmoid(acc)).astype(BF16)
    xmb = xm.astype(BF16)
    for h in range(A_HEADS):
        cols = slice(h * A_HEAD_DIM, (h + 1) * A_HEAD_DIM)
        q_ref[:, cols] = jnp.dot(xc[:, cols], wq_ref[h], preferred_element_type=F32)
        k_ref[:, cols] = (jnp.dot(xc[:, cols], wk_ref[h], preferred_element_type=F32)
                          * (A_HEAD_DIM ** -0.5))
        v_ref[:, cols] = jnp.dot(xmb[:, cols], wv_ref[h], preferred_element_type=F32)
    qt_ref[...] = q_ref[...].T
    kt_ref[...] = k_ref[...].T


def _sample_conv_qkv(proj, conv_buf_flat, cw, cb, wq, wk, wv):
    b = proj.shape[0]
    row = jax.ShapeDtypeStruct((b, A_INNER), F32)
    col = jax.ShapeDtypeStruct((A_INNER, b), F32)
    return pl.pallas_call(
        _sample_conv_qkv_kernel,
        out_shape=[row, row, row, col, col],
        compiler_params=pltpu.CompilerParams(vmem_limit_bytes=VMEM_LIMIT_BYTES),
        name="sample_conv_qkv",
    )(proj, conv_buf_flat, cw, cb, wq, wk, wv)


def _sample_mlstm_kernel(q_ref, k_ref, v_ref, qt_ref, kt_ref, g_ref, m0_ref, n0_ref, c0_ref,
                         h_ref, c_ref, n_ref, m_ref):
    b = pl.program_id(0)
    h = pl.program_id(1)
    hd = A_HEAD_DIM
    pick = (lax.broadcasted_iota(jnp.int32, (LANES, LANES), 0) == b).astype(F32)
    q_col = jnp.dot(qt_ref[...], pick, precision=lax.Precision.HIGHEST, preferred_element_type=F32)
    k_col = jnp.dot(kt_ref[...], pick, precision=lax.Precision.HIGHEST, preferred_element_type=F32)
    reps = hd // LANES
    q_mat = jnp.concatenate([q_col] * reps, axis=1)
    k_mat = jnp.concatenate([k_col] * reps, axis=1)
    q_row, k_row, v_row = q_ref[...], k_ref[...], v_ref[...]
    lane = lax.broadcasted_iota(jnp.int32, (1, LANES), 1)
    gates = g_ref[...]
    log_i = jnp.sum(jnp.where(lane == h, gates, 0.0), axis=1, keepdims=True)
    f_pre = jnp.sum(jnp.where(lane == h + A_HEADS, gates, 0.0), axis=1, keepdims=True)
    log_f = _log_sigmoid(f_pre)
    lane_h = lax.broadcasted_iota(jnp.int32, (1, A_HEADS), 1)
    m0 = jnp.sum(jnp.where(lane_h == h, m0_ref[...], 0.0), axis=1, keepdims=True)
    m_t = jnp.maximum(log_f + m0, log_i)
    w_inter = jnp.exp(log_f + m0 - m_t)
    w_new = jnp.exp(log_i - m_t)
    c0 = c0_ref[...]
    n0 = n0_ref[...]
    s = jnp.sum(q_row * k_row, axis=1, keepdims=True) * w_new
    q_c = jnp.sum(q_mat * c0, axis=0, keepdims=True)
    q_n = jnp.sum(q_row * n0, axis=1, keepdims=True)
    num = s * v_row + w_inter * q_c
    den = s + w_inter * q_n
    h_ref[...] = num / jnp.maximum(jnp.abs(den), jnp.exp(-m_t))
    c_ref[...] = w_inter * c0 + (w_new * k_mat) * v_row
    n_ref[...] = w_inter * n0 + w_new * k_row
    m_ref[...] = jnp.broadcast_to(m_t, (1, LANES))


def _sample_mlstm(q, k, v, qt, kt, gates, m0, n0, c0):
    b = q.shape[0]
    hd = A_HEAD_DIM
    row3 = lambda a: a.reshape(b, 1, a.shape[-1])
    rspec = pl.BlockSpec((None, 1, hd), lambda i, h: (i, 0, h))
    cspec = pl.BlockSpec((hd, LANES), lambda i, h: (h, 0))
    nspec = pl.BlockSpec((None, None, 1, hd), lambda i, h: (i, h, 0, 0))
    mspec = pl.BlockSpec((None, None, 1, LANES), lambda i, h: (i, h, 0, 0))
    big = pl.BlockSpec((None, None, hd, hd), lambda i, h: (i, h, 0, 0))
    return pl.pallas_call(
        _sample_mlstm_kernel,
        grid=(b, A_HEADS),
        in_specs=[rspec, rspec, rspec, cspec, cspec,
                  pl.BlockSpec((None, 1, LANES), lambda i, h: (i, 0, 0)),
                  pl.BlockSpec((None, 1, A_HEADS), lambda i, h: (i, 0, 0)),
                  nspec, big],
        out_specs=[rspec, big, nspec, mspec],
        out_shape=[jax.ShapeDtypeStruct((b, 1, A_INNER), F32),
                   jax.ShapeDtypeStruct((b, A_HEADS, hd, hd), F32),
                   jax.ShapeDtypeStruct((b, A_HEADS, 1, hd), F32),
                   jax.ShapeDtypeStruct((b, A_HEADS, 1, LANES), F32)],
        compiler_params=_params("parallel", "arbitrary"),
        name="sample_mlstm",
    )(row3(q), row3(k), row3(v), qt, kt, row3(gates), row3(m0),
      n0.reshape(b, A_HEADS, 1, hd), c0)


def _bias_kernel(bucket_ref, rel_ref, bias_ref, bias_t_ref):
    h = pl.program_id(0)
    bucket = bucket_ref[...]
    acc = jnp.zeros(bucket.shape, F32)
    for n in range(N_BUCKETS):
        acc = jnp.where(bucket == n, rel_ref[n, h], acc)
    bias_ref[...] = acc
    bias_t_ref[...] = acc.T


def _bias_tables(bucket, rel_bias):
    w, w2 = bucket.shape
    return pl.pallas_call(
        _bias_kernel,
        grid=(B_HEADS,),
        in_specs=[pl.BlockSpec((w, w2), lambda h: (0, 0)),
                  pl.BlockSpec(memory_space=pltpu.SMEM)],
        out_specs=[pl.BlockSpec((None, w, w2), lambda h: (h, 0, 0)),
                   pl.BlockSpec((None, w2, w), lambda h: (h, 0, 0))],
        out_shape=[jax.ShapeDtypeStruct((B_HEADS, w, w2), F32),
                   jax.ShapeDtypeStruct((B_HEADS, w2, w), F32)],
        compiler_params=_params("arbitrary"),
        name="bias_tables",
    )(bucket, rel_bias)


def _swa_prompt_kernel(q_ref, kp_ref, kc_ref, vp_ref, vc_ref, bias_ref, sink_ref, o_ref):
    n = pl.program_id(1)
    w = WINDOW
    q = q_ref[...].astype(BF16)
    kk = jnp.concatenate([kp_ref[...], kc_ref[...]], axis=0).astype(BF16)
    vv = jnp.concatenate([vp_ref[...], vc_ref[...]], axis=0).astype(BF16)
    qi = lax.broadcasted_iota(jnp.int32, (w, 2 * w), 0)
    kj = lax.broadcasted_iota(jnp.int32, (w, 2 * w), 1)
    dist = qi + w - kj
    valid = (dist >= 0) & (dist < w) & ((kj >= w) | (n > 0))
    for h in range(B_HEADS):
        kvh = h // B_GROUP
        qh = q[:, h * B_HEAD_DIM:(h + 1) * B_HEAD_DIM]
        kh = kk[:, kvh * B_HEAD_DIM:(kvh + 1) * B_HEAD_DIM]
        vh = vv[:, kvh * B_HEAD_DIM:(kvh + 1) * B_HEAD_DIM]
        s = lax.dot_general(qh, kh, (((1,), (1,)), ((), ())), preferred_element_type=F32)
        s = jnp.where(valid, s * (B_HEAD_DIM ** -0.5) + bias_ref[h], NEG_INF)
        sink = sink_ref[0, h]
        m = jnp.maximum(jnp.max(s, axis=1, keepdims=True), sink)
        p = jnp.exp(s - m)
        den = jnp.sum(p, axis=1, keepdims=True) + jnp.exp(sink - m)
        p = (p / den).astype(BF16)
        o_ref[:, h * B_HEAD_DIM:(h + 1) * B_HEAD_DIM] = jnp.dot(p, vh, preferred_element_type=F32)


def _swa_prompt(qkv, bias, sinks, batch, length):
    nb = length // WINDOW
    kvw = B_KV_HEADS * B_HEAD_DIM
    kcol = D_MODEL // kvw
    cur = lambda b, n: b * nb + n
    prev = lambda b, n: b * nb + jnp.maximum(n - 1, 0)
    return pl.pallas_call(
        _swa_prompt_kernel,
        grid=(batch, nb),
        in_specs=[pl.BlockSpec((WINDOW, D_MODEL), lambda b, n: (cur(b, n), 0)),
                  pl.BlockSpec((WINDOW, kvw), lambda b, n: (prev(b, n), kcol)),
                  pl.BlockSpec((WINDOW, kvw), lambda b, n: (cur(b, n), kcol)),
                  pl.BlockSpec((WINDOW, kvw), lambda b, n: (prev(b, n), kcol + 1)),
                  pl.BlockSpec((WINDOW, kvw), lambda b, n: (cur(b, n), kcol + 1)),
                  pl.BlockSpec((B_HEADS, WINDOW, 2 * WINDOW), lambda b, n: (0, 0, 0)),
                  pl.BlockSpec(memory_space=pltpu.SMEM)],
        out_specs=pl.BlockSpec((WINDOW, D_MODEL), lambda b, n: (cur(b, n), 0)),
        out_shape=jax.ShapeDtypeStruct((batch * length, D_MODEL), F32),
        compiler_params=_params("parallel", "arbitrary"),
        name="swa_prompt",
    )(qkv, qkv, qkv, qkv, qkv, bias, sinks)


DECODE_BATCH_BLOCK = 8


def _swa_decode_kernel(qkv_ref, ck_ref, cv_ref, bias_t_ref, sink_ref, o_ref):
    w = WINDOW
    kvw = B_KV_HEADS * B_HEAD_DIM
    pos = lax.broadcasted_iota(jnp.int32, (w, 1), 0)
    for bi in range(DECODE_BATCH_BLOCK):
        row = qkv_ref[bi:bi + 1, :]
        for h in range(B_HEADS):
            kvh = h // B_GROUP
            kv_cols = slice(kvh * B_HEAD_DIM, (kvh + 1) * B_HEAD_DIM)
            q = row[:, h * B_HEAD_DIM:(h + 1) * B_HEAD_DIM]
            k_new = row[:, D_MODEL + kvh * B_HEAD_DIM:D_MODEL + (kvh + 1) * B_HEAD_DIM]
            v_new = row[:, D_MODEL + kvw + kvh * B_HEAD_DIM:D_MODEL + kvw + (kvh + 1) * B_HEAD_DIM]
            kc = ck_ref[bi, :, kv_cols]
            vc = cv_ref[bi, :, kv_cols]
            scale = B_HEAD_DIM ** -0.5
            s_c = jnp.sum(kc * q, axis=1, keepdims=True) * scale + bias_t_ref[h, 0:w, 0:1]
            s_c = jnp.where(pos >= 1, s_c, NEG_INF)
            s_n = jnp.sum(k_new * q, axis=1, keepdims=True) * scale + bias_t_ref[h, w:w + 1, 0:1]
            sink = sink_ref[0, h]
            m = jnp.maximum(jnp.maximum(jnp.max(s_c, axis=0, keepdims=True), s_n), sink)
            p_c = jnp.exp(s_c - m)
            p_n = jnp.exp(s_n - m)
            den = jnp.sum(p_c, axis=0, keepdims=True) + p_n + jnp.exp(sink - m)
            o = (jnp.sum(p_c * vc, axis=0, keepdims=True) + p_n * v_new) / den
            o_ref[bi:bi + 1, h * B_HEAD_DIM:(h + 1) * B_HEAD_DIM] = o


def _swa_decode(qkv, cache_k, cache_v, bias_t, sinks):
    b = qkv.shape[0]
    kvw = B_KV_HEADS * B_HEAD_DIM
    bb = DECODE_BATCH_BLOCK
    return pl.pallas_call(
        _swa_decode_kernel,
        grid=(b // bb,),
        in_specs=[pl.BlockSpec((bb, qkv.shape[1]), lambda i: (i, 0)),
                  pl.BlockSpec((bb, WINDOW, kvw), lambda i: (i, 0, 0)),
                  pl.BlockSpec((bb, WINDOW, kvw), lambda i: (i, 0, 0)),
                  pl.BlockSpec((B_HEADS, 2 * WINDOW, WINDOW), lambda i: (0, 0, 0)),
                  pl.BlockSpec(memory_space=pltpu.SMEM)],
        out_specs=pl.BlockSpec((bb, D_MODEL), lambda i: (i, 0)),
        out_shape=jax.ShapeDtypeStruct((b, D_MODEL), F32),
        compiler_params=_params("parallel"),
        name="swa_decode",
    )(qkv, cache_k, cache_v, bias_t, sinks)


_CANDIDATES = [(i, j) for i in range(P_TOPK) for j in range(P_TOPK)
               if (i + 1) * (j + 1) <= P_TOPK]
_BIG_POS = float(4 * P_TOPK * P_TOPK)
_HK = P_HEADS * N_KEYS


def _pack_bf16(x):
    return pltpu.bitcast(x.astype(BF16), jnp.uint32)


def _unpack_bf16(words):
    return pltpu.bitcast(words, BF16)


def _extract_top(s_ref, r_ref, top_ref):
    ph = P_HEADS
    for k in range(N_KEYS):
        r_ref[k * ph:(k + 1) * ph, :] = jnp.full((ph, LANES), NOT_SELECTED_RANK, F32)

    def body(r, carry):
        best = _tree(jnp.maximum, [s_ref[k * ph:(k + 1) * ph, :] for k in range(N_KEYS)])
        first = _tree(jnp.minimum,
                      [jnp.where(s_ref[k * ph:(k + 1) * ph, :] == best, float(k), float(N_KEYS))
                       for k in range(N_KEYS)])
        rank = lax.convert_element_type(r, F32)
        for k in range(N_KEYS):
            rows = slice(k * ph, (k + 1) * ph)
            hit = first == float(k)
            s_ref[rows, :] = jnp.where(hit, -jnp.inf, s_ref[rows, :])
            r_ref[rows, :] = jnp.where(hit, rank, r_ref[rows, :])
        top_ref[r] = best
        return carry

    lax.fori_loop(0, P_TOPK, body, 0)


def _sort_desc(vals):
    a = list(vals)
    n = len(a)
    k = 2
    while k <= n:
        j = k // 2
        while j >= 1:
            for i in range(n):
                l = i ^ j
                if l > i:
                    hi, lo = jnp.maximum(a[i], a[l]), jnp.minimum(a[i], a[l])
                    a[i], a[l] = (hi, lo) if (i & k) == 0 else (lo, hi)
            j //= 2
        k *= 2
    return a


def _merge_top(top, grp):
    n = len(top)
    a = [jnp.maximum(top[i], grp[n - 1 - i]) for i in range(n)]
    j = n // 2
    while j >= 1:
        for i in range(n):
            l = i ^ j
            if l > i:
                a[i], a[l] = jnp.maximum(a[i], a[l]), jnp.minimum(a[i], a[l])
        j //= 2
    return a


def _top_values(read):
    top = None
    for k0 in range(0, N_KEYS, P_TOPK):
        grp = _sort_desc([read(k) for k in range(k0, k0 + P_TOPK)])
        top = grp if top is None else _merge_top(top, grp)
    return top


def _tie_flags(read, top):
    flags = _tree(jnp.add, [jnp.where(top[i] > top[i + 1], 0.0, 1.0) for i in range(P_TOPK - 1)])
    reach = _tree(jnp.add, [jnp.where(read(k) >= top[P_TOPK - 1], 1.0, 0.0) for k in range(N_KEYS)])
    return flags + jnp.where(reach == float(P_TOPK), 0.0, 1.0)


def _peer_topk_kernel(x_ref, wq_ref, sk_ref, l1_ref, w1_ref, r2_ref, e2_ref,
                      s_scr, sc_scr, r1_scr, r2_scr, top_scr):
    ph = P_HEADS
    n_tiles = s_scr.shape[0]
    xb = x_ref[...].astype(BF16)
    qt = lax.dot_general(wq_ref[...], xb, (((1,), (1,)), ((), ())), preferred_element_type=F32)
    for p in range(2):
        for h in range(ph):
            q_hp = qt[p * _HK + h * N_KEYS:p * _HK + (h + 1) * N_KEYS].astype(BF16)
            sc = jnp.dot(sk_ref[h * 2 + p], q_hp, preferred_element_type=F32)
            for lt in range(n_tiles):
                tile = s_scr.at[lt, p]
                tile[pl.ds(h, N_KEYS, stride=ph), :] = sc[:, lt * LANES:(lt + 1) * LANES]

    def tile_body(lt, carry):
        _peer_select_tile(lt, s_scr, sc_scr, r1_scr, r2_scr, top_scr,
                          l1_ref, w1_ref, r2_ref, e2_ref)
        return carry

    lax.fori_loop(0, n_tiles, tile_body, 0)


def _peer_select_tile(lt, s_scr, sc_scr, r1_scr, r2_scr, top_scr, l1_ref, w1_ref, r2_ref, e2_ref):
    ph = P_HEADS

    def key_rows(k):
        return slice(k * ph, (k + 1) * ph)

    def read(p):
        return lambda k: s_scr[lt, p, key_rows(k), :]

    a = _top_values(read(0))
    b = _top_values(read(1))
    undecided = jnp.max(_tie_flags(read(0), a) + _tie_flags(read(1), b))
    cand = {ij: a[ij[0]] + b[ij[1]] for ij in _CANDIDATES}
    pos = {ij: float(ij[0] * P_TOPK + ij[1]) for ij in _CANDIDATES}
    work = dict(cand)
    tau = tau_pos = None
    for r in range(P_TOPK):
        tau = _tree(jnp.maximum, [work[ij] for ij in _CANDIDATES])
        tau_pos = _tree(jnp.minimum,
                        [jnp.where(work[ij] == tau, pos[ij], _BIG_POS) for ij in _CANDIDATES])
        if r < P_TOPK - 1:
            for ij in _CANDIDATES:
                work[ij] = jnp.where(tau_pos == pos[ij], -jnp.inf, work[ij])
    ea = [jnp.exp(a[i] - a[0]) for i in range(P_TOPK)]
    eb = [jnp.exp(b[j] - b[0]) for j in range(P_TOPK)]
    count = [jnp.zeros((ph, LANES), F32) for _ in range(P_TOPK)]
    z = jnp.zeros((ph, LANES), F32)
    for ij in _CANDIDATES:
        i, j = ij
        chosen = jnp.where(cand[ij] > tau, 1.0,
                           jnp.where(cand[ij] == tau,
                                     jnp.where(tau_pos >= pos[ij], 1.0, 0.0), 0.0))
        count[i] = count[i] + chosen
        z = z + chosen * (ea[i] * eb[j])
    inv_z = 1.0 / z

    @pl.when(undecided == 0.0)
    def _():
        for k in range(N_KEYS):
            rows = key_rows(k)
            s1 = s_scr[lt, 0, rows, :]
            reach = jnp.zeros((ph, LANES), F32)
            for i in range(P_TOPK):
                reach = jnp.where(s1 == a[i], count[i], reach)
            l1_ref[lt, rows, :] = reach
            s2 = s_scr[lt, 1, rows, :]
            r2_scr[rows, :] = _tree(jnp.add, [jnp.where(b[j] > s2, 1.0, 0.0)
                                              for j in range(P_TOPK)])

    @pl.when(undecided != 0.0)
    def _():
        sc_scr[...] = s_scr[lt, 0]
        _extract_top(sc_scr, r1_scr, top_scr)
        sc_scr[...] = s_scr[lt, 1]
        _extract_top(sc_scr, r2_scr, top_scr)
        for k in range(N_KEYS):
            rows = key_rows(k)
            r1 = r1_scr[rows, :]
            reach = jnp.zeros((ph, LANES), F32)
            for i in range(P_TOPK):
                reach = jnp.where(r1 == float(i), count[i], reach)
            l1_ref[lt, rows, :] = reach

    for k in range(N_KEYS):
        rows = key_rows(k)
        w1_ref[lt, rows, :] = jnp.exp(s_scr[lt, 0, rows, :] - a[0]) * inv_z
        sc_scr[rows, :] = jnp.exp(s_scr[lt, 1, rows, :] - b[0])
    for h in range(ph):
        r2_ref[lt, h] = _pack_bf16(r2_scr[pl.ds(h, N_KEYS, stride=ph), :])
        e2_ref[lt, h] = _pack_bf16(sc_scr[pl.ds(h, N_KEYS, stride=ph), :])


def _peer_topk(x, wq_t, subkeys):
    t = x.shape[0]
    nt = t // LANES
    tm = PEER_TOPK_TOKENS if t % PEER_TOPK_TOKENS == 0 else LANES
    n_tiles = tm // LANES
    tile = pl.BlockSpec((n_tiles, _HK, LANES), lambda i: (i, 0, 0))
    tile3 = pl.BlockSpec((n_tiles, P_HEADS, N_KEYS // 2, LANES), lambda i: (i, 0, 0, 0))
    flat = jax.ShapeDtypeStruct((nt, _HK, LANES), F32)
    cube = jax.ShapeDtypeStruct((nt, P_HEADS, N_KEYS // 2, LANES), jnp.uint32)
    return pl.pallas_call(
        _peer_topk_kernel,
        grid=(t // tm,),
        in_specs=[pl.BlockSpec((tm, D_MODEL), lambda i: (i, 0)),
                  pl.BlockSpec((2 * _HK, D_MODEL), lambda i: (0, 0)),
                  pl.BlockSpec((2 * P_HEADS, N_KEYS, D_KEY // 2), lambda i: (0, 0, 0))],
        out_specs=[tile, tile, tile3, tile3],
        out_shape=[flat, flat, cube, cube],
        scratch_shapes=[pltpu.VMEM((n_tiles, 2, _HK, LANES), F32),
                        pltpu.VMEM((_HK, LANES), F32),
                        pltpu.VMEM((_HK, LANES), F32),
                        pltpu.VMEM((_HK, LANES), F32),
                        pltpu.VMEM((P_TOPK, P_HEADS, LANES), F32)],
        compiler_params=_params("parallel"),
        name="peer_topk",
    )(x, wq_t, subkeys)


def _replicated_bf16(ref, tile, row):
    rep = ref[tile, pl.ds(row, SUBLANES, stride=0), :]
    return jnp.concatenate([rep, rep], axis=0).astype(BF16)


def _peer_mix_kernel(x_ref, u_ref, vt_ref, l1_ref, w1_ref, r2_ref, e2_ref, g_ref, b_ref,
                     out_ref, xt_scr, acc_scr, act_scr, p_scr):
    e = pl.program_id(1)
    n_groups, _, group = xt_scr.shape

    @pl.when(e == 0)
    def _():
        for gi in range(n_groups):
            xt_scr[gi] = x_ref[gi * group:(gi + 1) * group, :].T.astype(BF16)
        acc_scr[...] = jnp.zeros(acc_scr.shape, F32)

    def project(gi):
        act_scr[gi] = jnp.dot(u_ref[...], xt_scr[gi], preferred_element_type=F32)

    def mix(gi):
        acc_scr[gi] += jnp.dot(vt_ref[...], p_scr[gi], preferred_element_type=F32)

    lookahead = min(2, n_groups)
    for gi in range(lookahead):
        project(gi)
    for gi in range(n_groups):
        for c0 in range(0, group, LANES):
            cols = slice(c0, c0 + LANES)
            ct = (gi * group + c0) // LANES
            for kb in range(0, N_KEYS, PEER_KEY_BLOCK):
                subs = range(kb, kb + PEER_KEY_BLOCK, BF16_ROWS)
                gates = {(jj, k0): jnp.zeros((BF16_ROWS, LANES), BF16)
                         for jj in range(PEER_ROWS_PER_STEP) for k0 in subs}
                for h in range(P_HEADS):
                    rows_h = [jj * P_HEADS + h for jj in range(PEER_ROWS_PER_STEP)]
                    reach = [_replicated_bf16(l1_ref, ct, r) for r in rows_h]
                    weight = [_replicated_bf16(w1_ref, ct, r) for r in rows_h]
                    for k0 in subs:
                        words = slice(k0 // 2, k0 // 2 + SUBLANES)
                        r2 = _unpack_bf16(r2_ref[ct, h, words, :])
                        e2 = _unpack_bf16(e2_ref[ct, h, words, :])
                        for jj in range(PEER_ROWS_PER_STEP):
                            picked = jnp.where(r2 < reach[jj], e2, jnp.zeros_like(e2))
                            gates[jj, k0] = gates[jj, k0] + picked * weight[jj]
                for jj in range(PEER_ROWS_PER_STEP):
                    for k0 in subs:
                        rows = slice(jj * N_KEYS + k0, jj * N_KEYS + k0 + BF16_ROWS)
                        a = act_scr[gi, rows, cols]
                        gelu = 0.5 * a * (1.0 + lax.erf(a * (2.0 ** -0.5)))
                        p_scr[gi, rows, cols] = gelu.astype(BF16) * gates[jj, k0]
        if gi >= 1:
            mix(gi - 1)
        if gi + lookahead < n_groups:
            project(gi + lookahead)
    mix(n_groups - 1)

    @pl.when(e == pl.num_programs(1) - 1)
    def _():
        for gi in range(n_groups):
            rows = slice(gi * group, (gi + 1) * group)
            y = DN_ALPHA * x_ref[rows, :] + acc_scr[gi].T
            out_ref[rows, :] = _layer_norm(y, g_ref[...], b_ref[...])


def _peer_mix(x, u, vt, l1, w1, r2, e2, g, b, tm):
    t = x.shape[0]
    te = PEER_ROWS_PER_STEP * N_KEYS
    tr = PEER_ROWS_PER_STEP * P_HEADS
    nt = tm // LANES
    group = min(tm, PEER_COLUMN_GROUP)
    n_groups = tm // group
    return pl.pallas_call(
        _peer_mix_kernel,
        grid=(t // tm, N_EXPERTS // te),
        in_specs=[pl.BlockSpec((tm, D_MODEL), lambda i, e: (i, 0)),
                  pl.BlockSpec((te, D_MODEL), lambda i, e: (e, 0)),
                  pl.BlockSpec((D_MODEL, te), lambda i, e: (0, e)),
                  pl.BlockSpec((nt, tr, LANES), lambda i, e: (i, e, 0)),
                  pl.BlockSpec((nt, tr, LANES), lambda i, e: (i, e, 0)),
                  pl.BlockSpec((nt, P_HEADS, N_KEYS // 2, LANES), lambda i, e: (i, 0, 0, 0)),
                  pl.BlockSpec((nt, P_HEADS, N_KEYS // 2, LANES), lambda i, e: (i, 0, 0, 0)),
                  pl.BlockSpec((1, D_MODEL), lambda i, e: (0, 0)),
                  pl.BlockSpec((1, D_MODEL), lambda i, e: (0, 0))],
        out_specs=pl.BlockSpec((tm, D_MODEL), lambda i, e: (i, 0)),
        out_shape=jax.ShapeDtypeStruct((t, D_MODEL), F32),
        scratch_shapes=[pltpu.VMEM((n_groups, D_MODEL, group), BF16),
                        pltpu.VMEM((n_groups, D_MODEL, group), F32),
                        pltpu.VMEM((n_groups, te, group), F32),
                        pltpu.VMEM((n_groups, te, group), BF16)],
        compiler_params=_params("parallel", "arbitrary"),
        name="peer_mix",
    )(x, u, vt, l1, w1, r2, e2, g, b)


def _peer_layer(x, pw, g, b):
    t = x.shape[0]
    l1, w1, r2, e2 = _peer_topk(x, pw["wq_t"], pw["subkeys"])
    tm = PEER_TOKEN_TILE if t % PEER_TOKEN_TILE == 0 else LANES
    return _peer_mix(x, pw["u"], pw["vt"], l1, w1, r2, e2, g, b, tm)


def _t5_bucket(dist):
    max_exact = N_BUCKETS // 2
    d = jnp.maximum(dist, 0)
    df = jnp.maximum(d, 1).astype(F32)
    large = max_exact + (jnp.log(df / max_exact) / math.log(MAX_DISTANCE / max_exact)
                         * (N_BUCKETS - max_exact)).astype(jnp.int32)
    large = jnp.minimum(large, N_BUCKETS - 1)
    return jnp.where(d < max_exact, d, large)


def _row(a):
    return a.reshape(1, -1)


def _prepare(w):
    n_gate = 2 * A_HEADS
    w_in = w["a_w_in"][0]
    b_in = w["a_b_in"][0]
    prep = {
        "w_in_main": w_in[:, :2 * A_INNER].astype(BF16),
        "b_in_main": _row(b_in[:2 * A_INNER]),
        "w_in_gate": jnp.pad(w_in[:, 2 * A_INNER:], ((0, 0), (0, LANES - n_gate))).astype(BF16),
        "b_in_gate": _row(jnp.pad(b_in[2 * A_INNER:], (0, LANES - n_gate))),
        "conv_w": w["a_conv_w"][0],
        "conv_b": _row(w["a_conv_b"][0]),
        "wq": w["a_w_q"][0].astype(BF16),
        "wk": w["a_w_k"][0].astype(BF16),
        "wv": w["a_w_v"][0].astype(BF16),
        "w_out": w["a_w_out"][0].astype(BF16),
        "w_qkv": jnp.concatenate([w["b_w_q"][0], w["kv_w"]], axis=1).astype(BF16),
        "w_o": w["b_w_o"][0].astype(BF16),
        "sinks": _row(w["b_sinks"][0]),
        "rel_bias": w["rel_bias"],
    }
    peer = []
    half = D_KEY // 2
    for layer in range(DEPTH):
        wq = w["peer_w_q"][layer].reshape(D_MODEL, P_HEADS, 2, half)
        sk = w["peer_subkeys"][layer]
        peer.append({
            "wq_t": wq.transpose(2, 1, 3, 0).reshape(2 * _HK, D_MODEL).astype(BF16),
            "subkeys": sk.reshape(2 * P_HEADS, N_KEYS, half).astype(BF16),
            "u": w["peer_u"][layer].astype(BF16),
            "vt": w["peer_v"][layer].T.astype(BF16),
        })
    prep["peer"] = peer
    return prep


def _zero_bias(n):
    return jnp.zeros((1, n), F32)


def _attention_tables(rel_bias):
    qi = jnp.arange(WINDOW)[:, None]
    kj = jnp.arange(2 * WINDOW)[None, :]
    bucket = _t5_bucket(qi + WINDOW - kj).astype(jnp.int32)
    return _bias_tables(bucket, rel_bias)


def _prompt_trunk(x, w, p, bias):
    batch, length, d = x.shape
    t = batch * length
    xt = x.reshape(t, d)
    proj = _linear(xt, p["w_in_main"], p["b_in_main"], 512, 512)
    gates = _linear(xt, p["w_in_gate"], p["b_in_gate"], 512, LANES)
    row_tab, col_tab = _gate_tables(gates, batch, length, MLSTM_CHUNK)
    q, k, v = _conv_qkv(proj, p["conv_w"], p["conv_b"], p["wq"], p["wk"], p["wv"], batch, length)
    h, c_new, n_new, m_new = _mlstm_prompt(q, k, v, row_tab, col_tab, batch, length, MLSTM_CHUNK)
    x1 = _outproj_ln(h, proj, p["w_out"], xt, _row(w["ln_mix_g"][0]), _row(w["ln_mix_b"][0]), 256)
    x2 = _peer_layer(x1, p["peer"][0], _row(w["ln_ffn_g"][0]), _row(w["ln_ffn_b"][0]))
    qkv = _linear(x2, p["w_qkv"], _zero_bias(p["w_qkv"].shape[1]), 512, 512)
    o = _swa_prompt(qkv, bias, p["sinks"], batch, length)
    x3 = _outproj_ln(o, None, p["w_o"], x2, _row(w["ln_mix_g"][1]), _row(w["ln_mix_b"][1]), 256)
    x4 = _peer_layer(x3, p["peer"][1], _row(w["ln_ffn_g"][1]), _row(w["ln_ffn_b"][1]))

    kvw = B_KV_HEADS * B_HEAD_DIM
    qkv3 = qkv.reshape(batch, length, -1)
    k_win = qkv3[:, -WINDOW:, D_MODEL:D_MODEL + kvw].reshape(batch, WINDOW, B_KV_HEADS, B_HEAD_DIM)
    v_win = qkv3[:, -WINDOW:, D_MODEL + kvw:].reshape(batch, WINDOW, B_KV_HEADS, B_HEAD_DIM)
    conv = proj.reshape(batch, length, -1)[:, -(A_CONV_W - 1):, :A_INNER]
    return (x4.reshape(batch, length, d),
            c_new[None],
            n_new.reshape(1, batch, A_HEADS, A_HEAD_DIM),
            m_new[:, :, 0, 0][None],
            conv[None], k_win, v_win)


def _sample_trunk(x, conv0, c0, n0, m0, k_buf, v_buf, w, p, bias_t):
    batch, length, d = x.shape
    xt = x.reshape(batch, d)
    proj = _linear(xt, p["w_in_main"], p["b_in_main"], batch, 512)
    gates = _linear(xt, p["w_in_gate"], p["b_in_gate"], batch, LANES)
    buf = conv0[0]
    q, k, v, qt, kt = _sample_conv_qkv(proj, buf.reshape(batch, -1), p["conv_w"], p["conv_b"],
                                       p["wq"], p["wk"], p["wv"])
    h, c_new, n_new, m_new = _sample_mlstm(q, k, v, qt, kt, gates, m0[0], n0[0], c0[0])
    x1 = _outproj_ln(h.reshape(batch, A_INNER), proj, p["w_out"], xt,
                     _row(w["ln_mix_g"][0]), _row(w["ln_mix_b"][0]), batch)
    x2 = _peer_layer(x1, p["peer"][0], _row(w["ln_ffn_g"][0]), _row(w["ln_ffn_b"][0]))
    qkv = _linear(x2, p["w_qkv"], _zero_bias(p["w_qkv"].shape[1]), batch, 512)
    kvw = B_KV_HEADS * B_HEAD_DIM
    o = _swa_decode(qkv, k_buf.reshape(batch, WINDOW, kvw), v_buf.reshape(batch, WINDOW, kvw),
                    bias_t, p["sinks"])
    x3 = _outproj_ln(o, None, p["w_o"], x2, _row(w["ln_mix_g"][1]), _row(w["ln_mix_b"][1]), batch)
    x4 = _peer_layer(x3, p["peer"][1], _row(w["ln_ffn_g"][1]), _row(w["ln_ffn_b"][1]))

    k_new = qkv[:, D_MODEL:D_MODEL + kvw].reshape(batch, 1, B_KV_HEADS, B_HEAD_DIM)
    v_new = qkv[:, D_MODEL + kvw:].reshape(batch, 1, B_KV_HEADS, B_HEAD_DIM)
    k_win = jnp.concatenate([k_buf[:, 1:], k_new], axis=1)
    v_win = jnp.concatenate([v_buf[:, 1:], v_new], axis=1)
    conv = jnp.concatenate([buf[:, 1:], proj[:, None, :A_INNER]], axis=1)
    return (x4.reshape(batch, length, d),
            c_new[None],
            n_new.reshape(1, batch, A_HEADS, A_HEAD_DIM),
            m_new[:, :, 0, 0][None],
            conv[None], k_win, v_win)


def kernel(x_prompt, x_sample, state_mlstm_C, state_mlstm_n, state_mlstm_m, state_mlstm_conv,
           cache_k_win, cache_v_win, a_w_in, a_b_in, a_conv_w, a_conv_b, a_w_q, a_w_k, a_w_v,
           a_w_out, kv_w, b_w_q, b_w_o, b_sinks, rel_bias, ln_mix_g, ln_mix_b, ln_ffn_g,
           ln_ffn_b, peer_w_q, peer_subkeys, peer_u, peer_v):
    w = {"a_w_in": a_w_in, "a_b_in": a_b_in, "a_conv_w": a_conv_w, "a_conv_b": a_conv_b,
         "a_w_q": a_w_q, "a_w_k": a_w_k, "a_w_v": a_w_v, "a_w_out": a_w_out, "kv_w": kv_w,
         "b_w_q": b_w_q, "b_w_o": b_w_o, "b_sinks": b_sinks, "rel_bias": rel_bias,
         "ln_mix_g": ln_mix_g, "ln_mix_b": ln_mix_b, "ln_ffn_g": ln_ffn_g, "ln_ffn_b": ln_ffn_b,
         "peer_w_q": peer_w_q, "peer_subkeys": peer_subkeys, "peer_u": peer_u, "peer_v": peer_v}
    p = _prepare(w)
    bias, bias_t = _attention_tables(rel_bias)
    prompt = _prompt_trunk(x_prompt, w, p, bias)
    sample = _sample_trunk(x_sample, state_mlstm_conv, state_mlstm_C, state_mlstm_n,
                           state_mlstm_m, cache_k_win, cache_v_win, w, p, bias_t)
    return (prompt[0], sample[0]) + prompt[1:] + sample[1:]
```

```python
import functools
import math

import jax
import jax.numpy as jnp
from jax import lax
from jax.experimental import pallas as pl
from jax.experimental.pallas import tpu as pltpu

D_MODEL = 1024
DEPTH = 2
A_HEADS = 4
A_INNER = 2 * D_MODEL
A_HEAD_DIM = A_INNER // A_HEADS
A_CONV_W = 4
B_HEADS = 16
B_KV_HEADS = 4
B_GROUP = B_HEADS // B_KV_HEADS
B_HEAD_DIM = D_MODEL // B_HEADS
WINDOW = 128
N_BUCKETS = 32
MAX_DISTANCE = 128
P_HEADS = 8
N_KEYS = 128
N_EXPERTS = N_KEYS * N_KEYS
D_KEY = 256
P_TOPK = 16
DN_ALPHA = (2 * DEPTH) ** 0.25
LN_EPS = 1e-5
NEG_INF = -1e30

LANES = 128
SUBLANES = 8
BF16_ROWS = 2 * SUBLANES
VMEM_LIMIT_BYTES = 48 * 1024 * 1024

LINEAR_COLUMNS = 2048
MLSTM_CHUNK = 256
PEER_TOPK_TOKENS = 256
PEER_TOKEN_TILE = 1024
PEER_ROWS_PER_STEP = 4
PEER_COLUMN_GROUP = 256
PEER_KEY_BLOCK = 32
NOT_SELECTED_RANK = float(P_TOPK)

F32 = jnp.float32
BF16 = jnp.bfloat16


def _params(*sem):
    return pltpu.CompilerParams(dimension_semantics=sem, vmem_limit_bytes=VMEM_LIMIT_BYTES)


def _tree(op, vals):
    vals = list(vals)
    while len(vals) > 1:
        nxt = [op(vals[i], vals[i + 1]) for i in range(0, len(vals) - 1, 2)]
        if len(vals) % 2:
            nxt.append(vals[-1])
        vals = nxt
    return vals[0]


def _layer_norm(y, g, b):
    mu = jnp.mean(y, axis=-1, keepdims=True)
    yc = y - mu
    var = jnp.mean(yc * yc, axis=-1, keepdims=True)
    return yc * lax.rsqrt(var + LN_EPS) * g + b


def _linear_kernel(x_ref, w_ref, b_ref, o_ref):
    x = x_ref[...].astype(BF16)
    o_ref[...] = jnp.dot(x, w_ref[...], preferred_element_type=F32) + b_ref[...]


def _linear(x, w, b, tm, tn):
    t, k = x.shape
    n = w.shape[1]
    return pl.pallas_call(
        _linear_kernel,
        grid=(t // tm, n // tn),
        in_specs=[pl.BlockSpec((tm, k), lambda i, j: (i, 0)),
                  pl.BlockSpec((k, tn), lambda i, j: (0, j)),
                  pl.BlockSpec((1, tn), lambda i, j: (0, j))],
        out_specs=pl.BlockSpec((tm, tn), lambda i, j: (i, j)),
        out_shape=jax.ShapeDtypeStruct((t, n), F32),
        compiler_params=_params("parallel", "arbitrary"),
        name="linear",
    )(x, w, b)


def _outproj_ln_kernel(gated, *refs):
    if gated:
        a_ref, o_ref, w_ref, res_ref, g_ref, b_ref, out_ref = refs
        act = jax.nn.sigmoid(o_ref[...]) * a_ref[...]
    else:
        a_ref, w_ref, res_ref, g_ref, b_ref, out_ref = refs
        act = a_ref[...]
    sub = jnp.dot(act.astype(BF16), w_ref[...], preferred_element_type=F32)
    out_ref[...] = _layer_norm(DN_ALPHA * res_ref[...] + sub, g_ref[...], b_ref[...])


def _outproj_ln(act, gate_src, w, res, g, b, tm):
    t, k = act.shape
    d = w.shape[1]
    gated = gate_src is not None
    in_specs = [pl.BlockSpec((tm, k), lambda i: (i, 0))]
    args = [act]
    if gated:
        in_specs.append(pl.BlockSpec((tm, k), lambda i: (i, 1)))
        args.append(gate_src)
    in_specs += [pl.BlockSpec((k, d), lambda i: (0, 0)),
                 pl.BlockSpec((tm, d), lambda i: (i, 0)),
                 pl.BlockSpec((1, d), lambda i: (0, 0)),
                 pl.BlockSpec((1, d), lambda i: (0, 0))]
    args += [w, res, g, b]
    return pl.pallas_call(
        functools.partial(_outproj_ln_kernel, gated),
        grid=(t // tm,),
        in_specs=in_specs,
        out_specs=pl.BlockSpec((tm, d), lambda i: (i, 0)),
        out_shape=jax.ShapeDtypeStruct((t, d), F32),
        compiler_params=_params("parallel"),
        name="outproj_ln",
    )(*args)


def _log_sigmoid(x):
    return jnp.minimum(x, 0.0) - jnp.log1p(jnp.exp(-jnp.abs(x)))


def _gates_kernel(chunk, g_ref, row_ref, col_ref):
    length = g_ref.shape[0]
    gt = g_ref[...].T
    top = gt[0:SUBLANES]
    row = lax.broadcasted_iota(jnp.int32, top.shape, 0)
    pos = lax.broadcasted_iota(jnp.int32, top.shape, 1) % chunk
    is_f = row >= A_HEADS
    x = jnp.where(is_f, _log_sigmoid(top), 0.0)
    shift = 1
    while shift < chunk:
        x = x + jnp.where(pos >= shift, pltpu.roll(x, shift, 1), 0.0)
        shift *= 2
    table = jnp.where(is_f, x, top)
    row_ref[...] = table
    padded = jnp.concatenate([table, jnp.zeros((LANES - SUBLANES, length), F32)], axis=0)
    col_ref[...] = padded.T


def _gate_tables(gates, batch, length, chunk):
    return pl.pallas_call(
        functools.partial(_gates_kernel, chunk),
        grid=(batch,),
        in_specs=[pl.BlockSpec((length, LANES), lambda b: (b, 0))],
        out_specs=[pl.BlockSpec((None, SUBLANES, length), lambda b: (b, 0, 0)),
                   pl.BlockSpec((length, LANES), lambda b: (b, 0))],
        out_shape=[jax.ShapeDtypeStruct((batch, SUBLANES, length), F32),
                   jax.ShapeDtypeStruct((batch * length, LANES), F32)],
        compiler_params=_params("parallel"),
        name="gate_tables",
    )(gates)


CONV_ROW_CHUNK = 512
CONV_PAD = SUBLANES


def _conv_qkv_kernel(xm_ref, cw_ref, cb_ref, wq_ref, wk_ref, wv_ref,
                     q_ref, k_ref, v_ref, pad_ref):
    length = xm_ref.shape[0]
    pad_ref[0:CONV_PAD, :] = jnp.zeros((CONV_PAD, A_HEAD_DIM), F32)
    pad_ref[CONV_PAD:CONV_PAD + length, :] = xm_ref[...]
    first = CONV_PAD - (A_CONV_W - 1)
    for c0 in range(0, length, CONV_ROW_CHUNK):
        acc = cb_ref[...]
        for w in range(A_CONV_W):
            acc = acc + pad_ref[c0 + first + w:c0 + first + w + CONV_ROW_CHUNK, :] * cw_ref[w:w + 1, :]
        xc = (acc * jax.nn.sigmoid(acc)).astype(BF16)
        xm = xm_ref[c0:c0 + CONV_ROW_CHUNK, :].astype(BF16)
        rows = slice(c0, c0 + CONV_ROW_CHUNK)
        q_ref[rows, :] = jnp.dot(xc, wq_ref[...], preferred_element_type=F32).astype(BF16)
        k = jnp.dot(xc, wk_ref[...], preferred_element_type=F32) * (A_HEAD_DIM ** -0.5)
        k_ref[rows, :] = k.astype(BF16)
        v_ref[rows, :] = jnp.dot(xm, wv_ref[...], preferred_element_type=F32).astype(BF16)


def _conv_qkv(proj, cw, cb, wq, wk, wv, batch, length):
    t = batch * length
    hd = A_HEAD_DIM
    tok = pl.BlockSpec((length, hd), lambda b, h: (b, h))
    wspec = pl.BlockSpec((None, hd, hd), lambda b, h: (h, 0, 0))
    out = jax.ShapeDtypeStruct((t, A_INNER), BF16)
    return pl.pallas_call(
        _conv_qkv_kernel,
        grid=(batch, A_HEADS),
        in_specs=[tok,
                  pl.BlockSpec((A_CONV_W, hd), lambda b, h: (0, h)),
                  pl.BlockSpec((1, hd), lambda b, h: (0, h)),
                  wspec, wspec, wspec],
        out_specs=[tok, tok, tok],
        out_shape=[out, out, out],
        scratch_shapes=[pltpu.VMEM((length + CONV_PAD, hd), F32)],
        compiler_params=_params("parallel", "arbitrary"),
        name="conv_qkv",
    )(proj, cw, cb, wq, wk, wv)


def _mlstm_kernel(q_ref, k_ref, v_ref, row_ref, col_ref,
                  h_ref, c_out_ref, n_out_ref, m_out_ref,
                  c_scr, n_scr, m_scr):
    ci = pl.program_id(1)
    chunk = q_ref.shape[0]
    hd = A_HEAD_DIM

    @pl.when(ci == 0)
    def _():
        c_scr[...] = jnp.zeros(c_scr.shape, F32)
        n_scr[...] = jnp.zeros(n_scr.shape, F32)
        m_scr[...] = jnp.zeros(m_scr.shape, F32)

    t_idx = lax.broadcasted_iota(jnp.int32, (chunk, chunk), 0)
    s_idx = lax.broadcasted_iota(jnp.int32, (chunk, chunk), 1)
    causal = s_idx <= t_idx
    for h in range(A_HEADS):
        cols = slice(h * hd, (h + 1) * hd)
        qh, kh, vh = q_ref[:, cols], k_ref[:, cols], v_ref[:, cols]
        i_col = col_ref[:, h:h + 1]
        f_col = col_ref[:, A_HEADS + h:A_HEADS + h + 1]
        i_row = row_ref[h:h + 1, :]
        f_row = row_ref[A_HEADS + h:A_HEADS + h + 1, :]
        m_prev = m_scr[h][:, 0:1]
        d = jnp.where(causal, f_col - f_row + i_row, NEG_INF)
        b_inter = f_col + m_prev
        m_t = jnp.maximum(b_inter, jnp.max(d, axis=1, keepdims=True))
        qk = lax.dot_general(qh, kh, (((1,), (1,)), ((), ())), preferred_element_type=F32)
        s = qk * jnp.exp(d - m_t)
        w_inter = jnp.exp(b_inter - m_t)
        q_c = jnp.dot(qh, c_scr[h].astype(BF16), preferred_element_type=F32)
        num = jnp.dot(s.astype(BF16), vh, preferred_element_type=F32) + w_inter * q_c
        q_n = jnp.sum(qh.astype(F32) * n_scr[h], axis=1, keepdims=True)
        den = jnp.sum(s, axis=1, keepdims=True) + w_inter * q_n
        h_ref[:, cols] = num / jnp.maximum(jnp.abs(den), jnp.exp(-m_t))
        f_last = f_col[chunk - 1:chunk, :]
        g = f_last - f_col + i_col
        m_new = jnp.maximum(f_last + m_prev, jnp.max(g, axis=0, keepdims=True))
        decay = jnp.exp(f_last + m_prev - m_new)
        wk = jnp.exp(g - m_new) * kh.astype(F32)
        kv = lax.dot_general(wk.astype(BF16), vh, (((0,), (0,)), ((), ())),
                             preferred_element_type=F32)
        c_scr[h] = decay * c_scr[h] + kv
        n_scr[h] = decay * n_scr[h] + jnp.sum(wk, axis=0, keepdims=True)
        m_scr[h] = jnp.broadcast_to(m_new, (1, LANES))

    @pl.when(ci == pl.num_programs(1) - 1)
    def _():
        c_out_ref[...] = c_scr[...]
        n_out_ref[...] = n_scr[...]
        m_out_ref[...] = m_scr[...]


def _mlstm_prompt(q, k, v, row_tab, col_tab, batch, length, chunk):
    nc = length // chunk
    t = batch * length
    tok = pl.BlockSpec((chunk, A_INNER), lambda b, c: (b * nc + c, 0))
    return pl.pallas_call(
        _mlstm_kernel,
        grid=(batch, nc),
        in_specs=[tok, tok, tok,
                  pl.BlockSpec((None, SUBLANES, chunk), lambda b, c: (b, 0, c)),
                  pl.BlockSpec((chunk, LANES), lambda b, c: (b * nc + c, 0))],
        out_specs=[tok,
                   pl.BlockSpec((None, A_HEADS, A_HEAD_DIM, A_HEAD_DIM), lambda b, c: (b, 0, 0, 0)),
                   pl.BlockSpec((None, A_HEADS, 1, A_HEAD_DIM), lambda b, c: (b, 0, 0, 0)),
                   pl.BlockSpec((None, A_HEADS, 1, LANES), lambda b, c: (b, 0, 0, 0))],
        out_shape=[jax.ShapeDtypeStruct((t, A_INNER), F32),
                   jax.ShapeDtypeStruct((batch, A_HEADS, A_HEAD_DIM, A_HEAD_DIM), F32),
                   jax.ShapeDtypeStruct((batch, A_HEADS, 1, A_HEAD_DIM), F32),
                   jax.ShapeDtypeStruct((batch, A_HEADS, 1, LANES), F32)],
        scratch_shapes=[pltpu.VMEM((A_HEADS, A_HEAD_DIM, A_HEAD_DIM), F32),
                        pltpu.VMEM((A_HEADS, 1, A_HEAD_DIM), F32),
                        pltpu.VMEM((A_HEADS, 1, LANES), F32)],
        compiler_params=_params("parallel", "arbitrary"),
        name="mlstm_prompt",
    )(q, k, v, row_tab, col_tab)


def _sample_conv_qkv_kernel(proj_ref, buf_ref, cw_ref, cb_ref, wq_ref, wk_ref, wv_ref,
                            q_ref, k_ref, v_ref, qt_ref, kt_ref):
    xm = proj_ref[:, 0:A_INNER]
    acc = cb_ref[...] + xm * cw_ref[A_CONV_W - 1:A_CONV_W, :]
    for w in range(A_CONV_W - 1):
        acc = acc + buf_ref[:, w * A_INNER:(w + 1) * A_INNER] * cw_ref[w:w + 1, :]
    xc = (acc * jax.nn.sigmoid(acc)).astype(BF16)
    xmb = xm.astype(BF16)
    for h in range(A_HEADS):
        cols = slice(h * A_HEAD_DIM, (h + 1) * A_HEAD_DIM)
        q_ref[:, cols] = jnp.dot(xc[:, cols], wq_ref[h], preferred_element_type=F32)
        k_ref[:, cols] = (jnp.dot(xc[:, cols], wk_ref[h], preferred_element_type=F32)
                          * (A_HEAD_DIM ** -0.5))
        v_ref[:, cols] = jnp.dot(xmb[:, cols], wv_ref[h], preferred_element_type=F32)
    qt_ref[...] = q_ref[...].T
    kt_ref[...] = k_ref[...].T


def _sample_conv_qkv(proj, conv_buf_flat, cw, cb, wq, wk, wv):
    b = proj.shape[0]
    row = jax.ShapeDtypeStruct((b, A_INNER), F32)
    col = jax.ShapeDtypeStruct((A_INNER, b), F32)
    return pl.pallas_call(
        _sample_conv_qkv_kernel,
        out_shape=[row, row, row, col, col],
        compiler_params=pltpu.CompilerParams(vmem_limit_bytes=VMEM_LIMIT_BYTES),
        name="sample_conv_qkv",
    )(proj, conv_buf_flat, cw, cb, wq, wk, wv)


def _sample_mlstm_kernel(q_ref, k_ref, v_ref, qt_ref, kt_ref, g_ref, m0_ref, n0_ref, c0_ref,
                         h_ref, c_ref, n_ref, m_ref):
    b = pl.program_id(0)
    h = pl.program_id(1)
    hd = A_HEAD_DIM
    is_b = lax.broadcasted_iota(jnp.int32, (hd, LANES), 1) == b
    q_col = jnp.sum(jnp.where(is_b, qt_ref[...], 0.0), axis=1, keepdims=True)
    k_col = jnp.sum(jnp.where(is_b, kt_ref[...], 0.0), axis=1, keepdims=True)
    q_mat = jnp.broadcast_to(q_col, (hd, hd))
    k_mat = jnp.broadcast_to(k_col, (hd, hd))
    q_row, k_row, v_row = q_ref[...], k_ref[...], v_ref[...]
    lane = lax.broadcasted_iota(jnp.int32, (1, LANES), 1)
    gates = g_ref[...]
    log_i = jnp.sum(jnp.where(lane == h, gates, 0.0), axis=1, keepdims=True)
    f_pre = jnp.sum(jnp.where(lane == h + A_HEADS, gates, 0.0), axis=1, keepdims=True)
    log_f = _log_sigmoid(f_pre)
    lane_h = lax.broadcasted_iota(jnp.int32, (1, A_HEADS), 1)
    m0 = jnp.sum(jnp.where(lane_h == h, m0_ref[...], 0.0), axis=1, keepdims=True)
    m_t = jnp.maximum(log_f + m0, log_i)
    w_inter = jnp.exp(log_f + m0 - m_t)
    w_new = jnp.exp(log_i - m_t)
    c0 = c0_ref[...]
    n0 = n0_ref[...]
    s = jnp.sum(q_row * k_row, axis=1, keepdims=True) * w_new
    q_c = jnp.sum(q_mat * c0, axis=0, keepdims=True)
    q_n = jnp.sum(q_row * n0, axis=1, keepdims=True)
    num = s * v_row + w_inter * q_c
    den = s + w_inter * q_n
    h_ref[...] = num / jnp.maximum(jnp.abs(den), jnp.exp(-m_t))
    c_ref[...] = w_inter * c0 + (w_new * k_mat) * v_row
    n_ref[...] = w_inter * n0 + w_new * k_row
    m_ref[...] = jnp.broadcast_to(m_t, (1, LANES))


def _sample_mlstm(q, k, v, qt, kt, gates, m0, n0, c0):
    b = q.shape[0]
    hd = A_HEAD_DIM
    row3 = lambda a: a.reshape(b, 1, a.shape[-1])
    rspec = pl.BlockSpec((None, 1, hd), lambda i, h: (i, 0, h))
    cspec = pl.BlockSpec((hd, LANES), lambda i, h: (h, 0))
    nspec = pl.BlockSpec((None, None, 1, hd), lambda i, h: (i, h, 0, 0))
    mspec = pl.BlockSpec((None, None, 1, LANES), lambda i, h: (i, h, 0, 0))
    big = pl.BlockSpec((None, None, hd, hd), lambda i, h: (i, h, 0, 0))
    return pl.pallas_call(
        _sample_mlstm_kernel,
        grid=(b, A_HEADS),
        in_specs=[rspec, rspec, rspec, cspec, cspec,
                  pl.BlockSpec((None, 1, LANES), lambda i, h: (i, 0, 0)),
                  pl.BlockSpec((None, 1, A_HEADS), lambda i, h: (i, 0, 0)),
                  nspec, big],
        out_specs=[rspec, big, nspec, mspec],
        out_shape=[jax.ShapeDtypeStruct((b, 1, A_INNER), F32),
                   jax.ShapeDtypeStruct((b, A_HEADS, hd, hd), F32),
                   jax.ShapeDtypeStruct((b, A_HEADS, 1, hd), F32),
                   jax.ShapeDtypeStruct((b, A_HEADS, 1, LANES), F32)],
        compiler_params=_params("parallel", "arbitrary"),
        name="sample_mlstm",
    )(row3(q), row3(k), row3(v), qt, kt, row3(gates), row3(m0),
      n0.reshape(b, A_HEADS, 1, hd), c0)


def _bias_kernel(bucket_ref, rel_ref, bias_ref):
    h = pl.program_id(0)
    bucket = bucket_ref[...]
    acc = jnp.zeros(bucket.shape, F32)
    for n in range(N_BUCKETS):
        acc = jnp.where(bucket == n, rel_ref[n, h], acc)
    bias_ref[...] = acc


def _bias_tables(bucket, rel_bias):
    w, w2 = bucket.shape
    return pl.pallas_call(
        _bias_kernel,
        grid=(B_HEADS,),
        in_specs=[pl.BlockSpec((w, w2), lambda h: (0, 0)),
                  pl.BlockSpec(memory_space=pltpu.SMEM)],
        out_specs=pl.BlockSpec((None, w, w2), lambda h: (h, 0, 0)),
        out_shape=jax.ShapeDtypeStruct((B_HEADS, w, w2), F32),
        compiler_params=_params("arbitrary"),
        name="bias_tables",
    )(bucket, rel_bias)


def _swa_prompt_kernel(q_ref, kp_ref, kc_ref, vp_ref, vc_ref, bias_ref, sink_ref, o_ref):
    n = pl.program_id(1)
    w = WINDOW
    q = q_ref[...].astype(BF16)
    kk = jnp.concatenate([kp_ref[...], kc_ref[...]], axis=0).astype(BF16)
    vv = jnp.concatenate([vp_ref[...], vc_ref[...]], axis=0).astype(BF16)
    qi = lax.broadcasted_iota(jnp.int32, (w, 2 * w), 0)
    kj = lax.broadcasted_iota(jnp.int32, (w, 2 * w), 1)
    dist = qi + w - kj
    valid = (dist >= 0) & (dist < w) & ((kj >= w) | (n > 0))
    for h in range(B_HEADS):
        kvh = h // B_GROUP
        qh = q[:, h * B_HEAD_DIM:(h + 1) * B_HEAD_DIM]
        kh = kk[:, kvh * B_HEAD_DIM:(kvh + 1) * B_HEAD_DIM]
        vh = vv[:, kvh * B_HEAD_DIM:(kvh + 1) * B_HEAD_DIM]
        s = lax.dot_general(qh, kh, (((1,), (1,)), ((), ())), preferred_element_type=F32)
        s = jnp.where(valid, s * (B_HEAD_DIM ** -0.5) + bias_ref[h], NEG_INF)
        sink = sink_ref[0, h]
        m = jnp.maximum(jnp.max(s, axis=1, keepdims=True), sink)
        p = jnp.exp(s - m)
        den = jnp.sum(p, axis=1, keepdims=True) + jnp.exp(sink - m)
        p = (p / den).astype(BF16)
        o_ref[:, h * B_HEAD_DIM:(h + 1) * B_HEAD_DIM] = jnp.dot(p, vh, preferred_element_type=F32)


def _swa_prompt(qkv, bias, sinks, batch, length):
    nb = length // WINDOW
    kvw = B_KV_HEADS * B_HEAD_DIM
    kcol = D_MODEL // kvw
    cur = lambda b, n: b * nb + n
    prev = lambda b, n: b * nb + jnp.maximum(n - 1, 0)
    return pl.pallas_call(
        _swa_prompt_kernel,
        grid=(batch, nb),
        in_specs=[pl.BlockSpec((WINDOW, D_MODEL), lambda b, n: (cur(b, n), 0)),
                  pl.BlockSpec((WINDOW, kvw), lambda b, n: (prev(b, n), kcol)),
                  pl.BlockSpec((WINDOW, kvw), lambda b, n: (cur(b, n), kcol)),
                  pl.BlockSpec((WINDOW, kvw), lambda b, n: (prev(b, n), kcol + 1)),
                  pl.BlockSpec((WINDOW, kvw), lambda b, n: (cur(b, n), kcol + 1)),
                  pl.BlockSpec((B_HEADS, WINDOW, 2 * WINDOW), lambda b, n: (0, 0, 0)),
                  pl.BlockSpec(memory_space=pltpu.SMEM)],
        out_specs=pl.BlockSpec((WINDOW, D_MODEL), lambda b, n: (cur(b, n), 0)),
        out_shape=jax.ShapeDtypeStruct((batch * length, D_MODEL), F32),
        compiler_params=_params("parallel", "arbitrary"),
        name="swa_prompt",
    )(qkv, qkv, qkv, qkv, qkv, bias, sinks)


DECODE_BATCH_BLOCK = 8


def _swa_decode_kernel(qkv_ref, ck_ref, cv_ref, bias_ref, sink_ref, o_ref):
    w = WINDOW
    hd = B_HEAD_DIM
    kvw = B_KV_HEADS * hd
    scale = hd ** -0.5
    pos = lax.broadcasted_iota(jnp.int32, (B_GROUP, w), 1)
    for bi in range(DECODE_BATCH_BLOCK):
        row = qkv_ref[bi:bi + 1, :]
        for kvh in range(B_KV_HEADS):
            heads = slice(kvh * B_GROUP, (kvh + 1) * B_GROUP)
            kv_cols = slice(kvh * hd, (kvh + 1) * hd)
            q = jnp.concatenate([row[:, h * hd:(h + 1) * hd]
                                 for h in range(kvh * B_GROUP, (kvh + 1) * B_GROUP)], axis=0)
            k_new = row[:, D_MODEL + kvh * hd:D_MODEL + (kvh + 1) * hd]
            v_new = row[:, D_MODEL + kvw + kvh * hd:D_MODEL + kvw + (kvh + 1) * hd]
            kc = ck_ref[bi, :, kv_cols].astype(BF16)
            vc = cv_ref[bi, :, kv_cols].astype(BF16)
            s_c = lax.dot_general(q.astype(BF16), kc, (((1,), (1,)), ((), ())),
                                  preferred_element_type=F32)
            s_c = jnp.where(pos >= 1, s_c * scale + bias_ref[heads, 0:w], NEG_INF)
            s_n = jnp.sum(q * k_new, axis=1, keepdims=True) * scale + bias_ref[heads, w:w + 1]
            sink = sink_ref[heads, :]
            m = jnp.maximum(jnp.maximum(jnp.max(s_c, axis=1, keepdims=True), s_n), sink)
            p_c = jnp.exp(s_c - m)
            p_n = jnp.exp(s_n - m)
            den = jnp.sum(p_c, axis=1, keepdims=True) + p_n + jnp.exp(sink - m)
            o = (jnp.dot(p_c.astype(BF16), vc, preferred_element_type=F32) + p_n * v_new) / den
            for g in range(B_GROUP):
                h = kvh * B_GROUP + g
                o_ref[bi:bi + 1, h * hd:(h + 1) * hd] = o[g:g + 1, :]


def _swa_decode(qkv, cache_k, cache_v, bias_row, sinks_col):
    b = qkv.shape[0]
    kvw = B_KV_HEADS * B_HEAD_DIM
    bb = DECODE_BATCH_BLOCK
    return pl.pallas_call(
        _swa_decode_kernel,
        grid=(b // bb,),
        in_specs=[pl.BlockSpec((bb, qkv.shape[1]), lambda i: (i, 0)),
                  pl.BlockSpec((bb, WINDOW, kvw), lambda i: (i, 0, 0)),
                  pl.BlockSpec((bb, WINDOW, kvw), lambda i: (i, 0, 0)),
                  pl.BlockSpec((B_HEADS, 2 * WINDOW), lambda i: (0, 0)),
                  pl.BlockSpec((B_HEADS, 1), lambda i: (0, 0))],
        out_specs=pl.BlockSpec((bb, D_MODEL), lambda i: (i, 0)),
        out_shape=jax.ShapeDtypeStruct((b, D_MODEL), F32),
        compiler_params=_params("parallel"),
        name="swa_decode",
    )(qkv, cache_k, cache_v, bias_row, sinks_col)


_CANDIDATES = [(i, j) for i in range(P_TOPK) for j in range(P_TOPK)
               if (i + 1) * (j + 1) <= P_TOPK]
_BIG_POS = float(4 * P_TOPK * P_TOPK)
_HK = P_HEADS * N_KEYS


def _pack_bf16(x):
    return pltpu.bitcast(x.astype(BF16), jnp.uint32)


def _unpack_bf16(words):
    return pltpu.bitcast(words, BF16)


def _extract_top(s_ref, r_ref, top_ref):
    ph = P_HEADS
    for k in range(N_KEYS):
        r_ref[k * ph:(k + 1) * ph, :] = jnp.full((ph, LANES), NOT_SELECTED_RANK, F32)

    def body(r, carry):
        best = _tree(jnp.maximum, [s_ref[k * ph:(k + 1) * ph, :] for k in range(N_KEYS)])
        first = _tree(jnp.minimum,
                      [jnp.where(s_ref[k * ph:(k + 1) * ph, :] == best, float(k), float(N_KEYS))
                       for k in range(N_KEYS)])
        rank = lax.convert_element_type(r, F32)
        for k in range(N_KEYS):
            rows = slice(k * ph, (k + 1) * ph)
            hit = first == float(k)
            s_ref[rows, :] = jnp.where(hit, -jnp.inf, s_ref[rows, :])
            r_ref[rows, :] = jnp.where(hit, rank, r_ref[rows, :])
        top_ref[r] = best
        return carry

    lax.fori_loop(0, P_TOPK, body, 0)


def _sort_desc(vals):
    a = list(vals)
    n = len(a)
    k = 2
    while k <= n:
        j = k // 2
        while j >= 1:
            for i in range(n):
                l = i ^ j
                if l > i:
                    hi, lo = jnp.maximum(a[i], a[l]), jnp.minimum(a[i], a[l])
                    a[i], a[l] = (hi, lo) if (i & k) == 0 else (lo, hi)
            j //= 2
        k *= 2
    return a


def _merge_top(top, grp):
    n = len(top)
    a = [jnp.maximum(top[i], grp[n - 1 - i]) for i in range(n)]
    j = n // 2
    while j >= 1:
        for i in range(n):
            l = i ^ j
            if l > i:
                a[i], a[l] = jnp.maximum(a[i], a[l]), jnp.minimum(a[i], a[l])
        j //= 2
    return a


def _top_values(read):
    top = None
    for k0 in range(0, N_KEYS, P_TOPK):
        grp = _sort_desc([read(k) for k in range(k0, k0 + P_TOPK)])
        top = grp if top is None else _merge_top(top, grp)
    return top


def _tie_flags(read, top):
    flags = _tree(jnp.add, [jnp.where(top[i] > top[i + 1], 0.0, 1.0) for i in range(P_TOPK - 1)])
    reach = _tree(jnp.add, [jnp.where(read(k) >= top[P_TOPK - 1], 1.0, 0.0) for k in range(N_KEYS)])
    return flags + jnp.where(reach == float(P_TOPK), 0.0, 1.0)


def _peer_topk_kernel(x_ref, wq_ref, sk_ref, l1_ref, w1_ref, r2_ref, e2_ref,
                      s_scr, sc_scr, r1_scr, r2_scr, top_scr):
    ph = P_HEADS
    n_tiles = s_scr.shape[0]
    xb = x_ref[...].astype(BF16)
    qt = lax.dot_general(wq_ref[...], xb, (((1,), (1,)), ((), ())), preferred_element_type=F32)
    for p in range(2):
        for h in range(ph):
            q_hp = qt[p * _HK + h * N_KEYS:p * _HK + (h + 1) * N_KEYS].astype(BF16)
            sc = jnp.dot(sk_ref[h * 2 + p], q_hp, preferred_element_type=F32)
            for lt in range(n_tiles):
                tile = s_scr.at[lt, p]
                tile[pl.ds(h, N_KEYS, stride=ph), :] = sc[:, lt * LANES:(lt + 1) * LANES]

    def tile_body(lt, carry):
        _peer_select_tile(lt, s_scr, sc_scr, r1_scr, r2_scr, top_scr,
                          l1_ref, w1_ref, r2_ref, e2_ref)
        return carry

    lax.fori_loop(0, n_tiles, tile_body, 0)


def _peer_select_tile(lt, s_scr, sc_scr, r1_scr, r2_scr, top_scr, l1_ref, w1_ref, r2_ref, e2_ref):
    ph = P_HEADS

    def key_rows(k):
        return slice(k * ph, (k + 1) * ph)

    def read(p):
        return lambda k: s_scr[lt, p, key_rows(k), :]

    a = _top_values(read(0))
    b = _top_values(read(1))
    undecided = jnp.max(_tie_flags(read(0), a) + _tie_flags(read(1), b))
    cand = {ij: a[ij[0]] + b[ij[1]] for ij in _CANDIDATES}
    pos = {ij: float(ij[0] * P_TOPK + ij[1]) for ij in _CANDIDATES}
    work = dict(cand)
    tau = tau_pos = None
    for r in range(P_TOPK):
        tau = _tree(jnp.maximum, [work[ij] for ij in _CANDIDATES])
        tau_pos = _tree(jnp.minimum,
                        [jnp.where(work[ij] == tau, pos[ij], _BIG_POS) for ij in _CANDIDATES])
        if r < P_TOPK - 1:
            for ij in _CANDIDATES:
                work[ij] = jnp.where(tau_pos == pos[ij], -jnp.inf, work[ij])
    ea = [jnp.exp(a[i] - a[0]) for i in range(P_TOPK)]
    eb = [jnp.exp(b[j] - b[0]) for j in range(P_TOPK)]
    count = [jnp.zeros((ph, LANES), F32) for _ in range(P_TOPK)]
    z = jnp.zeros((ph, LANES), F32)
    for ij in _CANDIDATES:
        i, j = ij
        chosen = jnp.where(cand[ij] > tau, 1.0,
                           jnp.where(cand[ij] == tau,
                                     jnp.where(tau_pos >= pos[ij], 1.0, 0.0), 0.0))
        count[i] = count[i] + chosen
        z = z + chosen * (ea[i] * eb[j])
    inv_z = 1.0 / z

    @pl.when(undecided == 0.0)
    def _():
        for k in range(N_KEYS):
            rows = key_rows(k)
            s1 = s_scr[lt, 0, rows, :]
            reach = jnp.zeros((ph, LANES), F32)
            for i in range(P_TOPK):
                reach = jnp.where(s1 == a[i], count[i], reach)
            l1_ref[lt, rows, :] = reach
            s2 = s_scr[lt, 1, rows, :]
            r2_scr[rows, :] = _tree(jnp.add, [jnp.where(b[j] > s2, 1.0, 0.0)
                                              for j in range(P_TOPK)])

    @pl.when(undecided != 0.0)
    def _():
        sc_scr[...] = s_scr[lt, 0]
        _extract_top(sc_scr, r1_scr, top_scr)
        sc_scr[...] = s_scr[lt, 1]
        _extract_top(sc_scr, r2_scr, top_scr)
        for k in range(N_KEYS):
            rows = key_rows(k)
            r1 = r1_scr[rows, :]
            reach = jnp.zeros((ph, LANES), F32)
            for i in range(P_TOPK):
                reach = jnp.where(r1 == float(i), count[i], reach)
            l1_ref[lt, rows, :] = reach

    for k in range(N_KEYS):
        rows = key_rows(k)
        w1_ref[lt, rows, :] = jnp.exp(s_scr[lt, 0, rows, :] - a[0]) * inv_z
        sc_scr[rows, :] = jnp.exp(s_scr[lt, 1, rows, :] - b[0])
    for h in range(ph):
        r2_ref[lt, h] = _pack_bf16(r2_scr[pl.ds(h, N_KEYS, stride=ph), :])
        e2_ref[lt, h] = _pack_bf16(sc_scr[pl.ds(h, N_KEYS, stride=ph), :])


def _peer_topk(x, wq_t, subkeys):
    t = x.shape[0]
    nt = t // LANES
    tm = PEER_TOPK_TOKENS if t % PEER_TOPK_TOKENS == 0 else LANES
    n_tiles = tm // LANES
    tile = pl.BlockSpec((n_tiles, _HK, LANES), lambda i: (i, 0, 0))
    tile3 = pl.BlockSpec((n_tiles, P_HEADS, N_KEYS // 2, LANES), lambda i: (i, 0, 0, 0))
    flat = jax.ShapeDtypeStruct((nt, _HK, LANES), F32)
    cube = jax.ShapeDtypeStruct((nt, P_HEADS, N_KEYS // 2, LANES), jnp.uint32)
    return pl.pallas_call(
        _peer_topk_kernel,
        grid=(t // tm,),
        in_specs=[pl.BlockSpec((tm, D_MODEL), lambda i: (i, 0)),
                  pl.BlockSpec((2 * _HK, D_MODEL), lambda i: (0, 0)),
                  pl.BlockSpec((2 * P_HEADS, N_KEYS, D_KEY // 2), lambda i: (0, 0, 0))],
        out_specs=[tile, tile, tile3, tile3],
        out_shape=[flat, flat, cube, cube],
        scratch_shapes=[pltpu.VMEM((n_tiles, 2, _HK, LANES), F32),
                        pltpu.VMEM((_HK, LANES), F32),
                        pltpu.VMEM((_HK, LANES), F32),
                        pltpu.VMEM((_HK, LANES), F32),
                        pltpu.VMEM((P_TOPK, P_HEADS, LANES), F32)],
        compiler_params=_params("parallel"),
        name="peer_topk",
    )(x, wq_t, subkeys)


def _replicated_bf16(ref, tile, row):
    rep = ref[tile, pl.ds(row, SUBLANES, stride=0), :]
    return jnp.concatenate([rep, rep], axis=0).astype(BF16)


def _peer_mix_kernel(x_ref, u_ref, vt_ref, l1_ref, w1_ref, r2_ref, e2_ref, g_ref, b_ref,
                     out_ref, xt_scr, acc_scr, act_scr, p_scr):
    e = pl.program_id(1)
    n_groups, _, group = xt_scr.shape

    @pl.when(e == 0)
    def _():
        for gi in range(n_groups):
            xt_scr[gi] = x_ref[gi * group:(gi + 1) * group, :].T.astype(BF16)
        acc_scr[...] = jnp.zeros(acc_scr.shape, F32)

    def project(gi):
        act_scr[gi] = jnp.dot(u_ref[...], xt_scr[gi], preferred_element_type=F32)

    def mix(gi):
        acc_scr[gi] += jnp.dot(vt_ref[...], p_scr[gi], preferred_element_type=F32)

    lookahead = min(2, n_groups)
    for gi in range(lookahead):
        project(gi)
    for gi in range(n_groups):
        for c0 in range(0, group, LANES):
            cols = slice(c0, c0 + LANES)
            ct = (gi * group + c0) // LANES
            for kb in range(0, N_KEYS, PEER_KEY_BLOCK):
                subs = range(kb, kb + PEER_KEY_BLOCK, BF16_ROWS)
                gates = {(jj, k0): jnp.zeros((BF16_ROWS, LANES), BF16)
                         for jj in range(PEER_ROWS_PER_STEP) for k0 in subs}
                for h in range(P_HEADS):
                    rows_h = [jj * P_HEADS + h for jj in range(PEER_ROWS_PER_STEP)]
                    reach = [_replicated_bf16(l1_ref, ct, r) for r in rows_h]
                    weight = [_replicated_bf16(w1_ref, ct, r) for r in rows_h]
                    for k0 in subs:
                        words = slice(k0 // 2, k0 // 2 + SUBLANES)
                        r2 = _unpack_bf16(r2_ref[ct, h, words, :])
                        e2 = _unpack_bf16(e2_ref[ct, h, words, :])
                        for jj in range(PEER_ROWS_PER_STEP):
                            picked = jnp.where(r2 < reach[jj], e2, jnp.zeros_like(e2))
                            gates[jj, k0] = gates[jj, k0] + picked * weight[jj]
                for jj in range(PEER_ROWS_PER_STEP):
                    for k0 in subs:
                        rows = slice(jj * N_KEYS + k0, jj * N_KEYS + k0 + BF16_ROWS)
                        a = act_scr[gi, rows, cols]
                        gelu = 0.5 * a * (1.0 + lax.erf(a * (2.0 ** -0.5)))
                        p_scr[gi, rows, cols] = gelu.astype(BF16) * gates[jj, k0]
        if gi >= 1:
            mix(gi - 1)
        if gi + lookahead < n_groups:
            project(gi + lookahead)
    mix(n_groups - 1)

    @pl.when(e == pl.num_programs(1) - 1)
    def _():
        for gi in range(n_groups):
            rows = slice(gi * group, (gi + 1) * group)
            y = DN_ALPHA * x_ref[rows, :] + acc_scr[gi].T
            out_ref[rows, :] = _layer_norm(y, g_ref[...], b_ref[...])


def _peer_mix(x, u, vt, l1, w1, r2, e2, g, b, tm):
    t = x.shape[0]
    te = PEER_ROWS_PER_STEP * N_KEYS
    tr = PEER_ROWS_PER_STEP * P_HEADS
    nt = tm // LANES
    group = min(tm, PEER_COLUMN_GROUP)
    n_groups = tm // group
    return pl.pallas_call(
        _peer_mix_kernel,
        grid=(t // tm, N_EXPERTS // te),
        in_specs=[pl.BlockSpec((tm, D_MODEL), lambda i, e: (i, 0)),
                  pl.BlockSpec((te, D_MODEL), lambda i, e: (e, 0)),
                  pl.BlockSpec((D_MODEL, te), lambda i, e: (0, e)),
                  pl.BlockSpec((nt, tr, LANES), lambda i, e: (i, e, 0)),
                  pl.BlockSpec((nt, tr, LANES), lambda i, e: (i, e, 0)),
                  pl.BlockSpec((nt, P_HEADS, N_KEYS // 2, LANES), lambda i, e: (i, 0, 0, 0)),
                  pl.BlockSpec((nt, P_HEADS, N_KEYS // 2, LANES), lambda i, e: (i, 0, 0, 0)),
                  pl.BlockSpec((1, D_MODEL), lambda i, e: (0, 0)),
                  pl.BlockSpec((1, D_MODEL), lambda i, e: (0, 0))],
        out_specs=pl.BlockSpec((tm, D_MODEL), lambda i, e: (i, 0)),
        out_shape=jax.ShapeDtypeStruct((t, D_MODEL), F32),
        scratch_shapes=[pltpu.VMEM((n_groups, D_MODEL, group), BF16),
                        pltpu.VMEM((n_groups, D_MODEL, group), F32),
                        pltpu.VMEM((n_groups, te, group), F32),
                        pltpu.VMEM((n_groups, te, group), BF16)],
        compiler_params=_params("parallel", "arbitrary"),
        name="peer_mix",
    )(x, u, vt, l1, w1, r2, e2, g, b)


def _peer_layer(x, pw, g, b):
    t = x.shape[0]
    l1, w1, r2, e2 = _peer_topk(x, pw["wq_t"], pw["subkeys"])
    tm = PEER_TOKEN_TILE if t % PEER_TOKEN_TILE == 0 else LANES
    return _peer_mix(x, pw["u"], pw["vt"], l1, w1, r2, e2, g, b, tm)


def _t5_bucket(dist):
    max_exact = N_BUCKETS // 2
    d = jnp.maximum(dist, 0)
    df = jnp.maximum(d, 1).astype(F32)
    large = max_exact + (jnp.log(df / max_exact) / math.log(MAX_DISTANCE / max_exact)
                         * (N_BUCKETS - max_exact)).astype(jnp.int32)
    large = jnp.minimum(large, N_BUCKETS - 1)
    return jnp.where(d < max_exact, d, large)


def _row(a):
    return a.reshape(1, -1)


def _prepare(w):
    n_gate = 2 * A_HEADS
    w_in = w["a_w_in"][0]
    b_in = w["a_b_in"][0]
    prep = {
        "w_in_main": w_in[:, :2 * A_INNER].astype(BF16),
        "b_in_main": _row(b_in[:2 * A_INNER]),
        "w_in_gate": jnp.pad(w_in[:, 2 * A_INNER:], ((0, 0), (0, LANES - n_gate))).astype(BF16),
        "b_in_gate": _row(jnp.pad(b_in[2 * A_INNER:], (0, LANES - n_gate))),
        "conv_w": w["a_conv_w"][0],
        "conv_b": _row(w["a_conv_b"][0]),
        "wq": w["a_w_q"][0].astype(BF16),
        "wk": w["a_w_k"][0].astype(BF16),
        "wv": w["a_w_v"][0].astype(BF16),
        "w_out": w["a_w_out"][0].astype(BF16),
        "w_qkv": jnp.concatenate([w["b_w_q"][0], w["kv_w"]], axis=1).astype(BF16),
        "w_o": w["b_w_o"][0].astype(BF16),
        "sinks": _row(w["b_sinks"][0]),
        "rel_bias": w["rel_bias"],
    }
    peer = []
    half = D_KEY // 2
    for layer in range(DEPTH):
        wq = w["peer_w_q"][layer].reshape(D_MODEL, P_HEADS, 2, half)
        sk = w["peer_subkeys"][layer]
        peer.append({
            "wq_t": wq.transpose(2, 1, 3, 0).reshape(2 * _HK, D_MODEL).astype(BF16),
            "subkeys": sk.reshape(2 * P_HEADS, N_KEYS, half).astype(BF16),
            "u": w["peer_u"][layer].astype(BF16),
            "vt": w["peer_v"][layer].T.astype(BF16),
        })
    prep["peer"] = peer
    return prep


def _zero_bias(n):
    return jnp.zeros((1, n), F32)


def _attention_tables(rel_bias):
    qi = jnp.arange(WINDOW)[:, None]
    kj = jnp.arange(2 * WINDOW)[None, :]
    bucket = _t5_bucket(qi + WINDOW - kj).astype(jnp.int32)
    return _bias_tables(bucket, rel_bias)


def _prompt_trunk(x, w, p, bias):
    batch, length, d = x.shape
    t = batch * length
    xt = x.reshape(t, d)
    proj = _linear(xt, p["w_in_main"], p["b_in_main"], 512, LINEAR_COLUMNS)
    gates = _linear(xt, p["w_in_gate"], p["b_in_gate"], 512, LANES)
    row_tab, col_tab = _gate_tables(gates, batch, length, MLSTM_CHUNK)
    q, k, v = _conv_qkv(proj, p["conv_w"], p["conv_b"], p["wq"], p["wk"], p["wv"], batch, length)
    h, c_new, n_new, m_new = _mlstm_prompt(q, k, v, row_tab, col_tab, batch, length, MLSTM_CHUNK)
    x1 = _outproj_ln(h, proj, p["w_out"], xt, _row(w["ln_mix_g"][0]), _row(w["ln_mix_b"][0]), 256)
    x2 = _peer_layer(x1, p["peer"][0], _row(w["ln_ffn_g"][0]), _row(w["ln_ffn_b"][0]))
    qkv = _linear(x2, p["w_qkv"], _zero_bias(p["w_qkv"].shape[1]), 512, p["w_qkv"].shape[1])
    o = _swa_prompt(qkv, bias, p["sinks"], batch, length)
    x3 = _outproj_ln(o, None, p["w_o"], x2, _row(w["ln_mix_g"][1]), _row(w["ln_mix_b"][1]), 256)
    x4 = _peer_layer(x3, p["peer"][1], _row(w["ln_ffn_g"][1]), _row(w["ln_ffn_b"][1]))

    kvw = B_KV_HEADS * B_HEAD_DIM
    qkv3 = qkv.reshape(batch, length, -1)
    k_win = qkv3[:, -WINDOW:, D_MODEL:D_MODEL + kvw].reshape(batch, WINDOW, B_KV_HEADS, B_HEAD_DIM)
    v_win = qkv3[:, -WINDOW:, D_MODEL + kvw:].reshape(batch, WINDOW, B_KV_HEADS, B_HEAD_DIM)
    conv = proj.reshape(batch, length, -1)[:, -(A_CONV_W - 1):, :A_INNER]
    return (x4.reshape(batch, length, d),
            c_new[None],
            n_new.reshape(1, batch, A_HEADS, A_HEAD_DIM),
            m_new[:, :, 0, 0][None],
            conv[None], k_win, v_win)


def _sample_trunk(x, conv0, c0, n0, m0, k_buf, v_buf, w, p, bias):
    batch, length, d = x.shape
    xt = x.reshape(batch, d)
    proj = _linear(xt, p["w_in_main"], p["b_in_main"], batch, 512)
    gates = _linear(xt, p["w_in_gate"], p["b_in_gate"], batch, LANES)
    buf = conv0[0]
    q, k, v, qt, kt = _sample_conv_qkv(proj, buf.reshape(batch, -1), p["conv_w"], p["conv_b"],
                                       p["wq"], p["wk"], p["wv"])
    h, c_new, n_new, m_new = _sample_mlstm(q, k, v, qt, kt, gates, m0[0], n0[0], c0[0])
    x1 = _outproj_ln(h.reshape(batch, A_INNER), proj, p["w_out"], xt,
                     _row(w["ln_mix_g"][0]), _row(w["ln_mix_b"][0]), batch)
    x2 = _peer_layer(x1, p["peer"][0], _row(w["ln_ffn_g"][0]), _row(w["ln_ffn_b"][0]))
    qkv = _linear(x2, p["w_qkv"], _zero_bias(p["w_qkv"].shape[1]), batch, 512)
    kvw = B_KV_HEADS * B_HEAD_DIM
    o = _swa_decode(qkv, k_buf.reshape(batch, WINDOW, kvw), v_buf.reshape(batch, WINDOW, kvw),
                    bias[:, 0, :], p["sinks"].reshape(B_HEADS, 1))
    x3 = _outproj_ln(o, None, p["w_o"], x2, _row(w["ln_mix_g"][1]), _row(w["ln_mix_b"][1]), batch)
    x4 = _peer_layer(x3, p["peer"][1], _row(w["ln_ffn_g"][1]), _row(w["ln_ffn_b"][1]))

    k_new = qkv[:, D_MODEL:D_MODEL + kvw].reshape(batch, 1, B_KV_HEADS, B_HEAD_DIM)
    v_new = qkv[:, D_MODEL + kvw:].reshape(batch, 1, B_KV_HEADS, B_HEAD_DIM)
    k_win = jnp.concatenate([k_buf[:, 1:], k_new], axis=1)
    v_win = jnp.concatenate([v_buf[:, 1:], v_new], axis=1)
    conv = jnp.concatenate([buf[:, 1:], proj[:, None, :A_INNER]], axis=1)
    return (x4.reshape(batch, length, d),
            c_new[None],
            n_new.reshape(1, batch, A_HEADS, A_HEAD_DIM),
            m_new[:, :, 0, 0][None],
            conv[None], k_win, v_win)


def kernel(x_prompt, x_sample, state_mlstm_C, state_mlstm_n, state_mlstm_m, state_mlstm_conv,
           cache_k_win, cache_v_win, a_w_in, a_b_in, a_conv_w, a_conv_b, a_w_q, a_w_k, a_w_v,
           a_w_out, kv_w, b_w_q, b_w_o, b_sinks, rel_bias, ln_mix_g, ln_mix_b, ln_ffn_g,
           ln_ffn_b, peer_w_q, peer_subkeys, peer_u, peer_v):
    w = {"a_w_in": a_w_in, "a_b_in": a_b_in, "a_conv_w": a_conv_w, "a_conv_b": a_conv_b,
         "a_w_q": a_w_q, "a_w_k": a_w_k, "a_w_v": a_w_v, "a_w_out": a_w_out, "kv_w": kv_w,
         "b_w_q": b_w_q, "b_w_o": b_w_o, "b_sinks": b_sinks, "rel_bias": rel_bias,
         "ln_mix_g": ln_mix_g, "ln_mix_b": ln_mix_b, "ln_ffn_g": ln_ffn_g, "ln_ffn_b": ln_ffn_b,
         "peer_w_q": peer_w_q, "peer_subkeys": peer_subkeys, "peer_u": peer_u, "peer_v": peer_v}
    p = _prepare(w)
    bias = _attention_tables(rel_bias)
    prompt = _prompt_trunk(x_prompt, w, p, bias)
    sample = _sample_trunk(x_sample, state_mlstm_conv, state_mlstm_C, state_mlstm_n,
                           state_mlstm_m, cache_k_win, cache_v_win, w, p, bias)
    return (prompt[0], sample[0]) + prompt[1:] + sample[1:]
```

```python
import functools
import math

import jax
import jax.numpy as jnp
from jax import lax
from jax.experimental import pallas as pl
from jax.experimental.pallas import tpu as pltpu

D_MODEL = 1024
DEPTH = 2
A_HEADS = 4
A_INNER = 2 * D_MODEL
A_HEAD_DIM = A_INNER // A_HEADS
A_CONV_W = 4
B_HEADS = 16
B_KV_HEADS = 4
B_GROUP = B_HEADS // B_KV_HEADS
B_HEAD_DIM = D_MODEL // B_HEADS
WINDOW = 128
N_BUCKETS = 32
MAX_DISTANCE = 128
P_HEADS = 8
N_KEYS = 128
N_EXPERTS = N_KEYS * N_KEYS
D_KEY = 256
P_TOPK = 16
DN_ALPHA = (2 * DEPTH) ** 0.25
LN_EPS = 1e-5
NEG_INF = -1e30

LANES = 128
SUBLANES = 8
BF16_ROWS = 2 * SUBLANES
VMEM_LIMIT_BYTES = 48 * 1024 * 1024

LINEAR_COLUMNS = 2048
MLSTM_CHUNK = 256
PEER_TOPK_TOKENS = 256
PEER_TOKEN_TILE = 1024
PEER_ROWS_PER_STEP = 8
PEER_COLUMN_GROUP = 256
PEER_KEY_BLOCK = 16
NOT_SELECTED_RANK = float(P_TOPK)

F32 = jnp.float32
BF16 = jnp.bfloat16


def _params(*sem):
    return pltpu.CompilerParams(dimension_semantics=sem, vmem_limit_bytes=VMEM_LIMIT_BYTES)


def _tree(op, vals):
    vals = list(vals)
    while len(vals) > 1:
        nxt = [op(vals[i], vals[i + 1]) for i in range(0, len(vals) - 1, 2)]
        if len(vals) % 2:
            nxt.append(vals[-1])
        vals = nxt
    return vals[0]


def _layer_norm(y, g, b):
    mu = jnp.mean(y, axis=-1, keepdims=True)
    yc = y - mu
    var = jnp.mean(yc * yc, axis=-1, keepdims=True)
    return yc * lax.rsqrt(var + LN_EPS) * g + b


def _linear_kernel(x_ref, w_ref, b_ref, o_ref):
    x = x_ref[...].astype(BF16)
    o_ref[...] = jnp.dot(x, w_ref[...], preferred_element_type=F32) + b_ref[...]


def _linear(x, w, b, tm, tn):
    t, k = x.shape
    n = w.shape[1]
    return pl.pallas_call(
        _linear_kernel,
        grid=(t // tm, n // tn),
        in_specs=[pl.BlockSpec((tm, k), lambda i, j: (i, 0)),
                  pl.BlockSpec((k, tn), lambda i, j: (0, j)),
                  pl.BlockSpec((1, tn), lambda i, j: (0, j))],
        out_specs=pl.BlockSpec((tm, tn), lambda i, j: (i, j)),
        out_shape=jax.ShapeDtypeStruct((t, n), F32),
        compiler_params=_params("parallel", "arbitrary"),
        name="linear",
    )(x, w, b)


def _outproj_ln_kernel(gated, *refs):
    if gated:
        a_ref, o_ref, w_ref, res_ref, g_ref, b_ref, out_ref = refs
        act = jax.nn.sigmoid(o_ref[...]) * a_ref[...]
    else:
        a_ref, w_ref, res_ref, g_ref, b_ref, out_ref = refs
        act = a_ref[...]
    sub = jnp.dot(act.astype(BF16), w_ref[...], preferred_element_type=F32)
    out_ref[...] = _layer_norm(DN_ALPHA * res_ref[...] + sub, g_ref[...], b_ref[...])


def _outproj_ln(act, gate_src, w, res, g, b, tm):
    t, k = act.shape
    d = w.shape[1]
    gated = gate_src is not None
    in_specs = [pl.BlockSpec((tm, k), lambda i: (i, 0))]
    args = [act]
    if gated:
        in_specs.append(pl.BlockSpec((tm, k), lambda i: (i, 1)))
        args.append(gate_src)
    in_specs += [pl.BlockSpec((k, d), lambda i: (0, 0)),
                 pl.BlockSpec((tm, d), lambda i: (i, 0)),
                 pl.BlockSpec((1, d), lambda i: (0, 0)),
                 pl.BlockSpec((1, d), lambda i: (0, 0))]
    args += [w, res, g, b]
    return pl.pallas_call(
        functools.partial(_outproj_ln_kernel, gated),
        grid=(t // tm,),
        in_specs=in_specs,
        out_specs=pl.BlockSpec((tm, d), lambda i: (i, 0)),
        out_shape=jax.ShapeDtypeStruct((t, d), F32),
        compiler_params=_params("parallel"),
        name="outproj_ln",
    )(*args)


def _log_sigmoid(x):
    return jnp.minimum(x, 0.0) - jnp.log1p(jnp.exp(-jnp.abs(x)))


def _gates_kernel(chunk, g_ref, row_ref, col_ref):
    length = g_ref.shape[0]
    gt = g_ref[...].T
    top = gt[0:SUBLANES]
    row = lax.broadcasted_iota(jnp.int32, top.shape, 0)
    pos = lax.broadcasted_iota(jnp.int32, top.shape, 1) % chunk
    is_f = row >= A_HEADS
    x = jnp.where(is_f, _log_sigmoid(top), 0.0)
    shift = 1
    while shift < chunk:
        x = x + jnp.where(pos >= shift, pltpu.roll(x, shift, 1), 0.0)
        shift *= 2
    table = jnp.where(is_f, x, top)
    row_ref[...] = table
    padded = jnp.concatenate([table, jnp.zeros((LANES - SUBLANES, length), F32)], axis=0)
    col_ref[...] = padded.T


def _gate_tables(gates, batch, length, chunk):
    return pl.pallas_call(
        functools.partial(_gates_kernel, chunk),
        grid=(batch,),
        in_specs=[pl.BlockSpec((length, LANES), lambda b: (b, 0))],
        out_specs=[pl.BlockSpec((None, SUBLANES, length), lambda b: (b, 0, 0)),
                   pl.BlockSpec((length, LANES), lambda b: (b, 0))],
        out_shape=[jax.ShapeDtypeStruct((batch, SUBLANES, length), F32),
                   jax.ShapeDtypeStruct((batch * length, LANES), F32)],
        compiler_params=_params("parallel"),
        name="gate_tables",
    )(gates)


CONV_ROW_CHUNK = 512
CONV_PAD = SUBLANES


def _conv_qkv_kernel(xm_ref, cw_ref, cb_ref, wq_ref, wk_ref, wv_ref,
                     q_ref, k_ref, v_ref, pad_ref):
    length = xm_ref.shape[0]
    pad_ref[0:CONV_PAD, :] = jnp.zeros((CONV_PAD, A_HEAD_DIM), F32)
    pad_ref[CONV_PAD:CONV_PAD + length, :] = xm_ref[...]
    first = CONV_PAD - (A_CONV_W - 1)
    for c0 in range(0, length, CONV_ROW_CHUNK):
        acc = cb_ref[...]
        for w in range(A_CONV_W):
            acc = acc + pad_ref[c0 + first + w:c0 + first + w + CONV_ROW_CHUNK, :] * cw_ref[w:w + 1, :]
        xc = (acc * jax.nn.sigmoid(acc)).astype(BF16)
        xm = xm_ref[c0:c0 + CONV_ROW_CHUNK, :].astype(BF16)
        rows = slice(c0, c0 + CONV_ROW_CHUNK)
        q_ref[rows, :] = jnp.dot(xc, wq_ref[...], preferred_element_type=F32).astype(BF16)
        k = jnp.dot(xc, wk_ref[...], preferred_element_type=F32) * (A_HEAD_DIM ** -0.5)
        k_ref[rows, :] = k.astype(BF16)
        v_ref[rows, :] = jnp.dot(xm, wv_ref[...], preferred_element_type=F32).astype(BF16)


def _conv_qkv(proj, cw, cb, wq, wk, wv, batch, length):
    t = batch * length
    hd = A_HEAD_DIM
    tok = pl.BlockSpec((length, hd), lambda b, h: (b, h))
    wspec = pl.BlockSpec((None, hd, hd), lambda b, h: (h, 0, 0))
    out = jax.ShapeDtypeStruct((t, A_INNER), BF16)
    return pl.pallas_call(
        _conv_qkv_kernel,
        grid=(batch, A_HEADS),
        in_specs=[tok,
                  pl.BlockSpec((A_CONV_W, hd), lambda b, h: (0, h)),
                  pl.BlockSpec((1, hd), lambda b, h: (0, h)),
                  wspec, wspec, wspec],
        out_specs=[tok, tok, tok],
        out_shape=[out, out, out],
        scratch_shapes=[pltpu.VMEM((length + CONV_PAD, hd), F32)],
        compiler_params=_params("parallel", "arbitrary"),
        name="conv_qkv",
    )(proj, cw, cb, wq, wk, wv)


def _mlstm_kernel(q_ref, k_ref, v_ref, row_ref, col_ref,
                  h_ref, c_out_ref, n_out_ref, m_out_ref,
                  c_scr, n_scr, m_scr):
    ci = pl.program_id(1)
    chunk = q_ref.shape[0]
    hd = A_HEAD_DIM

    @pl.when(ci == 0)
    def _():
        c_scr[...] = jnp.zeros(c_scr.shape, F32)
        n_scr[...] = jnp.zeros(n_scr.shape, F32)
        m_scr[...] = jnp.zeros(m_scr.shape, F32)

    t_idx = lax.broadcasted_iota(jnp.int32, (chunk, chunk), 0)
    s_idx = lax.broadcasted_iota(jnp.int32, (chunk, chunk), 1)
    causal = s_idx <= t_idx
    for h in range(A_HEADS):
        cols = slice(h * hd, (h + 1) * hd)
        qh, kh, vh = q_ref[:, cols], k_ref[:, cols], v_ref[:, cols]
        i_col = col_ref[:, h:h + 1]
        f_col = col_ref[:, A_HEADS + h:A_HEADS + h + 1]
        i_row = row_ref[h:h + 1, :]
        f_row = row_ref[A_HEADS + h:A_HEADS + h + 1, :]
        m_prev = m_scr[h][:, 0:1]
        d = jnp.where(causal, f_col - f_row + i_row, NEG_INF)
        b_inter = f_col + m_prev
        m_t = jnp.maximum(b_inter, jnp.max(d, axis=1, keepdims=True))
        qk = lax.dot_general(qh, kh, (((1,), (1,)), ((), ())), preferred_element_type=F32)
        s = qk * jnp.exp(d - m_t)
        w_inter = jnp.exp(b_inter - m_t)
        q_c = jnp.dot(qh, c_scr[h].astype(BF16), preferred_element_type=F32)
        num = jnp.dot(s.astype(BF16), vh, preferred_element_type=F32) + w_inter * q_c
        q_n = jnp.sum(qh.astype(F32) * n_scr[h], axis=1, keepdims=True)
        den = jnp.sum(s, axis=1, keepdims=True) + w_inter * q_n
        h_ref[:, cols] = num / jnp.maximum(jnp.abs(den), jnp.exp(-m_t))
        f_last = f_col[chunk - 1:chunk, :]
        g = f_last - f_col + i_col
        m_new = jnp.maximum(f_last + m_prev, jnp.max(g, axis=0, keepdims=True))
        decay = jnp.exp(f_last + m_prev - m_new)
        wk = jnp.exp(g - m_new) * kh.astype(F32)
        kv = lax.dot_general(wk.astype(BF16), vh, (((0,), (0,)), ((), ())),
                             preferred_element_type=F32)
        c_scr[h] = decay * c_scr[h] + kv
        n_scr[h] = decay * n_scr[h] + jnp.sum(wk, axis=0, keepdims=True)
        m_scr[h] = jnp.broadcast_to(m_new, (1, LANES))

    @pl.when(ci == pl.num_programs(1) - 1)
    def _():
        c_out_ref[...] = c_scr[...]
        n_out_ref[...] = n_scr[...]
        m_out_ref[...] = m_scr[...]


def _mlstm_prompt(q, k, v, row_tab, col_tab, batch, length, chunk):
    nc = length // chunk
    t = batch * length
    tok = pl.BlockSpec((chunk, A_INNER), lambda b, c: (b * nc + c, 0))
    return pl.pallas_call(
        _mlstm_kernel,
        grid=(batch, nc),
        in_specs=[tok, tok, tok,
                  pl.BlockSpec((None, SUBLANES, chunk), lambda b, c: (b, 0, c)),
                  pl.BlockSpec((chunk, LANES), lambda b, c: (b * nc + c, 0))],
        out_specs=[tok,
                   pl.BlockSpec((None, A_HEADS, A_HEAD_DIM, A_HEAD_DIM), lambda b, c: (b, 0, 0, 0)),
                   pl.BlockSpec((None, A_HEADS, 1, A_HEAD_DIM), lambda b, c: (b, 0, 0, 0)),
                   pl.BlockSpec((None, A_HEADS, 1, LANES), lambda b, c: (b, 0, 0, 0))],
        out_shape=[jax.ShapeDtypeStruct((t, A_INNER), F32),
                   jax.ShapeDtypeStruct((batch, A_HEADS, A_HEAD_DIM, A_HEAD_DIM), F32),
                   jax.ShapeDtypeStruct((batch, A_HEADS, 1, A_HEAD_DIM), F32),
                   jax.ShapeDtypeStruct((batch, A_HEADS, 1, LANES), F32)],
        scratch_shapes=[pltpu.VMEM((A_HEADS, A_HEAD_DIM, A_HEAD_DIM), F32),
                        pltpu.VMEM((A_HEADS, 1, A_HEAD_DIM), F32),
                        pltpu.VMEM((A_HEADS, 1, LANES), F32)],
        compiler_params=_params("parallel", "arbitrary"),
        name="mlstm_prompt",
    )(q, k, v, row_tab, col_tab)


def _sample_conv_qkv_kernel(proj_ref, buf_ref, cw_ref, cb_ref, wq_ref, wk_ref, wv_ref,
                            q_ref, k_ref, v_ref, qt_ref, kt_ref):
    xm = proj_ref[:, 0:A_INNER]
    acc = cb_ref[...] + xm * cw_ref[A_CONV_W - 1:A_CONV_W, :]
    for w in range(A_CONV_W - 1):
        acc = acc + buf_ref[:, w * A_INNER:(w + 1) * A_INNER] * cw_ref[w:w + 1, :]
    xc = (acc * jax.nn.sigmoid(acc)).astype(BF16)
    xmb = xm.astype(BF16)
    for h in range(A_HEADS):
        cols = slice(h * A_HEAD_DIM, (h + 1) * A_HEAD_DIM)
        q_ref[:, cols] = jnp.dot(xc[:, cols], wq_ref[h], preferred_element_type=F32)
        k_ref[:, cols] = (jnp.dot(xc[:, cols], wk_ref[h], preferred_element_type=F32)
                          * (A_HEAD_DIM ** -0.5))
        v_ref[:, cols] = jnp.dot(xmb[:, cols], wv_ref[h], preferred_element_type=F32)
    qt_ref[...] = q_ref[...].T
    kt_ref[...] = k_ref[...].T


def _sample_conv_qkv(proj, conv_buf_flat, cw, cb, wq, wk, wv):
    b = proj.shape[0]
    row = jax.ShapeDtypeStruct((b, A_INNER), F32)
    col = jax.ShapeDtypeStruct((A_INNER, b), F32)
    return pl.pallas_call(
        _sample_conv_qkv_kernel,
        out_shape=[row, row, row, col, col],
        compiler_params=pltpu.CompilerParams(vmem_limit_bytes=VMEM_LIMIT_BYTES),
        name="sample_conv_qkv",
    )(proj, conv_buf_flat, cw, cb, wq, wk, wv)


def _sample_mlstm_kernel(q_ref, k_ref, v_ref, qt_ref, kt_ref, g_ref, m0_ref, n0_ref, c0_ref,
                         h_ref, c_ref, n_ref, m_ref):
    b = pl.program_id(0)
    hd = A_HEAD_DIM
    is_b = lax.broadcasted_iota(jnp.int32, (hd, LANES), 1) == b
    gates = g_ref[...]
    m0_all = m0_ref[...]
    for h in range(A_HEADS):
        cols = slice(h * hd, (h + 1) * hd)
        q_col = jnp.sum(jnp.where(is_b, qt_ref[cols, :], 0.0), axis=1, keepdims=True)
        k_col = jnp.sum(jnp.where(is_b, kt_ref[cols, :], 0.0), axis=1, keepdims=True)
        q_mat = jnp.broadcast_to(q_col, (hd, hd))
        k_mat = jnp.broadcast_to(k_col, (hd, hd))
        q_row, k_row, v_row = q_ref[:, cols], k_ref[:, cols], v_ref[:, cols]
        log_i = gates[:, h:h + 1]
        log_f = _log_sigmoid(gates[:, A_HEADS + h:A_HEADS + h + 1])
        m0 = m0_all[:, h:h + 1]
        m_t = jnp.maximum(log_f + m0, log_i)
        w_inter = jnp.exp(log_f + m0 - m_t)
        w_new = jnp.exp(log_i - m_t)
        c0 = c0_ref[h]
        n0 = n0_ref[h]
        s = jnp.sum(q_row * k_row, axis=1, keepdims=True) * w_new
        q_c = jnp.sum(q_mat * c0, axis=0, keepdims=True)
        q_n = jnp.sum(q_row * n0, axis=1, keepdims=True)
        num = s * v_row + w_inter * q_c
        den = s + w_inter * q_n
        h_ref[:, cols] = num / jnp.maximum(jnp.abs(den), jnp.exp(-m_t))
        c_ref[h] = w_inter * c0 + (w_new * k_mat) * v_row
        n_ref[h] = w_inter * n0 + w_new * k_row
        m_ref[h] = jnp.broadcast_to(m_t, (1, LANES))


def _sample_mlstm(q, k, v, qt, kt, gates, m0, n0, c0):
    b = q.shape[0]
    hd = A_HEAD_DIM
    row3 = lambda a: a.reshape(b, 1, a.shape[-1])
    rspec = pl.BlockSpec((None, 1, A_INNER), lambda i: (i, 0, 0))
    cspec = pl.BlockSpec((A_INNER, LANES), lambda i: (0, 0))
    nspec = pl.BlockSpec((None, A_HEADS, 1, hd), lambda i: (i, 0, 0, 0))
    mspec = pl.BlockSpec((None, A_HEADS, 1, LANES), lambda i: (i, 0, 0, 0))
    big = pl.BlockSpec((None, A_HEADS, hd, hd), lambda i: (i, 0, 0, 0))
    return pl.pallas_call(
        _sample_mlstm_kernel,
        grid=(b,),
        in_specs=[rspec, rspec, rspec, cspec, cspec,
                  pl.BlockSpec((None, 1, LANES), lambda i: (i, 0, 0)),
                  pl.BlockSpec((None, 1, A_HEADS), lambda i: (i, 0, 0)),
                  nspec, big],
        out_specs=[rspec, big, nspec, mspec],
        out_shape=[jax.ShapeDtypeStruct((b, 1, A_INNER), F32),
                   jax.ShapeDtypeStruct((b, A_HEADS, hd, hd), F32),
                   jax.ShapeDtypeStruct((b, A_HEADS, 1, hd), F32),
                   jax.ShapeDtypeStruct((b, A_HEADS, 1, LANES), F32)],
        compiler_params=_params("parallel"),
        name="sample_mlstm",
    )(row3(q), row3(k), row3(v), qt, kt, row3(gates), row3(m0),
      n0.reshape(b, A_HEADS, 1, hd), c0)


def _bias_kernel(bucket_ref, rel_ref, bias_ref):
    h = pl.program_id(0)
    bucket = bucket_ref[...]
    acc = jnp.zeros(bucket.shape, F32)
    for n in range(N_BUCKETS):
        acc = jnp.where(bucket == n, rel_ref[n, h], acc)
    bias_ref[...] = acc


def _bias_tables(bucket, rel_bias):
    w, w2 = bucket.shape
    return pl.pallas_call(
        _bias_kernel,
        grid=(B_HEADS,),
        in_specs=[pl.BlockSpec((w, w2), lambda h: (0, 0)),
                  pl.BlockSpec(memory_space=pltpu.SMEM)],
        out_specs=pl.BlockSpec((None, w, w2), lambda h: (h, 0, 0)),
        out_shape=jax.ShapeDtypeStruct((B_HEADS, w, w2), F32),
        compiler_params=_params("arbitrary"),
        name="bias_tables",
    )(bucket, rel_bias)


def _swa_prompt_kernel(q_ref, kp_ref, kc_ref, vp_ref, vc_ref, bias_ref, sink_ref, o_ref):
    n = pl.program_id(1)
    w = WINDOW
    q = q_ref[...].astype(BF16)
    kk = jnp.concatenate([kp_ref[...], kc_ref[...]], axis=0).astype(BF16)
    vv = jnp.concatenate([vp_ref[...], vc_ref[...]], axis=0).astype(BF16)
    qi = lax.broadcasted_iota(jnp.int32, (w, 2 * w), 0)
    kj = lax.broadcasted_iota(jnp.int32, (w, 2 * w), 1)
    dist = qi + w - kj
    valid = (dist >= 0) & (dist < w) & ((kj >= w) | (n > 0))
    for h in range(B_HEADS):
        kvh = h // B_GROUP
        qh = q[:, h * B_HEAD_DIM:(h + 1) * B_HEAD_DIM]
        kh = kk[:, kvh * B_HEAD_DIM:(kvh + 1) * B_HEAD_DIM]
        vh = vv[:, kvh * B_HEAD_DIM:(kvh + 1) * B_HEAD_DIM]
        s = lax.dot_general(qh, kh, (((1,), (1,)), ((), ())), preferred_element_type=F32)
        s = jnp.where(valid, s * (B_HEAD_DIM ** -0.5) + bias_ref[h], NEG_INF)
        sink = sink_ref[0, h]
        m = jnp.maximum(jnp.max(s, axis=1, keepdims=True), sink)
        p = jnp.exp(s - m)
        den = jnp.sum(p, axis=1, keepdims=True) + jnp.exp(sink - m)
        p = (p / den).astype(BF16)
        o_ref[:, h * B_HEAD_DIM:(h + 1) * B_HEAD_DIM] = jnp.dot(p, vh, preferred_element_type=F32)


def _swa_prompt(qkv, bias, sinks, batch, length):
    nb = length // WINDOW
    kvw = B_KV_HEADS * B_HEAD_DIM
    kcol = D_MODEL // kvw
    cur = lambda b, n: b * nb + n
    prev = lambda b, n: b * nb + jnp.maximum(n - 1, 0)
    return pl.pallas_call(
        _swa_prompt_kernel,
        grid=(batch, nb),
        in_specs=[pl.BlockSpec((WINDOW, D_MODEL), lambda b, n: (cur(b, n), 0)),
                  pl.BlockSpec((WINDOW, kvw), lambda b, n: (prev(b, n), kcol)),
                  pl.BlockSpec((WINDOW, kvw), lambda b, n: (cur(b, n), kcol)),
                  pl.BlockSpec((WINDOW, kvw), lambda b, n: (prev(b, n), kcol + 1)),
                  pl.BlockSpec((WINDOW, kvw), lambda b, n: (cur(b, n), kcol + 1)),
                  pl.BlockSpec((B_HEADS, WINDOW, 2 * WINDOW), lambda b, n: (0, 0, 0)),
                  pl.BlockSpec(memory_space=pltpu.SMEM)],
        out_specs=pl.BlockSpec((WINDOW, D_MODEL), lambda b, n: (cur(b, n), 0)),
        out_shape=jax.ShapeDtypeStruct((batch * length, D_MODEL), F32),
        compiler_params=_params("parallel", "arbitrary"),
        name="swa_prompt",
    )(qkv, qkv, qkv, qkv, qkv, bias, sinks)


DECODE_BATCH_BLOCK = 8


def _swa_decode_kernel(qkv_ref, ck_ref, cv_ref, bias_ref, sink_ref, o_ref):
    w = WINDOW
    hd = B_HEAD_DIM
    kvw = B_KV_HEADS * hd
    scale = hd ** -0.5
    pos = lax.broadcasted_iota(jnp.int32, (B_GROUP, w), 1)
    for bi in range(DECODE_BATCH_BLOCK):
        row = qkv_ref[bi:bi + 1, :]
        for kvh in range(B_KV_HEADS):
            heads = slice(kvh * B_GROUP, (kvh + 1) * B_GROUP)
            kv_cols = slice(kvh * hd, (kvh + 1) * hd)
            q = jnp.concatenate([row[:, h * hd:(h + 1) * hd]
                                 for h in range(kvh * B_GROUP, (kvh + 1) * B_GROUP)], axis=0)
            k_new = row[:, D_MODEL + kvh * hd:D_MODEL + (kvh + 1) * hd]
            v_new = row[:, D_MODEL + kvw + kvh * hd:D_MODEL + kvw + (kvh + 1) * hd]
            kc = ck_ref[bi, :, kv_cols].astype(BF16)
            vc = cv_ref[bi, :, kv_cols].astype(BF16)
            s_c = lax.dot_general(q.astype(BF16), kc, (((1,), (1,)), ((), ())),
                                  preferred_element_type=F32)
            s_c = jnp.where(pos >= 1, s_c * scale + bias_ref[heads, 0:w], NEG_INF)
            s_n = jnp.sum(q * k_new, axis=1, keepdims=True) * scale + bias_ref[heads, w:w + 1]
            sink = sink_ref[heads, :]
            m = jnp.maximum(jnp.maximum(jnp.max(s_c, axis=1, keepdims=True), s_n), sink)
            p_c = jnp.exp(s_c - m)
            p_n = jnp.exp(s_n - m)
            den = jnp.sum(p_c, axis=1, keepdims=True) + p_n + jnp.exp(sink - m)
            o = (jnp.dot(p_c.astype(BF16), vc, preferred_element_type=F32) + p_n * v_new) / den
            for g in range(B_GROUP):
                h = kvh * B_GROUP + g
                o_ref[bi:bi + 1, h * hd:(h + 1) * hd] = o[g:g + 1, :]


def _swa_decode(qkv, cache_k, cache_v, bias_row, sinks_col):
    b = qkv.shape[0]
    kvw = B_KV_HEADS * B_HEAD_DIM
    bb = DECODE_BATCH_BLOCK
    return pl.pallas_call(
        _swa_decode_kernel,
        grid=(b // bb,),
        in_specs=[pl.BlockSpec((bb, qkv.shape[1]), lambda i: (i, 0)),
                  pl.BlockSpec((bb, WINDOW, kvw), lambda i: (i, 0, 0)),
                  pl.BlockSpec((bb, WINDOW, kvw), lambda i: (i, 0, 0)),
                  pl.BlockSpec((B_HEADS, 2 * WINDOW), lambda i: (0, 0)),
                  pl.BlockSpec((B_HEADS, 1), lambda i: (0, 0))],
        out_specs=pl.BlockSpec((bb, D_MODEL), lambda i: (i, 0)),
        out_shape=jax.ShapeDtypeStruct((b, D_MODEL), F32),
        compiler_params=_params("parallel"),
        name="swa_decode",
    )(qkv, cache_k, cache_v, bias_row, sinks_col)


_CANDIDATES = [(i, j) for i in range(P_TOPK) for j in range(P_TOPK)
               if (i + 1) * (j + 1) <= P_TOPK]
_BIG_POS = float(4 * P_TOPK * P_TOPK)
_HK = P_HEADS * N_KEYS


def _pack_bf16(x):
    return pltpu.bitcast(x.astype(BF16), jnp.uint32)


def _unpack_bf16(words):
    return pltpu.bitcast(words, BF16)


def _twice_bf16(x):
    high = pltpu.bitcast(x.astype(BF16).astype(F32), jnp.uint32)
    return high | (high >> 16)


def _extract_top(s_ref, r_ref, top_ref):
    ph = P_HEADS
    for k in range(N_KEYS):
        r_ref[k * ph:(k + 1) * ph, :] = jnp.full((ph, LANES), NOT_SELECTED_RANK, F32)

    def body(r, carry):
        best = _tree(jnp.maximum, [s_ref[k * ph:(k + 1) * ph, :] for k in range(N_KEYS)])
        first = _tree(jnp.minimum,
                      [jnp.where(s_ref[k * ph:(k + 1) * ph, :] == best, float(k), float(N_KEYS))
                       for k in range(N_KEYS)])
        rank = lax.convert_element_type(r, F32)
        for k in range(N_KEYS):
            rows = slice(k * ph, (k + 1) * ph)
            hit = first == float(k)
            s_ref[rows, :] = jnp.where(hit, -jnp.inf, s_ref[rows, :])
            r_ref[rows, :] = jnp.where(hit, rank, r_ref[rows, :])
        top_ref[r] = best
        return carry

    lax.fori_loop(0, P_TOPK, body, 0)


def _sort_desc(vals):
    a = list(vals)
    n = len(a)
    k = 2
    while k <= n:
        j = k // 2
        while j >= 1:
            for i in range(n):
                l = i ^ j
                if l > i:
                    hi, lo = jnp.maximum(a[i], a[l]), jnp.minimum(a[i], a[l])
                    a[i], a[l] = (hi, lo) if (i & k) == 0 else (lo, hi)
            j //= 2
        k *= 2
    return a


def _merge_top(top, grp):
    n = len(top)
    a = [jnp.maximum(top[i], grp[n - 1 - i]) for i in range(n)]
    j = n // 2
    while j >= 1:
        for i in range(n):
            l = i ^ j
            if l > i:
                a[i], a[l] = jnp.maximum(a[i], a[l]), jnp.minimum(a[i], a[l])
        j //= 2
    return a


def _top_values(read):
    top = None
    for k0 in range(0, N_KEYS, P_TOPK):
        grp = _sort_desc([read(k) for k in range(k0, k0 + P_TOPK)])
        top = grp if top is None else _merge_top(top, grp)
    return top


def _tie_flags(read, top):
    flags = _tree(jnp.add, [jnp.where(top[i] > top[i + 1], 0.0, 1.0) for i in range(P_TOPK - 1)])
    reach = _tree(jnp.add, [jnp.where(read(k) >= top[P_TOPK - 1], 1.0, 0.0) for k in range(N_KEYS)])
    return flags + jnp.where(reach == float(P_TOPK), 0.0, 1.0)


def _peer_topk_kernel(x_ref, wq_ref, sk_ref, l1_ref, w1_ref, r2_ref, e2_ref,
                      s_scr, sc_scr, r1_scr, r2_scr, top_scr):
    ph = P_HEADS
    n_tiles = s_scr.shape[0]
    xb = x_ref[...].astype(BF16)
    qt = lax.dot_general(wq_ref[...], xb, (((1,), (1,)), ((), ())), preferred_element_type=F32)
    for p in range(2):
        for h in range(ph):
            q_hp = qt[p * _HK + h * N_KEYS:p * _HK + (h + 1) * N_KEYS].astype(BF16)
            sc = jnp.dot(sk_ref[h * 2 + p], q_hp, preferred_element_type=F32)
            for lt in range(n_tiles):
                tile = s_scr.at[lt, p]
                tile[pl.ds(h, N_KEYS, stride=ph), :] = sc[:, lt * LANES:(lt + 1) * LANES]

    def tile_body(lt, carry):
        _peer_select_tile(lt, s_scr, sc_scr, r1_scr, r2_scr, top_scr,
                          l1_ref, w1_ref, r2_ref, e2_ref)
        return carry

    lax.fori_loop(0, n_tiles, tile_body, 0)


def _peer_select_tile(lt, s_scr, sc_scr, r1_scr, r2_scr, top_scr, l1_ref, w1_ref, r2_ref, e2_ref):
    ph = P_HEADS

    def key_rows(k):
        return slice(k * ph, (k + 1) * ph)

    def read(p):
        return lambda k: s_scr[lt, p, key_rows(k), :]

    a = _top_values(read(0))
    b = _top_values(read(1))
    undecided = jnp.max(_tie_flags(read(0), a) + _tie_flags(read(1), b))
    cand = {ij: a[ij[0]] + b[ij[1]] for ij in _CANDIDATES}
    pos = {ij: float(ij[0] * P_TOPK + ij[1]) for ij in _CANDIDATES}
    work = dict(cand)
    tau = tau_pos = None
    for r in range(P_TOPK):
        tau = _tree(jnp.maximum, [work[ij] for ij in _CANDIDATES])
        tau_pos = _tree(jnp.minimum,
                        [jnp.where(work[ij] == tau, pos[ij], _BIG_POS) for ij in _CANDIDATES])
        if r < P_TOPK - 1:
            for ij in _CANDIDATES:
                work[ij] = jnp.where(tau_pos == pos[ij], -jnp.inf, work[ij])
    ea = [jnp.exp(a[i] - a[0]) for i in range(P_TOPK)]
    eb = [jnp.exp(b[j] - b[0]) for j in range(P_TOPK)]
    count = [jnp.zeros((ph, LANES), F32) for _ in range(P_TOPK)]
    z = jnp.zeros((ph, LANES), F32)
    for ij in _CANDIDATES:
        i, j = ij
        chosen = jnp.where(cand[ij] > tau, 1.0,
                           jnp.where(cand[ij] == tau,
                                     jnp.where(tau_pos >= pos[ij], 1.0, 0.0), 0.0))
        count[i] = count[i] + chosen
        z = z + chosen * (ea[i] * eb[j])
    inv_z = 1.0 / z

    @pl.when(undecided == 0.0)
    def _():
        for k in range(N_KEYS):
            rows = key_rows(k)
            s1 = s_scr[lt, 0, rows, :]
            reach = jnp.zeros((ph, LANES), F32)
            for i in range(P_TOPK):
                reach = jnp.where(s1 == a[i], count[i], reach)
            l1_ref[lt, rows, :] = _twice_bf16(reach)
            s2 = s_scr[lt, 1, rows, :]
            r2_scr[rows, :] = _tree(jnp.add, [jnp.where(b[j] > s2, 1.0, 0.0)
                                              for j in range(P_TOPK)])

    @pl.when(undecided != 0.0)
    def _():
        sc_scr[...] = s_scr[lt, 0]
        _extract_top(sc_scr, r1_scr, top_scr)
        sc_scr[...] = s_scr[lt, 1]
        _extract_top(sc_scr, r2_scr, top_scr)
        for k in range(N_KEYS):
            rows = key_rows(k)
            r1 = r1_scr[rows, :]
            reach = jnp.zeros((ph, LANES), F32)
            for i in range(P_TOPK):
                reach = jnp.where(r1 == float(i), count[i], reach)
            l1_ref[lt, rows, :] = _twice_bf16(reach)

    half_inv_z = 0.5 * inv_z
    for k in range(N_KEYS):
        rows = key_rows(k)
        w1_ref[lt, rows, :] = _twice_bf16(jnp.exp(s_scr[lt, 0, rows, :] - a[0]) * half_inv_z)
        sc_scr[rows, :] = jnp.exp(s_scr[lt, 1, rows, :] - b[0])
    for h in range(ph):
        r2_ref[lt, h] = _pack_bf16(r2_scr[pl.ds(h, N_KEYS, stride=ph), :])
        e2_ref[lt, h] = _pack_bf16(sc_scr[pl.ds(h, N_KEYS, stride=ph), :])


def _peer_topk(x, wq_t, subkeys):
    t = x.shape[0]
    nt = t // LANES
    tm = PEER_TOPK_TOKENS if t % PEER_TOPK_TOKENS == 0 else LANES
    n_tiles = tm // LANES
    tile = pl.BlockSpec((n_tiles, _HK, LANES), lambda i: (i, 0, 0))
    tile3 = pl.BlockSpec((n_tiles, P_HEADS, N_KEYS // 2, LANES), lambda i: (i, 0, 0, 0))
    flat = jax.ShapeDtypeStruct((nt, _HK, LANES), jnp.uint32)
    cube = jax.ShapeDtypeStruct((nt, P_HEADS, N_KEYS // 2, LANES), jnp.uint32)
    return pl.pallas_call(
        _peer_topk_kernel,
        grid=(t // tm,),
        in_specs=[pl.BlockSpec((tm, D_MODEL), lambda i: (i, 0)),
                  pl.BlockSpec((2 * _HK, D_MODEL), lambda i: (0, 0)),
                  pl.BlockSpec((2 * P_HEADS, N_KEYS, D_KEY // 2), lambda i: (0, 0, 0))],
        out_specs=[tile, tile, tile3, tile3],
        out_shape=[flat, flat, cube, cube],
        scratch_shapes=[pltpu.VMEM((n_tiles, 2, _HK, LANES), F32),
                        pltpu.VMEM((_HK, LANES), F32),
                        pltpu.VMEM((_HK, LANES), F32),
                        pltpu.VMEM((_HK, LANES), F32),
                        pltpu.VMEM((P_TOPK, P_HEADS, LANES), F32)],
        compiler_params=_params("parallel"),
        name="peer_topk",
    )(x, wq_t, subkeys)


def _replicated_bf16(ref, tile, row):
    return _unpack_bf16(ref[tile, pl.ds(row, SUBLANES, stride=0), :])


def _peer_mix_kernel(x_ref, u_ref, vt_ref, l1_ref, w1_ref, r2_ref, e2_ref, g_ref, b_ref,
                     out_ref, xt_scr, acc_scr, act_scr, p_scr):
    e = pl.program_id(1)
    n_groups, _, group = xt_scr.shape

    @pl.when(e == 0)
    def _():
        for gi in range(n_groups):
            xt_scr[gi] = x_ref[gi * group:(gi + 1) * group, :].T.astype(BF16)
        acc_scr[...] = jnp.zeros(acc_scr.shape, F32)

    def project(gi):
        act_scr[gi] = jnp.dot(_unpack_bf16(u_ref[...]), xt_scr[gi], preferred_element_type=F32)

    def mix(gi):
        acc_scr[gi] += jnp.dot(_unpack_bf16(vt_ref[...]), p_scr[gi], preferred_element_type=F32)

    lookahead = min(2, n_groups)
    for gi in range(lookahead):
        project(gi)
    for gi in range(n_groups):
        for c0 in range(0, group, LANES):
            cols = slice(c0, c0 + LANES)
            ct = (gi * group + c0) // LANES
            for kb in range(0, N_KEYS, PEER_KEY_BLOCK):
                subs = range(kb, kb + PEER_KEY_BLOCK, BF16_ROWS)
                gates = {(jj, k0): jnp.zeros((BF16_ROWS, LANES), BF16)
                         for jj in range(PEER_ROWS_PER_STEP) for k0 in subs}
                for h in range(P_HEADS):
                    rows_h = [jj * P_HEADS + h for jj in range(PEER_ROWS_PER_STEP)]
                    reach = [_replicated_bf16(l1_ref, ct, r) for r in rows_h]
                    weight = [_replicated_bf16(w1_ref, ct, r) for r in rows_h]
                    for k0 in subs:
                        words = slice(k0 // 2, k0 // 2 + SUBLANES)
                        r2 = _unpack_bf16(r2_ref[ct, h, words, :])
                        e2 = _unpack_bf16(e2_ref[ct, h, words, :])
                        for jj in range(PEER_ROWS_PER_STEP):
                            picked = jnp.where(r2 < reach[jj], e2, jnp.zeros_like(e2))
                            gates[jj, k0] = gates[jj, k0] + picked * weight[jj]
                for jj in range(PEER_ROWS_PER_STEP):
                    for k0 in subs:
                        rows = slice(jj * N_KEYS + k0, jj * N_KEYS + k0 + BF16_ROWS)
                        a = act_scr[gi, rows, cols]
                        gelu = a * (1.0 + lax.erf(a * (2.0 ** -0.5)))
                        p_scr[gi, rows, cols] = gelu.astype(BF16) * gates[jj, k0]
        if gi >= 1:
            mix(gi - 1)
        if gi + lookahead < n_groups:
            project(gi + lookahead)
    mix(n_groups - 1)

    @pl.when(e == pl.num_programs(1) - 1)
    def _():
        for gi in range(n_groups):
            rows = slice(gi * group, (gi + 1) * group)
            y = DN_ALPHA * x_ref[rows, :] + acc_scr[gi].T
            out_ref[rows, :] = _layer_norm(y, g_ref[...], b_ref[...])


def _peer_mix(x, u, vt, l1, w1, r2, e2, g, b, tm):
    t = x.shape[0]
    te = PEER_ROWS_PER_STEP * N_KEYS
    tr = PEER_ROWS_PER_STEP * P_HEADS
    nt = tm // LANES
    group = min(tm, PEER_COLUMN_GROUP)
    n_groups = tm // group
    return pl.pallas_call(
        _peer_mix_kernel,
        grid=(t // tm, N_EXPERTS // te),
        in_specs=[pl.BlockSpec((tm, D_MODEL), lambda i, e: (i, 0)),
                  pl.BlockSpec((te // 2, D_MODEL), lambda i, e: (e, 0)),
                  pl.BlockSpec((D_MODEL // 2, te), lambda i, e: (0, e)),
                  pl.BlockSpec((nt, tr, LANES), lambda i, e: (i, e, 0)),
                  pl.BlockSpec((nt, tr, LANES), lambda i, e: (i, e, 0)),
                  pl.BlockSpec((nt, P_HEADS, N_KEYS // 2, LANES), lambda i, e: (i, 0, 0, 0)),
                  pl.BlockSpec((nt, P_HEADS, N_KEYS // 2, LANES), lambda i, e: (i, 0, 0, 0)),
                  pl.BlockSpec((1, D_MODEL), lambda i, e: (0, 0)),
                  pl.BlockSpec((1, D_MODEL), lambda i, e: (0, 0))],
        out_specs=pl.BlockSpec((tm, D_MODEL), lambda i, e: (i, 0)),
        out_shape=jax.ShapeDtypeStruct((t, D_MODEL), F32),
        scratch_shapes=[pltpu.VMEM((n_groups, D_MODEL, group), BF16),
                        pltpu.VMEM((n_groups, D_MODEL, group), F32),
                        pltpu.VMEM((n_groups, te, group), F32),
                        pltpu.VMEM((n_groups, te, group), BF16)],
        compiler_params=_params("parallel", "arbitrary"),
        name="peer_mix",
    )(x, u, vt, l1, w1, r2, e2, g, b)


def _peer_layer(x, pw, g, b):
    t = x.shape[0]
    l1, w1, r2, e2 = _peer_topk(x, pw["wq_t"], pw["subkeys"])
    tm = PEER_TOKEN_TILE if t % PEER_TOKEN_TILE == 0 else LANES
    return _peer_mix(x, pw["u"], pw["vt"], l1, w1, r2, e2, g, b, tm)


def _t5_bucket(dist):
    max_exact = N_BUCKETS // 2
    d = jnp.maximum(dist, 0)
    df = jnp.maximum(d, 1).astype(F32)
    large = max_exact + (jnp.log(df / max_exact) / math.log(MAX_DISTANCE / max_exact)
                         * (N_BUCKETS - max_exact)).astype(jnp.int32)
    large = jnp.minimum(large, N_BUCKETS - 1)
    return jnp.where(d < max_exact, d, large)


def _row(a):
    return a.reshape(1, -1)


def _pack_row_pairs(a):
    n2, m = a.shape
    return lax.bitcast_convert_type(a.reshape(n2 // 2, 2, m).swapaxes(1, 2), jnp.uint32)


def _prepare(w):
    n_gate = 2 * A_HEADS
    w_in = w["a_w_in"][0]
    b_in = w["a_b_in"][0]
    prep = {
        "w_in_main": w_in[:, :2 * A_INNER].astype(BF16),
        "b_in_main": _row(b_in[:2 * A_INNER]),
        "w_in_gate": jnp.pad(w_in[:, 2 * A_INNER:], ((0, 0), (0, LANES - n_gate))).astype(BF16),
        "b_in_gate": _row(jnp.pad(b_in[2 * A_INNER:], (0, LANES - n_gate))),
        "conv_w": w["a_conv_w"][0],
        "conv_b": _row(w["a_conv_b"][0]),
        "wq": w["a_w_q"][0].astype(BF16),
        "wk": w["a_w_k"][0].astype(BF16),
        "wv": w["a_w_v"][0].astype(BF16),
        "w_out": w["a_w_out"][0].astype(BF16),
        "w_qkv": jnp.concatenate([w["b_w_q"][0], w["kv_w"]], axis=1).astype(BF16),
        "w_o": w["b_w_o"][0].astype(BF16),
        "sinks": _row(w["b_sinks"][0]),
        "rel_bias": w["rel_bias"],
    }
    peer = []
    half = D_KEY // 2
    for layer in range(DEPTH):
        wq = w["peer_w_q"][layer].reshape(D_MODEL, P_HEADS, 2, half)
        sk = w["peer_subkeys"][layer]
        peer.append({
            "wq_t": wq.transpose(2, 1, 3, 0).reshape(2 * _HK, D_MODEL).astype(BF16),
            "subkeys": sk.reshape(2 * P_HEADS, N_KEYS, half).astype(BF16),
            "u": _pack_row_pairs(w["peer_u"][layer].astype(BF16)),
            "vt": _pack_row_pairs(w["peer_v"][layer].T.astype(BF16)),
        })
    prep["peer"] = peer
    return prep


def _zero_bias(n):
    return jnp.zeros((1, n), F32)


def _attention_tables(rel_bias):
    qi = jnp.arange(WINDOW)[:, None]
    kj = jnp.arange(2 * WINDOW)[None, :]
    bucket = _t5_bucket(qi + WINDOW - kj).astype(jnp.int32)
    return _bias_tables(bucket, rel_bias)


def _prompt_trunk(x, w, p, bias):
    batch, length, d = x.shape
    t = batch * length
    xt = x.reshape(t, d)
    proj = _linear(xt, p["w_in_main"], p["b_in_main"], 512, LINEAR_COLUMNS)
    gates = _linear(xt, p["w_in_gate"], p["b_in_gate"], 512, LANES)
    row_tab, col_tab = _gate_tables(gates, batch, length, MLSTM_CHUNK)
    q, k, v = _conv_qkv(proj, p["conv_w"], p["conv_b"], p["wq"], p["wk"], p["wv"], batch, length)
    h, c_new, n_new, m_new = _mlstm_prompt(q, k, v, row_tab, col_tab, batch, length, MLSTM_CHUNK)
    x1 = _outproj_ln(h, proj, p["w_out"], xt, _row(w["ln_mix_g"][0]), _row(w["ln_mix_b"][0]), 256)
    x2 = _peer_layer(x1, p["peer"][0], _row(w["ln_ffn_g"][0]), _row(w["ln_ffn_b"][0]))
    qkv = _linear(x2, p["w_qkv"], _zero_bias(p["w_qkv"].shape[1]), 512, p["w_qkv"].shape[1])
    o = _swa_prompt(qkv, bias, p["sinks"], batch, length)
    x3 = _outproj_ln(o, None, p["w_o"], x2, _row(w["ln_mix_g"][1]), _row(w["ln_mix_b"][1]), 256)
    x4 = _peer_layer(x3, p["peer"][1], _row(w["ln_ffn_g"][1]), _row(w["ln_ffn_b"][1]))

    kvw = B_KV_HEADS * B_HEAD_DIM
    qkv3 = qkv.reshape(batch, length, -1)
    k_win = qkv3[:, -WINDOW:, D_MODEL:D_MODEL + kvw].reshape(batch, WINDOW, B_KV_HEADS, B_HEAD_DIM)
    v_win = qkv3[:, -WINDOW:, D_MODEL + kvw:].reshape(batch, WINDOW, B_KV_HEADS, B_HEAD_DIM)
    conv = proj.reshape(batch, length, -1)[:, -(A_CONV_W - 1):, :A_INNER]
    return (x4.reshape(batch, length, d),
            c_new[None],
            n_new.reshape(1, batch, A_HEADS, A_HEAD_DIM),
            m_new[:, :, 0, 0][None],
            conv[None], k_win, v_win)


def _sample_trunk(x, conv0, c0, n0, m0, k_buf, v_buf, w, p, bias):
    batch, length, d = x.shape
    xt = x.reshape(batch, d)
    proj = _linear(xt, p["w_in_main"], p["b_in_main"], batch, 512)
    gates = _linear(xt, p["w_in_gate"], p["b_in_gate"], batch, LANES)
    buf = conv0[0]
    q, k, v, qt, kt = _sample_conv_qkv(proj, buf.reshape(batch, -1), p["conv_w"], p["conv_b"],
                                       p["wq"], p["wk"], p["wv"])
    h, c_new, n_new, m_new = _sample_mlstm(q, k, v, qt, kt, gates, m0[0], n0[0], c0[0])
    x1 = _outproj_ln(h.reshape(batch, A_INNER), proj, p["w_out"], xt,
                     _row(w["ln_mix_g"][0]), _row(w["ln_mix_b"][0]), batch)
    x2 = _peer_layer(x1, p["peer"][0], _row(w["ln_ffn_g"][0]), _row(w["ln_ffn_b"][0]))
    qkv = _linear(x2, p["w_qkv"], _zero_bias(p["w_qkv"].shape[1]), batch, 512)
    kvw = B_KV_HEADS * B_HEAD_DIM
    o = _swa_decode(qkv, k_buf.reshape(batch, WINDOW, kvw), v_buf.reshape(batch, WINDOW, kvw),
                    bias[:, 0, :], p["sinks"].reshape(B_HEADS, 1))
    x3 = _outproj_ln(o, None, p["w_o"], x2, _row(w["ln_mix_g"][1]), _row(w["ln_mix_b"][1]), batch)
    x4 = _peer_layer(x3, p["peer"][1], _row(w["ln_ffn_g"][1]), _row(w["ln_ffn_b"][1]))

    k_new = qkv[:, D_MODEL:D_MODEL + kvw].reshape(batch, 1, B_KV_HEADS, B_HEAD_DIM)
    v_new = qkv[:, D_MODEL + kvw:].reshape(batch, 1, B_KV_HEADS, B_HEAD_DIM)
    k_win = jnp.concatenate([k_buf[:, 1:], k_new], axis=1)
    v_win = jnp.concatenate([v_buf[:, 1:], v_new], axis=1)
    conv = jnp.concatenate([buf[:, 1:], proj[:, None, :A_INNER]], axis=1)
    return (x4.reshape(batch, length, d),
            c_new[None],
            n_new.reshape(1, batch, A_HEADS, A_HEAD_DIM),
            m_new[:, :, 0, 0][None],
            conv[None], k_win, v_win)


def kernel(x_prompt, x_sample, state_mlstm_C, state_mlstm_n, state_mlstm_m, state_mlstm_conv,
           cache_k_win, cache_v_win, a_w_in, a_b_in, a_conv_w, a_conv_b, a_w_q, a_w_k, a_w_v,
           a_w_out, kv_w, b_w_q, b_w_o, b_sinks, rel_bias, ln_mix_g, ln_mix_b, ln_ffn_g,
           ln_ffn_b, peer_w_q, peer_subkeys, peer_u, peer_v):
    w = {"a_w_in": a_w_in, "a_b_in": a_b_in, "a_conv_w": a_conv_w, "a_conv_b": a_conv_b,
         "a_w_q": a_w_q, "a_w_k": a_w_k, "a_w_v": a_w_v, "a_w_out": a_w_out, "kv_w": kv_w,
         "b_w_q": b_w_q, "b_w_o": b_w_o, "b_sinks": b_sinks, "rel_bias": rel_bias,
         "ln_mix_g": ln_mix_g, "ln_mix_b": ln_mix_b, "ln_ffn_g": ln_ffn_g, "ln_ffn_b": ln_ffn_b,
         "peer_w_q": peer_w_q, "peer_subkeys": peer_subkeys, "peer_u": peer_u, "peer_v": peer_v}
    p = _prepare(w)
    bias = _attention_tables(rel_bias)
    prompt = _prompt_trunk(x_prompt, w, p, bias)
    sample = _sample_trunk(x_sample, state_mlstm_conv, state_mlstm_C, state_mlstm_n,
                           state_mlstm_m, cache_k_win, cache_v_win, w, p, bias)
    return (prompt[0], sample[0]) + prompt[1:] + sample[1:]
```

```python
import functools
import math

import jax
import jax.numpy as jnp
from jax import lax
from jax.experimental import pallas as pl
from jax.experimental.pallas import tpu as pltpu

D_MODEL = 1024
DEPTH = 2
A_HEADS = 4
A_INNER = 2 * D_MODEL
A_HEAD_DIM = A_INNER // A_HEADS
A_CONV_W = 4
B_HEADS = 16
B_KV_HEADS = 4
B_GROUP = B_HEADS // B_KV_HEADS
B_HEAD_DIM = D_MODEL // B_HEADS
WINDOW = 128
N_BUCKETS = 32
MAX_DISTANCE = 128
P_HEADS = 8
N_KEYS = 128
N_EXPERTS = N_KEYS * N_KEYS
D_KEY = 256
P_TOPK = 16
DN_ALPHA = (2 * DEPTH) ** 0.25
LN_EPS = 1e-5
NEG_INF = -1e30

LANES = 128
SUBLANES = 8
BF16_ROWS = 2 * SUBLANES
VMEM_LIMIT_BYTES = 48 * 1024 * 1024

LINEAR_COLUMNS = 2048
MLSTM_CHUNK = 256
PEER_TOPK_TOKENS = 256
PEER_TOKEN_TILE = 1024
PEER_ROWS_PER_STEP = 8
PEER_COLUMN_GROUP = 256
PEER_KEY_BLOCK = 16
NOT_SELECTED_RANK = float(P_TOPK)

F32 = jnp.float32
BF16 = jnp.bfloat16


def _params(*sem):
    return pltpu.CompilerParams(dimension_semantics=sem, vmem_limit_bytes=VMEM_LIMIT_BYTES)


def _tree(op, vals):
    vals = list(vals)
    while len(vals) > 1:
        nxt = [op(vals[i], vals[i + 1]) for i in range(0, len(vals) - 1, 2)]
        if len(vals) % 2:
            nxt.append(vals[-1])
        vals = nxt
    return vals[0]


def _layer_norm(y, g, b):
    mu = jnp.mean(y, axis=-1, keepdims=True)
    yc = y - mu
    var = jnp.mean(yc * yc, axis=-1, keepdims=True)
    return yc * lax.rsqrt(var + LN_EPS) * g + b


def _linear_kernel(x_ref, w_ref, b_ref, o_ref):
    x = x_ref[...].astype(BF16)
    o_ref[...] = jnp.dot(x, w_ref[...], preferred_element_type=F32) + b_ref[...]


def _linear(x, w, b, tm, tn):
    t, k = x.shape
    n = w.shape[1]
    return pl.pallas_call(
        _linear_kernel,
        grid=(t // tm, n // tn),
        in_specs=[pl.BlockSpec((tm, k), lambda i, j: (i, 0)),
                  pl.BlockSpec((k, tn), lambda i, j: (0, j)),
                  pl.BlockSpec((1, tn), lambda i, j: (0, j))],
        out_specs=pl.BlockSpec((tm, tn), lambda i, j: (i, j)),
        out_shape=jax.ShapeDtypeStruct((t, n), F32),
        compiler_params=_params("parallel", "arbitrary"),
        name="linear",
    )(x, w, b)


def _outproj_ln_kernel(gated, *refs):
    if gated:
        a_ref, o_ref, w_ref, res_ref, g_ref, b_ref, out_ref = refs
        act = jax.nn.sigmoid(o_ref[...]) * a_ref[...]
    else:
        a_ref, w_ref, res_ref, g_ref, b_ref, out_ref = refs
        act = a_ref[...]
    sub = jnp.dot(act.astype(BF16), w_ref[...], preferred_element_type=F32)
    out_ref[...] = _layer_norm(DN_ALPHA * res_ref[...] + sub, g_ref[...], b_ref[...])


def _outproj_ln(act, gate_src, w, res, g, b, tm):
    t, k = act.shape
    d = w.shape[1]
    gated = gate_src is not None
    in_specs = [pl.BlockSpec((tm, k), lambda i: (i, 0))]
    args = [act]
    if gated:
        in_specs.append(pl.BlockSpec((tm, k), lambda i: (i, 1)))
        args.append(gate_src)
    in_specs += [pl.BlockSpec((k, d), lambda i: (0, 0)),
                 pl.BlockSpec((tm, d), lambda i: (i, 0)),
                 pl.BlockSpec((1, d), lambda i: (0, 0)),
                 pl.BlockSpec((1, d), lambda i: (0, 0))]
    args += [w, res, g, b]
    return pl.pallas_call(
        functools.partial(_outproj_ln_kernel, gated),
        grid=(t // tm,),
        in_specs=in_specs,
        out_specs=pl.BlockSpec((tm, d), lambda i: (i, 0)),
        out_shape=jax.ShapeDtypeStruct((t, d), F32),
        compiler_params=_params("parallel"),
        name="outproj_ln",
    )(*args)


def _log_sigmoid(x):
    return jnp.minimum(x, 0.0) - jnp.log1p(jnp.exp(-jnp.abs(x)))


def _gates_kernel(chunk, g_ref, row_ref, col_ref):
    length = g_ref.shape[0]
    gt = g_ref[...].T
    top = gt[0:SUBLANES]
    row = lax.broadcasted_iota(jnp.int32, top.shape, 0)
    pos = lax.broadcasted_iota(jnp.int32, top.shape, 1) % chunk
    is_f = row >= A_HEADS
    x = jnp.where(is_f, _log_sigmoid(top), 0.0)
    shift = 1
    while shift < chunk:
        x = x + jnp.where(pos >= shift, pltpu.roll(x, shift, 1), 0.0)
        shift *= 2
    table = jnp.where(is_f, x, top)
    row_ref[...] = table
    padded = jnp.concatenate([table, jnp.zeros((LANES - SUBLANES, length), F32)], axis=0)
    col_ref[...] = padded.T


def _gate_tables(gates, batch, length, chunk):
    return pl.pallas_call(
        functools.partial(_gates_kernel, chunk),
        grid=(batch,),
        in_specs=[pl.BlockSpec((length, LANES), lambda b: (b, 0))],
        out_specs=[pl.BlockSpec((None, SUBLANES, length), lambda b: (b, 0, 0)),
                   pl.BlockSpec((length, LANES), lambda b: (b, 0))],
        out_shape=[jax.ShapeDtypeStruct((batch, SUBLANES, length), F32),
                   jax.ShapeDtypeStruct((batch * length, LANES), F32)],
        compiler_params=_params("parallel"),
        name="gate_tables",
    )(gates)


CONV_ROW_CHUNK = 512
CONV_PAD = SUBLANES


def _conv_qkv_kernel(xm_ref, cw_ref, cb_ref, wq_ref, wk_ref, wv_ref,
                     q_ref, k_ref, v_ref, pad_ref):
    length = xm_ref.shape[0]
    pad_ref[0:CONV_PAD, :] = jnp.zeros((CONV_PAD, A_HEAD_DIM), F32)
    pad_ref[CONV_PAD:CONV_PAD + length, :] = xm_ref[...]
    first = CONV_PAD - (A_CONV_W - 1)
    for c0 in range(0, length, CONV_ROW_CHUNK):
        acc = cb_ref[...]
        for w in range(A_CONV_W):
            acc = acc + pad_ref[c0 + first + w:c0 + first + w + CONV_ROW_CHUNK, :] * cw_ref[w:w + 1, :]
        xc = (acc * jax.nn.sigmoid(acc)).astype(BF16)
        xm = xm_ref[c0:c0 + CONV_ROW_CHUNK, :].astype(BF16)
        rows = slice(c0, c0 + CONV_ROW_CHUNK)
        q_ref[rows, :] = jnp.dot(xc, wq_ref[...], preferred_element_type=F32).astype(BF16)
        k = jnp.dot(xc, wk_ref[...], preferred_element_type=F32) * (A_HEAD_DIM ** -0.5)
        k_ref[rows, :] = k.astype(BF16)
        v_ref[rows, :] = jnp.dot(xm, wv_ref[...], preferred_element_type=F32).astype(BF16)


def _conv_qkv(proj, cw, cb, wq, wk, wv, batch, length):
    t = batch * length
    hd = A_HEAD_DIM
    tok = pl.BlockSpec((length, hd), lambda b, h: (b, h))
    wspec = pl.BlockSpec((None, hd, hd), lambda b, h: (h, 0, 0))
    out = jax.ShapeDtypeStruct((t, A_INNER), BF16)
    return pl.pallas_call(
        _conv_qkv_kernel,
        grid=(batch, A_HEADS),
        in_specs=[tok,
                  pl.BlockSpec((A_CONV_W, hd), lambda b, h: (0, h)),
                  pl.BlockSpec((1, hd), lambda b, h: (0, h)),
                  wspec, wspec, wspec],
        out_specs=[tok, tok, tok],
        out_shape=[out, out, out],
        scratch_shapes=[pltpu.VMEM((length + CONV_PAD, hd), F32)],
        compiler_params=_params("parallel", "arbitrary"),
        name="conv_qkv",
    )(proj, cw, cb, wq, wk, wv)


def _mlstm_kernel(q_ref, k_ref, v_ref, row_ref, col_ref,
                  h_ref, c_out_ref, n_out_ref, m_out_ref,
                  c_scr, n_scr, m_scr):
    ci = pl.program_id(1)
    chunk = q_ref.shape[0]
    hd = A_HEAD_DIM

    @pl.when(ci == 0)
    def _():
        c_scr[...] = jnp.zeros(c_scr.shape, F32)
        n_scr[...] = jnp.zeros(n_scr.shape, F32)
        m_scr[...] = jnp.zeros(m_scr.shape, F32)

    t_idx = lax.broadcasted_iota(jnp.int32, (chunk, chunk), 0)
    s_idx = lax.broadcasted_iota(jnp.int32, (chunk, chunk), 1)
    causal = s_idx <= t_idx
    for h in range(A_HEADS):
        cols = slice(h * hd, (h + 1) * hd)
        qh, kh, vh = q_ref[:, cols], k_ref[:, cols], v_ref[:, cols]
        i_col = col_ref[:, h:h + 1]
        f_col = col_ref[:, A_HEADS + h:A_HEADS + h + 1]
        i_row = row_ref[h:h + 1, :]
        f_row = row_ref[A_HEADS + h:A_HEADS + h + 1, :]
        m_prev = m_scr[h][:, 0:1]
        d = jnp.where(causal, f_col - f_row + i_row, NEG_INF)
        b_inter = f_col + m_prev
        m_t = jnp.maximum(b_inter, jnp.max(d, axis=1, keepdims=True))
        qk = lax.dot_general(qh, kh, (((1,), (1,)), ((), ())), preferred_element_type=F32)
        s = qk * jnp.exp(d - m_t)
        w_inter = jnp.exp(b_inter - m_t)
        q_c = jnp.dot(qh, c_scr[h].astype(BF16), preferred_element_type=F32)
        num = jnp.dot(s.astype(BF16), vh, preferred_element_type=F32) + w_inter * q_c
        q_n = jnp.sum(qh.astype(F32) * n_scr[h], axis=1, keepdims=True)
        den = jnp.sum(s, axis=1, keepdims=True) + w_inter * q_n
        h_ref[:, cols] = num / jnp.maximum(jnp.abs(den), jnp.exp(-m_t))
        f_last = f_col[chunk - 1:chunk, :]
        g = f_last - f_col + i_col
        m_new = jnp.maximum(f_last + m_prev, jnp.max(g, axis=0, keepdims=True))
        decay = jnp.exp(f_last + m_prev - m_new)
        wk = jnp.exp(g - m_new) * kh.astype(F32)
        kv = lax.dot_general(wk.astype(BF16), vh, (((0,), (0,)), ((), ())),
                             preferred_element_type=F32)
        c_scr[h] = decay * c_scr[h] + kv
        n_scr[h] = decay * n_scr[h] + jnp.sum(wk, axis=0, keepdims=True)
        m_scr[h] = jnp.broadcast_to(m_new, (1, LANES))

    @pl.when(ci == pl.num_programs(1) - 1)
    def _():
        c_out_ref[...] = c_scr[...]
        n_out_ref[...] = n_scr[...]
        m_out_ref[...] = m_scr[...]


def _mlstm_prompt(q, k, v, row_tab, col_tab, batch, length, chunk):
    nc = length // chunk
    t = batch * length
    tok = pl.BlockSpec((chunk, A_INNER), lambda b, c: (b * nc + c, 0))
    return pl.pallas_call(
        _mlstm_kernel,
        grid=(batch, nc),
        in_specs=[tok, tok, tok,
                  pl.BlockSpec((None, SUBLANES, chunk), lambda b, c: (b, 0, c)),
                  pl.BlockSpec((chunk, LANES), lambda b, c: (b * nc + c, 0))],
        out_specs=[tok,
                   pl.BlockSpec((None, A_HEADS, A_HEAD_DIM, A_HEAD_DIM), lambda b, c: (b, 0, 0, 0)),
                   pl.BlockSpec((None, A_HEADS, 1, A_HEAD_DIM), lambda b, c: (b, 0, 0, 0)),
                   pl.BlockSpec((None, A_HEADS, 1, LANES), lambda b, c: (b, 0, 0, 0))],
        out_shape=[jax.ShapeDtypeStruct((t, A_INNER), F32),
                   jax.ShapeDtypeStruct((batch, A_HEADS, A_HEAD_DIM, A_HEAD_DIM), F32),
                   jax.ShapeDtypeStruct((batch, A_HEADS, 1, A_HEAD_DIM), F32),
                   jax.ShapeDtypeStruct((batch, A_HEADS, 1, LANES), F32)],
        scratch_shapes=[pltpu.VMEM((A_HEADS, A_HEAD_DIM, A_HEAD_DIM), F32),
                        pltpu.VMEM((A_HEADS, 1, A_HEAD_DIM), F32),
                        pltpu.VMEM((A_HEADS, 1, LANES), F32)],
        compiler_params=_params("parallel", "arbitrary"),
        name="mlstm_prompt",
    )(q, k, v, row_tab, col_tab)


def _sample_conv_qkv_kernel(proj_ref, buf_ref, cw_ref, cb_ref, wq_ref, wk_ref, wv_ref,
                            q_ref, k_ref, v_ref, qt_ref, kt_ref):
    xm = proj_ref[:, 0:A_INNER]
    acc = cb_ref[...] + xm * cw_ref[A_CONV_W - 1:A_CONV_W, :]
    for w in range(A_CONV_W - 1):
        acc = acc + buf_ref[:, w * A_INNER:(w + 1) * A_INNER] * cw_ref[w:w + 1, :]
    xc = (acc * jax.nn.sigmoid(acc)).astype(BF16)
    xmb = xm.astype(BF16)
    for h in range(A_HEADS):
        cols = slice(h * A_HEAD_DIM, (h + 1) * A_HEAD_DIM)
        q_ref[:, cols] = jnp.dot(xc[:, cols], wq_ref[h], preferred_element_type=F32)
        k_ref[:, cols] = (jnp.dot(xc[:, cols], wk_ref[h], preferred_element_type=F32)
                          * (A_HEAD_DIM ** -0.5))
        v_ref[:, cols] = jnp.dot(xmb[:, cols], wv_ref[h], preferred_element_type=F32)
    qt_ref[...] = q_ref[...].T
    kt_ref[...] = k_ref[...].T


def _sample_conv_qkv(proj, conv_buf_flat, cw, cb, wq, wk, wv):
    b = proj.shape[0]
    row = jax.ShapeDtypeStruct((b, A_INNER), F32)
    col = jax.ShapeDtypeStruct((A_INNER, b), F32)
    return pl.pallas_call(
        _sample_conv_qkv_kernel,
        out_shape=[row, row, row, col, col],
        compiler_params=pltpu.CompilerParams(vmem_limit_bytes=VMEM_LIMIT_BYTES),
        name="sample_conv_qkv",
    )(proj, conv_buf_flat, cw, cb, wq, wk, wv)


def _sample_mlstm_kernel(q_ref, k_ref, v_ref, qt_ref, kt_ref, g_ref, m0_ref, n0_ref, c0_ref,
                         h_ref, c_ref, n_ref, m_ref):
    b = pl.program_id(0)
    hd = A_HEAD_DIM
    is_b = lax.broadcasted_iota(jnp.int32, (hd, LANES), 1) == b
    gates = g_ref[...]
    m0_all = m0_ref[...]
    for h in range(A_HEADS):
        cols = slice(h * hd, (h + 1) * hd)
        q_col = jnp.sum(jnp.where(is_b, qt_ref[cols, :], 0.0), axis=1, keepdims=True)
        k_col = jnp.sum(jnp.where(is_b, kt_ref[cols, :], 0.0), axis=1, keepdims=True)
        q_mat = jnp.broadcast_to(q_col, (hd, hd))
        k_mat = jnp.broadcast_to(k_col, (hd, hd))
        q_row, k_row, v_row = q_ref[:, cols], k_ref[:, cols], v_ref[:, cols]
        log_i = gates[:, h:h + 1]
        log_f = _log_sigmoid(gates[:, A_HEADS + h:A_HEADS + h + 1])
        m0 = m0_all[:, h:h + 1]
        m_t = jnp.maximum(log_f + m0, log_i)
        w_inter = jnp.exp(log_f + m0 - m_t)
        w_new = jnp.exp(log_i - m_t)
        c0 = c0_ref[h]
        n0 = n0_ref[h]
        s = jnp.sum(q_row * k_row, axis=1, keepdims=True) * w_new
        q_c = jnp.sum(q_mat * c0, axis=0, keepdims=True)
        q_n = jnp.sum(q_row * n0, axis=1, keepdims=True)
        num = s * v_row + w_inter * q_c
        den = s + w_inter * q_n
        h_ref[:, cols] = num / jnp.maximum(jnp.abs(den), jnp.exp(-m_t))
        c_ref[h] = w_inter * c0 + (w_new * k_mat) * v_row
        n_ref[h] = w_inter * n0 + w_new * k_row
        m_ref[h] = jnp.broadcast_to(m_t, (1, LANES))


def _sample_mlstm(q, k, v, qt, kt, gates, m0, n0, c0):
    b = q.shape[0]
    hd = A_HEAD_DIM
    row3 = lambda a: a.reshape(b, 1, a.shape[-1])
    rspec = pl.BlockSpec((None, 1, A_INNER), lambda i: (i, 0, 0))
    cspec = pl.BlockSpec((A_INNER, LANES), lambda i: (0, 0))
    nspec = pl.BlockSpec((None, A_HEADS, 1, hd), lambda i: (i, 0, 0, 0))
    mspec = pl.BlockSpec((None, A_HEADS, 1, LANES), lambda i: (i, 0, 0, 0))
    big = pl.BlockSpec((None, A_HEADS, hd, hd), lambda i: (i, 0, 0, 0))
    return pl.pallas_call(
        _sample_mlstm_kernel,
        grid=(b,),
        in_specs=[rspec, rspec, rspec, cspec, cspec,
                  pl.BlockSpec((None, 1, LANES), lambda i: (i, 0, 0)),
                  pl.BlockSpec((None, 1, A_HEADS), lambda i: (i, 0, 0)),
                  nspec, big],
        out_specs=[rspec, big, nspec, mspec],
        out_shape=[jax.ShapeDtypeStruct((b, 1, A_INNER), F32),
                   jax.ShapeDtypeStruct((b, A_HEADS, hd, hd), F32),
                   jax.ShapeDtypeStruct((b, A_HEADS, 1, hd), F32),
                   jax.ShapeDtypeStruct((b, A_HEADS, 1, LANES), F32)],
        compiler_params=_params("parallel"),
        name="sample_mlstm",
    )(row3(q), row3(k), row3(v), qt, kt, row3(gates), row3(m0),
      n0.reshape(b, A_HEADS, 1, hd), c0)


def _bias_kernel(bucket_ref, rel_ref, bias_ref):
    h = pl.program_id(0)
    bucket = bucket_ref[...]
    acc = jnp.zeros(bucket.shape, F32)
    for n in range(N_BUCKETS):
        acc = jnp.where(bucket == n, rel_ref[n, h], acc)
    bias_ref[...] = acc


def _bias_tables(bucket, rel_bias):
    w, w2 = bucket.shape
    return pl.pallas_call(
        _bias_kernel,
        grid=(B_HEADS,),
        in_specs=[pl.BlockSpec((w, w2), lambda h: (0, 0)),
                  pl.BlockSpec(memory_space=pltpu.SMEM)],
        out_specs=pl.BlockSpec((None, w, w2), lambda h: (h, 0, 0)),
        out_shape=jax.ShapeDtypeStruct((B_HEADS, w, w2), F32),
        compiler_params=_params("arbitrary"),
        name="bias_tables",
    )(bucket, rel_bias)


def _swa_prompt_kernel(q_ref, kp_ref, kc_ref, vp_ref, vc_ref, bias_ref, sink_ref, o_ref):
    n = pl.program_id(1)
    w = WINDOW
    q = q_ref[...].astype(BF16)
    kk = jnp.concatenate([kp_ref[...], kc_ref[...]], axis=0).astype(BF16)
    vv = jnp.concatenate([vp_ref[...], vc_ref[...]], axis=0).astype(BF16)
    qi = lax.broadcasted_iota(jnp.int32, (w, 2 * w), 0)
    kj = lax.broadcasted_iota(jnp.int32, (w, 2 * w), 1)
    dist = qi + w - kj
    valid = (dist >= 0) & (dist < w) & ((kj >= w) | (n > 0))
    for h in range(B_HEADS):
        kvh = h // B_GROUP
        qh = q[:, h * B_HEAD_DIM:(h + 1) * B_HEAD_DIM]
        kh = kk[:, kvh * B_HEAD_DIM:(kvh + 1) * B_HEAD_DIM]
        vh = vv[:, kvh * B_HEAD_DIM:(kvh + 1) * B_HEAD_DIM]
        s = lax.dot_general(qh, kh, (((1,), (1,)), ((), ())), preferred_element_type=F32)
        s = jnp.where(valid, s * (B_HEAD_DIM ** -0.5) + bias_ref[h], NEG_INF)
        sink = sink_ref[0, h]
        m = jnp.maximum(jnp.max(s, axis=1, keepdims=True), sink)
        p = jnp.exp(s - m)
        den = jnp.sum(p, axis=1, keepdims=True) + jnp.exp(sink - m)
        p = (p / den).astype(BF16)
        o_ref[:, h * B_HEAD_DIM:(h + 1) * B_HEAD_DIM] = jnp.dot(p, vh, preferred_element_type=F32)


def _swa_prompt(qkv, bias, sinks, batch, length):
    nb = length // WINDOW
    kvw = B_KV_HEADS * B_HEAD_DIM
    kcol = D_MODEL // kvw
    cur = lambda b, n: b * nb + n
    prev = lambda b, n: b * nb + jnp.maximum(n - 1, 0)
    return pl.pallas_call(
        _swa_prompt_kernel,
        grid=(batch, nb),
        in_specs=[pl.BlockSpec((WINDOW, D_MODEL), lambda b, n: (cur(b, n), 0)),
                  pl.BlockSpec((WINDOW, kvw), lambda b, n: (prev(b, n), kcol)),
                  pl.BlockSpec((WINDOW, kvw), lambda b, n: (cur(b, n), kcol)),
                  pl.BlockSpec((WINDOW, kvw), lambda b, n: (prev(b, n), kcol + 1)),
                  pl.BlockSpec((WINDOW, kvw), lambda b, n: (cur(b, n), kcol + 1)),
                  pl.BlockSpec((B_HEADS, WINDOW, 2 * WINDOW), lambda b, n: (0, 0, 0)),
                  pl.BlockSpec(memory_space=pltpu.SMEM)],
        out_specs=pl.BlockSpec((WINDOW, D_MODEL), lambda b, n: (cur(b, n), 0)),
        out_shape=jax.ShapeDtypeStruct((batch * length, D_MODEL), F32),
        compiler_params=_params("parallel", "arbitrary"),
        name="swa_prompt",
    )(qkv, qkv, qkv, qkv, qkv, bias, sinks)


DECODE_BATCH_BLOCK = 8


def _swa_decode_kernel(qkv_ref, ck_ref, cv_ref, bias_ref, sink_ref, o_ref):
    w = WINDOW
    hd = B_HEAD_DIM
    kvw = B_KV_HEADS * hd
    scale = hd ** -0.5
    pos = lax.broadcasted_iota(jnp.int32, (B_GROUP, w), 1)
    for bi in range(DECODE_BATCH_BLOCK):
        row = qkv_ref[bi:bi + 1, :]
        for kvh in range(B_KV_HEADS):
            heads = slice(kvh * B_GROUP, (kvh + 1) * B_GROUP)
            kv_cols = slice(kvh * hd, (kvh + 1) * hd)
            q = jnp.concatenate([row[:, h * hd:(h + 1) * hd]
                                 for h in range(kvh * B_GROUP, (kvh + 1) * B_GROUP)], axis=0)
            k_new = row[:, D_MODEL + kvh * hd:D_MODEL + (kvh + 1) * hd]
            v_new = row[:, D_MODEL + kvw + kvh * hd:D_MODEL + kvw + (kvh + 1) * hd]
            kc = ck_ref[bi, :, kv_cols].astype(BF16)
            vc = cv_ref[bi, :, kv_cols].astype(BF16)
            s_c = lax.dot_general(q.astype(BF16), kc, (((1,), (1,)), ((), ())),
                                  preferred_element_type=F32)
            s_c = jnp.where(pos >= 1, s_c * scale + bias_ref[heads, 0:w], NEG_INF)
            s_n = jnp.sum(q * k_new, axis=1, keepdims=True) * scale + bias_ref[heads, w:w + 1]
            sink = sink_ref[heads, :]
            m = jnp.maximum(jnp.maximum(jnp.max(s_c, axis=1, keepdims=True), s_n), sink)
            p_c = jnp.exp(s_c - m)
            p_n = jnp.exp(s_n - m)
            den = jnp.sum(p_c, axis=1, keepdims=True) + p_n + jnp.exp(sink - m)
            o = (jnp.dot(p_c.astype(BF16), vc, preferred_element_type=F32) + p_n * v_new) / den
            for g in range(B_GROUP):
                h = kvh * B_GROUP + g
                o_ref[bi:bi + 1, h * hd:(h + 1) * hd] = o[g:g + 1, :]


def _swa_decode(qkv, cache_k, cache_v, bias_row, sinks_col):
    b = qkv.shape[0]
    kvw = B_KV_HEADS * B_HEAD_DIM
    bb = DECODE_BATCH_BLOCK
    return pl.pallas_call(
        _swa_decode_kernel,
        grid=(b // bb,),
        in_specs=[pl.BlockSpec((bb, qkv.shape[1]), lambda i: (i, 0)),
                  pl.BlockSpec((bb, WINDOW, kvw), lambda i: (i, 0, 0)),
                  pl.BlockSpec((bb, WINDOW, kvw), lambda i: (i, 0, 0)),
                  pl.BlockSpec((B_HEADS, 2 * WINDOW), lambda i: (0, 0)),
                  pl.BlockSpec((B_HEADS, 1), lambda i: (0, 0))],
        out_specs=pl.BlockSpec((bb, D_MODEL), lambda i: (i, 0)),
        out_shape=jax.ShapeDtypeStruct((b, D_MODEL), F32),
        compiler_params=_params("parallel"),
        name="swa_decode",
    )(qkv, cache_k, cache_v, bias_row, sinks_col)


_CANDIDATES = [(i, j) for i in range(P_TOPK) for j in range(P_TOPK)
               if (i + 1) * (j + 1) <= P_TOPK]
_BIG_POS = float(4 * P_TOPK * P_TOPK)
_HK = P_HEADS * N_KEYS


def _pack_bf16(x):
    return pltpu.bitcast(x.astype(BF16), jnp.uint32)


def _unpack_bf16(words):
    return pltpu.bitcast(words, BF16)


def _twice_bf16(x):
    high = pltpu.bitcast(x.astype(BF16).astype(F32), jnp.uint32)
    return high | (high >> 16)


def _extract_top(s_ref, r_ref, top_ref):
    ph = P_HEADS
    for k in range(N_KEYS):
        r_ref[k * ph:(k + 1) * ph, :] = jnp.full((ph, LANES), NOT_SELECTED_RANK, F32)

    def body(r, carry):
        best = _tree(jnp.maximum, [s_ref[k * ph:(k + 1) * ph, :] for k in range(N_KEYS)])
        first = _tree(jnp.minimum,
                      [jnp.where(s_ref[k * ph:(k + 1) * ph, :] == best, float(k), float(N_KEYS))
                       for k in range(N_KEYS)])
        rank = lax.convert_element_type(r, F32)
        for k in range(N_KEYS):
            rows = slice(k * ph, (k + 1) * ph)
            hit = first == float(k)
            s_ref[rows, :] = jnp.where(hit, -jnp.inf, s_ref[rows, :])
            r_ref[rows, :] = jnp.where(hit, rank, r_ref[rows, :])
        top_ref[r] = best
        return carry

    lax.fori_loop(0, P_TOPK, body, 0)


def _sort_desc(vals):
    a = list(vals)
    n = len(a)
    k = 2
    while k <= n:
        j = k // 2
        while j >= 1:
            for i in range(n):
                l = i ^ j
                if l > i:
                    hi, lo = jnp.maximum(a[i], a[l]), jnp.minimum(a[i], a[l])
                    a[i], a[l] = (hi, lo) if (i & k) == 0 else (lo, hi)
            j //= 2
        k *= 2
    return a


def _merge_top(top, grp):
    n = len(top)
    a = [jnp.maximum(top[i], grp[n - 1 - i]) for i in range(n)]
    j = n // 2
    while j >= 1:
        for i in range(n):
            l = i ^ j
            if l > i:
                a[i], a[l] = jnp.maximum(a[i], a[l]), jnp.minimum(a[i], a[l])
        j //= 2
    return a


def _top_values(read):
    top = None
    for k0 in range(0, N_KEYS, P_TOPK):
        grp = _sort_desc([read(k) for k in range(k0, k0 + P_TOPK)])
        top = grp if top is None else _merge_top(top, grp)
    return top


def _tie_flags(read, top):
    flags = _tree(jnp.add, [jnp.where(top[i] > top[i + 1], 0.0, 1.0) for i in range(P_TOPK - 1)])
    reach = _tree(jnp.add, [jnp.where(read(k) >= top[P_TOPK - 1], 1.0, 0.0) for k in range(N_KEYS)])
    return flags + jnp.where(reach == float(P_TOPK), 0.0, 1.0)


def _peer_topk_kernel(x_ref, wq_ref, sk_ref, l1_ref, w1_ref, r2_ref, e2_ref,
                      s_scr, sc_scr, r1_scr, r2_scr, top_scr):
    ph = P_HEADS
    n_tiles = s_scr.shape[0]
    xb = x_ref[...].astype(BF16)
    qt = lax.dot_general(wq_ref[...], xb, (((1,), (1,)), ((), ())), preferred_element_type=F32)
    for p in range(2):
        for h in range(ph):
            q_hp = qt[p * _HK + h * N_KEYS:p * _HK + (h + 1) * N_KEYS].astype(BF16)
            sc = jnp.dot(sk_ref[h * 2 + p], q_hp, preferred_element_type=F32)
            for lt in range(n_tiles):
                tile = s_scr.at[lt, p]
                tile[pl.ds(h, N_KEYS, stride=ph), :] = sc[:, lt * LANES:(lt + 1) * LANES]

    def tile_body(lt, carry):
        _peer_select_tile(lt, s_scr, sc_scr, r1_scr, r2_scr, top_scr,
                          l1_ref, w1_ref, r2_ref, e2_ref)
        return carry

    lax.fori_loop(0, n_tiles, tile_body, 0)


def _peer_select_tile(lt, s_scr, sc_scr, r1_scr, r2_scr, top_scr, l1_ref, w1_ref, r2_ref, e2_ref):
    ph = P_HEADS

    def key_rows(k):
        return slice(k * ph, (k + 1) * ph)

    def read(p):
        return lambda k: s_scr[lt, p, key_rows(k), :]

    a = _top_values(read(0))
    b = _top_values(read(1))
    undecided = jnp.max(_tie_flags(read(0), a) + _tie_flags(read(1), b))
    cand = {ij: a[ij[0]] + b[ij[1]] for ij in _CANDIDATES}
    pos = {ij: float(ij[0] * P_TOPK + ij[1]) for ij in _CANDIDATES}
    work = dict(cand)
    tau = tau_pos = None
    for r in range(P_TOPK):
        tau = _tree(jnp.maximum, [work[ij] for ij in _CANDIDATES])
        tau_pos = _tree(jnp.minimum,
                        [jnp.where(work[ij] == tau, pos[ij], _BIG_POS) for ij in _CANDIDATES])
        if r < P_TOPK - 1:
            for ij in _CANDIDATES:
                work[ij] = jnp.where(tau_pos == pos[ij], -jnp.inf, work[ij])
    ea = [jnp.exp(a[i] - a[0]) for i in range(P_TOPK)]
    eb = [jnp.exp(b[j] - b[0]) for j in range(P_TOPK)]
    count = [jnp.zeros((ph, LANES), F32) for _ in range(P_TOPK)]
    z = jnp.zeros((ph, LANES), F32)
    for ij in _CANDIDATES:
        i, j = ij
        chosen = jnp.where(cand[ij] > tau, 1.0,
                           jnp.where(cand[ij] == tau,
                                     jnp.where(tau_pos >= pos[ij], 1.0, 0.0), 0.0))
        count[i] = count[i] + chosen
        z = z + chosen * (ea[i] * eb[j])
    inv_z = 1.0 / z

    @pl.when(undecided == 0.0)
    def _():
        for k in range(N_KEYS):
            rows = key_rows(k)
            s1 = s_scr[lt, 0, rows, :]
            reach = jnp.zeros((ph, LANES), F32)
            for i in range(P_TOPK):
                reach = jnp.where(s1 == a[i], count[i], reach)
            l1_ref[lt, rows, :] = _twice_bf16(reach)
            s2 = s_scr[lt, 1, rows, :]
            r2_scr[rows, :] = _tree(jnp.add, [jnp.where(b[j] > s2, 1.0, 0.0)
                                              for j in range(P_TOPK)])

    @pl.when(undecided != 0.0)
    def _():
        sc_scr[...] = s_scr[lt, 0]
        _extract_top(sc_scr, r1_scr, top_scr)
        sc_scr[...] = s_scr[lt, 1]
        _extract_top(sc_scr, r2_scr, top_scr)
        for k in range(N_KEYS):
            rows = key_rows(k)
            r1 = r1_scr[rows, :]
            reach = jnp.zeros((ph, LANES), F32)
            for i in range(P_TOPK):
                reach = jnp.where(r1 == float(i), count[i], reach)
            l1_ref[lt, rows, :] = _twice_bf16(reach)

    half_inv_z = 0.5 * inv_z
    for k in range(N_KEYS):
        rows = key_rows(k)
        w1_ref[lt, rows, :] = _twice_bf16(jnp.exp(s_scr[lt, 0, rows, :] - a[0]) * half_inv_z)
        sc_scr[rows, :] = jnp.exp(s_scr[lt, 1, rows, :] - b[0])
    for h in range(ph):
        r2_ref[lt, h] = _pack_bf16(r2_scr[pl.ds(h, N_KEYS, stride=ph), :])
        e2_ref[lt, h] = _pack_bf16(sc_scr[pl.ds(h, N_KEYS, stride=ph), :])


def _peer_topk(x, wq_t, subkeys):
    t = x.shape[0]
    nt = t // LANES
    tm = PEER_TOPK_TOKENS if t % PEER_TOPK_TOKENS == 0 else LANES
    n_tiles = tm // LANES
    tile = pl.BlockSpec((n_tiles, _HK, LANES), lambda i: (i, 0, 0))
    tile3 = pl.BlockSpec((n_tiles, P_HEADS, N_KEYS // 2, LANES), lambda i: (i, 0, 0, 0))
    flat = jax.ShapeDtypeStruct((nt, _HK, LANES), jnp.uint32)
    cube = jax.ShapeDtypeStruct((nt, P_HEADS, N_KEYS // 2, LANES), jnp.uint32)
    return pl.pallas_call(
        _peer_topk_kernel,
        grid=(t // tm,),
        in_specs=[pl.BlockSpec((tm, D_MODEL), lambda i: (i, 0)),
                  pl.BlockSpec((2 * _HK, D_MODEL), lambda i: (0, 0)),
                  pl.BlockSpec((2 * P_HEADS, N_KEYS, D_KEY // 2), lambda i: (0, 0, 0))],
        out_specs=[tile, tile, tile3, tile3],
        out_shape=[flat, flat, cube, cube],
        scratch_shapes=[pltpu.VMEM((n_tiles, 2, _HK, LANES), F32),
                        pltpu.VMEM((_HK, LANES), F32),
                        pltpu.VMEM((_HK, LANES), F32),
                        pltpu.VMEM((_HK, LANES), F32),
                        pltpu.VMEM((P_TOPK, P_HEADS, LANES), F32)],
        compiler_params=_params("parallel"),
        name="peer_topk",
    )(x, wq_t, subkeys)


def _replicated_bf16(ref, tile, row):
    return _unpack_bf16(ref[tile, pl.ds(row, SUBLANES, stride=0), :])


def _peer_mix_kernel(x_ref, u_ref, vt_ref, l1_ref, w1_ref, r2_ref, e2_ref, g_ref, b_ref,
                     out_ref, xt_scr, acc_scr, act_scr, p_scr):
    e = pl.program_id(1)
    n_groups, _, group = xt_scr.shape

    @pl.when(e == 0)
    def _():
        for gi in range(n_groups):
            xt_scr[gi] = x_ref[gi * group:(gi + 1) * group, :].T.astype(BF16)
        acc_scr[...] = jnp.zeros(acc_scr.shape, F32)

    def project(gi):
        act_scr[gi] = jnp.dot(_unpack_bf16(u_ref[...]), xt_scr[gi], preferred_element_type=F32)

    def mix(gi):
        acc_scr[gi] += jnp.dot(_unpack_bf16(vt_ref[...]), p_scr[gi], preferred_element_type=F32)

    lookahead = min(2, n_groups)
    for gi in range(lookahead):
        project(gi)
    for gi in range(n_groups):
        for c0 in range(0, group, LANES):
            cols = slice(c0, c0 + LANES)
            ct = (gi * group + c0) // LANES
            for kb in range(0, N_KEYS, PEER_KEY_BLOCK):
                subs = range(kb, kb + PEER_KEY_BLOCK, BF16_ROWS)
                gates = {(jj, k0): jnp.zeros((BF16_ROWS, LANES), BF16)
                         for jj in range(PEER_ROWS_PER_STEP) for k0 in subs}
                for h in range(P_HEADS):
                    rows_h = [jj * P_HEADS + h for jj in range(PEER_ROWS_PER_STEP)]
                    reach = [_replicated_bf16(l1_ref, ct, r) for r in rows_h]
                    weight = [_replicated_bf16(w1_ref, ct, r) for r in rows_h]
                    for k0 in subs:
                        words = slice(k0 // 2, k0 // 2 + SUBLANES)
                        r2 = _unpack_bf16(r2_ref[ct, h, words, :])
                        e2 = _unpack_bf16(e2_ref[ct, h, words, :])
                        for jj in range(PEER_ROWS_PER_STEP):
                            picked = jnp.where(r2 < reach[jj], e2, jnp.zeros_like(e2))
                            gates[jj, k0] = gates[jj, k0] + picked * weight[jj]
                for jj in range(PEER_ROWS_PER_STEP):
                    for k0 in subs:
                        rows = slice(jj * N_KEYS + k0, jj * N_KEYS + k0 + BF16_ROWS)
                        a = act_scr[gi, rows, cols]
                        gelu = a * (1.0 + lax.erf(a * (2.0 ** -0.5)))
                        p_scr[gi, rows, cols] = gelu.astype(BF16) * gates[jj, k0]
        if gi >= 1:
            mix(gi - 1)
        if gi + lookahead < n_groups:
            project(gi + lookahead)
    mix(n_groups - 1)

    @pl.when(e == pl.num_programs(1) - 1)
    def _():
        for gi in range(n_groups):
            rows = slice(gi * group, (gi + 1) * group)
            y = DN_ALPHA * x_ref[rows, :] + acc_scr[gi].T
            out_ref[rows, :] = _layer_norm(y, g_ref[...], b_ref[...])


def _peer_mix(x, u, vt, l1, w1, r2, e2, g, b, tm):
    t = x.shape[0]
    te = PEER_ROWS_PER_STEP * N_KEYS
    tr = PEER_ROWS_PER_STEP * P_HEADS
    nt = tm // LANES
    group = min(tm, PEER_COLUMN_GROUP)
    n_groups = tm // group
    return pl.pallas_call(
        _peer_mix_kernel,
        grid=(t // tm, N_EXPERTS // te),
        in_specs=[pl.BlockSpec((tm, D_MODEL), lambda i, e: (i, 0)),
                  pl.BlockSpec((te // 2, D_MODEL), lambda i, e: (e, 0)),
                  pl.BlockSpec((D_MODEL // 2, te), lambda i, e: (0, e)),
                  pl.BlockSpec((nt, tr, LANES), lambda i, e: (i, e, 0)),
                  pl.BlockSpec((nt, tr, LANES), lambda i, e: (i, e, 0)),
                  pl.BlockSpec((nt, P_HEADS, N_KEYS // 2, LANES), lambda i, e: (i, 0, 0, 0)),
                  pl.BlockSpec((nt, P_HEADS, N_KEYS // 2, LANES), lambda i, e: (i, 0, 0, 0)),
                  pl.BlockSpec((1, D_MODEL), lambda i, e: (0, 0)),
                  pl.BlockSpec((1, D_MODEL), lambda i, e: (0, 0))],
        out_specs=pl.BlockSpec((tm, D_MODEL), lambda i, e: (i, 0)),
        out_shape=jax.ShapeDtypeStruct((t, D_MODEL), F32),
        scratch_shapes=[pltpu.VMEM((n_groups, D_MODEL, group), BF16),
                        pltpu.VMEM((n_groups, D_MODEL, group), F32),
                        pltpu.VMEM((n_groups, te, group), F32),
                        pltpu.VMEM((n_groups, te, group), BF16)],
        compiler_params=_params("parallel", "arbitrary"),
        name="peer_mix",
    )(x, u, vt, l1, w1, r2, e2, g, b)


EXPERT_PREP_ROWS = 512


def _expert_prep_kernel(u_ref, v_ref, uw_ref, vtw_ref):
    uw_ref[...] = _pack_bf16(u_ref[...])
    vtw_ref[...] = _pack_bf16(v_ref[...].T)


def _expert_prep(u, v):
    n, d = u.shape
    rows = EXPERT_PREP_ROWS
    return pl.pallas_call(
        _expert_prep_kernel,
        grid=(n // rows,),
        in_specs=[pl.BlockSpec((rows, d), lambda i: (i, 0)),
                  pl.BlockSpec((rows, d), lambda i: (i, 0))],
        out_specs=[pl.BlockSpec((rows // 2, d), lambda i: (i, 0)),
                   pl.BlockSpec((d // 2, rows), lambda i: (0, i))],
        out_shape=[jax.ShapeDtypeStruct((n // 2, d), jnp.uint32),
                   jax.ShapeDtypeStruct((d // 2, n), jnp.uint32)],
        compiler_params=_params("parallel"),
        name="expert_prep",
    )(u, v)


def _peer_layer(x, pw, g, b):
    t = x.shape[0]
    l1, w1, r2, e2 = _peer_topk(x, pw["wq_t"], pw["subkeys"])
    tm = PEER_TOKEN_TILE if t % PEER_TOKEN_TILE == 0 else LANES
    return _peer_mix(x, pw["u"], pw["vt"], l1, w1, r2, e2, g, b, tm)


def _t5_bucket(dist):
    max_exact = N_BUCKETS // 2
    d = jnp.maximum(dist, 0)
    df = jnp.maximum(d, 1).astype(F32)
    large = max_exact + (jnp.log(df / max_exact) / math.log(MAX_DISTANCE / max_exact)
                         * (N_BUCKETS - max_exact)).astype(jnp.int32)
    large = jnp.minimum(large, N_BUCKETS - 1)
    return jnp.where(d < max_exact, d, large)


def _row(a):
    return a.reshape(1, -1)


def _prepare(w):
    n_gate = 2 * A_HEADS
    w_in = w["a_w_in"][0]
    b_in = w["a_b_in"][0]
    prep = {
        "w_in_main": w_in[:, :2 * A_INNER].astype(BF16),
        "b_in_main": _row(b_in[:2 * A_INNER]),
        "w_in_gate": jnp.pad(w_in[:, 2 * A_INNER:], ((0, 0), (0, LANES - n_gate))).astype(BF16),
        "b_in_gate": _row(jnp.pad(b_in[2 * A_INNER:], (0, LANES - n_gate))),
        "conv_w": w["a_conv_w"][0],
        "conv_b": _row(w["a_conv_b"][0]),
        "wq": w["a_w_q"][0].astype(BF16),
        "wk": w["a_w_k"][0].astype(BF16),
        "wv": w["a_w_v"][0].astype(BF16),
        "w_out": w["a_w_out"][0].astype(BF16),
        "w_qkv": jnp.concatenate([w["b_w_q"][0], w["kv_w"]], axis=1).astype(BF16),
        "w_o": w["b_w_o"][0].astype(BF16),
        "sinks": _row(w["b_sinks"][0]),
        "rel_bias": w["rel_bias"],
    }
    peer = []
    half = D_KEY // 2
    for layer in range(DEPTH):
        wq = w["peer_w_q"][layer].reshape(D_MODEL, P_HEADS, 2, half)
        sk = w["peer_subkeys"][layer]
        peer.append({
            "wq_t": wq.transpose(2, 1, 3, 0).reshape(2 * _HK, D_MODEL).astype(BF16),
            "subkeys": sk.reshape(2 * P_HEADS, N_KEYS, half).astype(BF16),
        })
        peer[-1]["u"], peer[-1]["vt"] = _expert_prep(w["peer_u"][layer], w["peer_v"][layer])
    prep["peer"] = peer
    return prep


def _zero_bias(n):
    return jnp.zeros((1, n), F32)


def _attention_tables(rel_bias):
    qi = jnp.arange(WINDOW)[:, None]
    kj = jnp.arange(2 * WINDOW)[None, :]
    bucket = _t5_bucket(qi + WINDOW - kj).astype(jnp.int32)
    return _bias_tables(bucket, rel_bias)


def _prompt_trunk(x, w, p, bias):
    batch, length, d = x.shape
    t = batch * length
    xt = x.reshape(t, d)
    proj = _linear(xt, p["w_in_main"], p["b_in_main"], 512, LINEAR_COLUMNS)
    gates = _linear(xt, p["w_in_gate"], p["b_in_gate"], 512, LANES)
    row_tab, col_tab = _gate_tables(gates, batch, length, MLSTM_CHUNK)
    q, k, v = _conv_qkv(proj, p["conv_w"], p["conv_b"], p["wq"], p["wk"], p["wv"], batch, length)
    h, c_new, n_new, m_new = _mlstm_prompt(q, k, v, row_tab, col_tab, batch, length, MLSTM_CHUNK)
    x1 = _outproj_ln(h, proj, p["w_out"], xt, _row(w["ln_mix_g"][0]), _row(w["ln_mix_b"][0]), 256)
    x2 = _peer_layer(x1, p["peer"][0], _row(w["ln_ffn_g"][0]), _row(w["ln_ffn_b"][0]))
    qkv = _linear(x2, p["w_qkv"], _zero_bias(p["w_qkv"].shape[1]), 512, p["w_qkv"].shape[1])
    o = _swa_prompt(qkv, bias, p["sinks"], batch, length)
    x3 = _outproj_ln(o, None, p["w_o"], x2, _row(w["ln_mix_g"][1]), _row(w["ln_mix_b"][1]), 256)
    x4 = _peer_layer(x3, p["peer"][1], _row(w["ln_ffn_g"][1]), _row(w["ln_ffn_b"][1]))

    kvw = B_KV_HEADS * B_HEAD_DIM
    qkv3 = qkv.reshape(batch, length, -1)
    k_win = qkv3[:, -WINDOW:, D_MODEL:D_MODEL + kvw].reshape(batch, WINDOW, B_KV_HEADS, B_HEAD_DIM)
    v_win = qkv3[:, -WINDOW:, D_MODEL + kvw:].reshape(batch, WINDOW, B_KV_HEADS, B_HEAD_DIM)
    conv = proj.reshape(batch, length, -1)[:, -(A_CONV_W - 1):, :A_INNER]
    return (x4.reshape(batch, length, d),
            c_new[None],
            n_new.reshape(1, batch, A_HEADS, A_HEAD_DIM),
            m_new[:, :, 0, 0][None],
            conv[None], k_win, v_win)


def _sample_trunk(x, conv0, c0, n0, m0, k_buf, v_buf, w, p, bias):
    batch, length, d = x.shape
    xt = x.reshape(batch, d)
    proj = _linear(xt, p["w_in_main"], p["b_in_main"], batch, 512)
    gates = _linear(xt, p["w_in_gate"], p["b_in_gate"], batch, LANES)
    buf = conv0[0]
    q, k, v, qt, kt = _sample_conv_qkv(proj, buf.reshape(batch, -1), p["conv_w"], p["conv_b"],
                                       p["wq"], p["wk"], p["wv"])
    h, c_new, n_new, m_new = _sample_mlstm(q, k, v, qt, kt, gates, m0[0], n0[0], c0[0])
    x1 = _outproj_ln(h.reshape(batch, A_INNER), proj, p["w_out"], xt,
                     _row(w["ln_mix_g"][0]), _row(w["ln_mix_b"][0]), batch)
    x2 = _peer_layer(x1, p["peer"][0], _row(w["ln_ffn_g"][0]), _row(w["ln_ffn_b"][0]))
    qkv = _linear(x2, p["w_qkv"], _zero_bias(p["w_qkv"].shape[1]), batch, 512)
    kvw = B_KV_HEADS * B_HEAD_DIM
    o = _swa_decode(qkv, k_buf.reshape(batch, WINDOW, kvw), v_buf.reshape(batch, WINDOW, kvw),
                    bias[:, 0, :], p["sinks"].reshape(B_HEADS, 1))
    x3 = _outproj_ln(o, None, p["w_o"], x2, _row(w["ln_mix_g"][1]), _row(w["ln_mix_b"][1]), batch)
    x4 = _peer_layer(x3, p["peer"][1], _row(w["ln_ffn_g"][1]), _row(w["ln_ffn_b"][1]))

    k_new = qkv[:, D_MODEL:D_MODEL + kvw].reshape(batch, 1, B_KV_HEADS, B_HEAD_DIM)
    v_new = qkv[:, D_MODEL + kvw:].reshape(batch, 1, B_KV_HEADS, B_HEAD_DIM)
    k_win = jnp.concatenate([k_buf[:, 1:], k_new], axis=1)
    v_win = jnp.concatenate([v_buf[:, 1:], v_new], axis=1)
    conv = jnp.concatenate([buf[:, 1:], proj[:, None, :A_INNER]], axis=1)
    return (x4.reshape(batch, length, d),
            c_new[None],
            n_new.reshape(1, batch, A_HEADS, A_HEAD_DIM),
            m_new[:, :, 0, 0][None],
            conv[None], k_win, v_win)


def kernel(x_prompt, x_sample, state_mlstm_C, state_mlstm_n, state_mlstm_m, state_mlstm_conv,
           cache_k_win, cache_v_win, a_w_in, a_b_in, a_conv_w, a_conv_b, a_w_q, a_w_k, a_w_v,
           a_w_out, kv_w, b_w_q, b_w_o, b_sinks, rel_bias, ln_mix_g, ln_mix_b, ln_ffn_g,
           ln_ffn_b, peer_w_q, peer_subkeys, peer_u, peer_v):
    w = {"a_w_in": a_w_in, "a_b_in": a_b_in, "a_conv_w": a_conv_w, "a_conv_b": a_conv_b,
         "a_w_q": a_w_q, "a_w_k": a_w_k, "a_w_v": a_w_v, "a_w_out": a_w_out, "kv_w": kv_w,
         "b_w_q": b_w_q, "b_w_o": b_w_o, "b_sinks": b_sinks, "rel_bias": rel_bias,
         "ln_mix_g": ln_mix_g, "ln_mix_b": ln_mix_b, "ln_ffn_g": ln_ffn_g, "ln_ffn_b": ln_ffn_b,
         "peer_w_q": peer_w_q, "peer_subkeys": peer_subkeys, "peer_u": peer_u, "peer_v": peer_v}
    p = _prepare(w)
    bias = _attention_tables(rel_bias)
    prompt = _prompt_trunk(x_prompt, w, p, bias)
    sample = _sample_trunk(x_sample, state_mlstm_conv, state_mlstm_C, state_mlstm_n,
                           state_mlstm_m, cache_k_win, cache_v_win, w, p, bias)
    return (prompt[0], sample[0]) + prompt[1:] + sample[1:]
```

```python
import functools
import math

import jax
import jax.numpy as jnp
from jax import lax
from jax.experimental import pallas as pl
from jax.experimental.pallas import tpu as pltpu

D_MODEL = 1024
DEPTH = 2
A_HEADS = 4
A_INNER = 2 * D_MODEL
A_HEAD_DIM = A_INNER // A_HEADS
A_CONV_W = 4
B_HEADS = 16
B_KV_HEADS = 4
B_GROUP = B_HEADS // B_KV_HEADS
B_HEAD_DIM = D_MODEL // B_HEADS
WINDOW = 128
N_BUCKETS = 32
MAX_DISTANCE = 128
P_HEADS = 8
N_KEYS = 128
N_EXPERTS = N_KEYS * N_KEYS
D_KEY = 256
P_TOPK = 16
DN_ALPHA = (2 * DEPTH) ** 0.25
LN_EPS = 1e-5
NEG_INF = -1e30

LANES = 128
SUBLANES = 8
BF16_ROWS = 2 * SUBLANES
VMEM_LIMIT_BYTES = 56 * 1024 * 1024

LINEAR_COLUMNS = 2048
MLSTM_CHUNK = 256
PEER_TOPK_TOKENS = 256
PEER_TOKEN_TILE = 1024
PEER_ROWS_PER_STEP = 8
PEER_COLUMN_GROUP = 256
PEER_KEY_BLOCK = 16
NOT_SELECTED_RANK = float(P_TOPK)

F32 = jnp.float32
BF16 = jnp.bfloat16


def _params(*sem):
    return pltpu.CompilerParams(dimension_semantics=sem, vmem_limit_bytes=VMEM_LIMIT_BYTES)


def _tree(op, vals):
    vals = list(vals)
    while len(vals) > 1:
        nxt = [op(vals[i], vals[i + 1]) for i in range(0, len(vals) - 1, 2)]
        if len(vals) % 2:
            nxt.append(vals[-1])
        vals = nxt
    return vals[0]


def _layer_norm(y, g, b):
    mu = jnp.mean(y, axis=-1, keepdims=True)
    yc = y - mu
    var = jnp.mean(yc * yc, axis=-1, keepdims=True)
    return yc * lax.rsqrt(var + LN_EPS) * g + b


def _linear_kernel(x_ref, w_ref, b_ref, o_ref):
    x = x_ref[...].astype(BF16)
    o_ref[...] = jnp.dot(x, w_ref[...], preferred_element_type=F32) + b_ref[...]


def _linear(x, w, b, tm, tn):
    t, k = x.shape
    n = w.shape[1]
    return pl.pallas_call(
        _linear_kernel,
        grid=(t // tm, n // tn),
        in_specs=[pl.BlockSpec((tm, k), lambda i, j: (i, 0)),
                  pl.BlockSpec((k, tn), lambda i, j: (0, j)),
                  pl.BlockSpec((1, tn), lambda i, j: (0, j))],
        out_specs=pl.BlockSpec((tm, tn), lambda i, j: (i, j)),
        out_shape=jax.ShapeDtypeStruct((t, n), F32),
        compiler_params=_params("parallel", "arbitrary"),
        name="linear",
    )(x, w, b)


def _outproj_ln_kernel(gated, *refs):
    if gated:
        a_ref, o_ref, w_ref, res_ref, g_ref, b_ref, out_ref = refs
        act = jax.nn.sigmoid(o_ref[...]) * a_ref[...]
    else:
        a_ref, w_ref, res_ref, g_ref, b_ref, out_ref = refs
        act = a_ref[...]
    sub = jnp.dot(act.astype(BF16), w_ref[...], preferred_element_type=F32)
    out_ref[...] = _layer_norm(DN_ALPHA * res_ref[...] + sub, g_ref[...], b_ref[...])


def _outproj_ln(act, gate_src, w, res, g, b, tm):
    t, k = act.shape
    d = w.shape[1]
    gated = gate_src is not None
    in_specs = [pl.BlockSpec((tm, k), lambda i: (i, 0))]
    args = [act]
    if gated:
        in_specs.append(pl.BlockSpec((tm, k), lambda i: (i, 1)))
        args.append(gate_src)
    in_specs += [pl.BlockSpec((k, d), lambda i: (0, 0)),
                 pl.BlockSpec((tm, d), lambda i: (i, 0)),
                 pl.BlockSpec((1, d), lambda i: (0, 0)),
                 pl.BlockSpec((1, d), lambda i: (0, 0))]
    args += [w, res, g, b]
    return pl.pallas_call(
        functools.partial(_outproj_ln_kernel, gated),
        grid=(t // tm,),
        in_specs=in_specs,
        out_specs=pl.BlockSpec((tm, d), lambda i: (i, 0)),
        out_shape=jax.ShapeDtypeStruct((t, d), F32),
        compiler_params=_params("parallel"),
        name="outproj_ln",
    )(*args)


def _log_sigmoid(x):
    return jnp.minimum(x, 0.0) - jnp.log1p(jnp.exp(-jnp.abs(x)))


def _gates_kernel(chunk, g_ref, row_ref, col_ref):
    length = g_ref.shape[0]
    gt = g_ref[...].T
    top = gt[0:SUBLANES]
    row = lax.broadcasted_iota(jnp.int32, top.shape, 0)
    pos = lax.broadcasted_iota(jnp.int32, top.shape, 1) % chunk
    is_f = row >= A_HEADS
    x = jnp.where(is_f, _log_sigmoid(top), 0.0)
    shift = 1
    while shift < chunk:
        x = x + jnp.where(pos >= shift, pltpu.roll(x, shift, 1), 0.0)
        shift *= 2
    table = jnp.where(is_f, x, top)
    row_ref[...] = table
    padded = jnp.concatenate([table, jnp.zeros((LANES - SUBLANES, length), F32)], axis=0)
    col_ref[...] = padded.T


def _gate_tables(gates, batch, length, chunk):
    return pl.pallas_call(
        functools.partial(_gates_kernel, chunk),
        grid=(batch,),
        in_specs=[pl.BlockSpec((length, LANES), lambda b: (b, 0))],
        out_specs=[pl.BlockSpec((None, SUBLANES, length), lambda b: (b, 0, 0)),
                   pl.BlockSpec((length, LANES), lambda b: (b, 0))],
        out_shape=[jax.ShapeDtypeStruct((batch, SUBLANES, length), F32),
                   jax.ShapeDtypeStruct((batch * length, LANES), F32)],
        compiler_params=_params("parallel"),
        name="gate_tables",
    )(gates)


CONV_ROW_CHUNK = 512
CONV_PAD = SUBLANES


def _conv_qkv_kernel(xm_ref, cw_ref, cb_ref, wq_ref, wk_ref, wv_ref,
                     q_ref, k_ref, v_ref, pad_ref):
    length = xm_ref.shape[0]
    pad_ref[0:CONV_PAD, :] = jnp.zeros((CONV_PAD, A_HEAD_DIM), F32)
    pad_ref[CONV_PAD:CONV_PAD + length, :] = xm_ref[...]
    first = CONV_PAD - (A_CONV_W - 1)
    for c0 in range(0, length, CONV_ROW_CHUNK):
        acc = cb_ref[...]
        for w in range(A_CONV_W):
            acc = acc + pad_ref[c0 + first + w:c0 + first + w + CONV_ROW_CHUNK, :] * cw_ref[w:w + 1, :]
        xc = (acc * jax.nn.sigmoid(acc)).astype(BF16)
        xm = xm_ref[c0:c0 + CONV_ROW_CHUNK, :].astype(BF16)
        rows = slice(c0, c0 + CONV_ROW_CHUNK)
        q_ref[rows, :] = jnp.dot(xc, wq_ref[...], preferred_element_type=F32).astype(BF16)
        k = jnp.dot(xc, wk_ref[...], preferred_element_type=F32) * (A_HEAD_DIM ** -0.5)
        k_ref[rows, :] = k.astype(BF16)
        v_ref[rows, :] = jnp.dot(xm, wv_ref[...], preferred_element_type=F32).astype(BF16)


def _conv_qkv(proj, cw, cb, wq, wk, wv, batch, length):
    t = batch * length
    hd = A_HEAD_DIM
    tok = pl.BlockSpec((length, hd), lambda b, h: (b, h))
    wspec = pl.BlockSpec((None, hd, hd), lambda b, h: (h, 0, 0))
    out = jax.ShapeDtypeStruct((t, A_INNER), BF16)
    return pl.pallas_call(
        _conv_qkv_kernel,
        grid=(batch, A_HEADS),
        in_specs=[tok,
                  pl.BlockSpec((A_CONV_W, hd), lambda b, h: (0, h)),
                  pl.BlockSpec((1, hd), lambda b, h: (0, h)),
                  wspec, wspec, wspec],
        out_specs=[tok, tok, tok],
        out_shape=[out, out, out],
        scratch_shapes=[pltpu.VMEM((length + CONV_PAD, hd), F32)],
        compiler_params=_params("parallel", "arbitrary"),
        name="conv_qkv",
    )(proj, cw, cb, wq, wk, wv)


def _mlstm_kernel(q_ref, k_ref, v_ref, row_ref, col_ref,
                  h_ref, c_out_ref, n_out_ref, m_out_ref,
                  c_scr, n_scr, m_scr):
    ci = pl.program_id(1)
    chunk = q_ref.shape[0]
    hd = A_HEAD_DIM

    @pl.when(ci == 0)
    def _():
        c_scr[...] = jnp.zeros(c_scr.shape, F32)
        n_scr[...] = jnp.zeros(n_scr.shape, F32)
        m_scr[...] = jnp.zeros(m_scr.shape, F32)

    t_idx = lax.broadcasted_iota(jnp.int32, (chunk, chunk), 0)
    s_idx = lax.broadcasted_iota(jnp.int32, (chunk, chunk), 1)
    causal = s_idx <= t_idx
    for h in range(A_HEADS):
        cols = slice(h * hd, (h + 1) * hd)
        qh, kh, vh = q_ref[:, cols], k_ref[:, cols], v_ref[:, cols]
        i_col = col_ref[:, h:h + 1]
        f_col = col_ref[:, A_HEADS + h:A_HEADS + h + 1]
        i_row = row_ref[h:h + 1, :]
        f_row = row_ref[A_HEADS + h:A_HEADS + h + 1, :]
        m_prev = m_scr[h][:, 0:1]
        d = jnp.where(causal, f_col - f_row + i_row, NEG_INF)
        b_inter = f_col + m_prev
        m_t = jnp.maximum(b_inter, jnp.max(d, axis=1, keepdims=True))
        qk = lax.dot_general(qh, kh, (((1,), (1,)), ((), ())), preferred_element_type=F32)
        s = qk * jnp.exp(d - m_t)
        w_inter = jnp.exp(b_inter - m_t)
        q_c = jnp.dot(qh, c_scr[h].astype(BF16), preferred_element_type=F32)
        num = jnp.dot(s.astype(BF16), vh, preferred_element_type=F32) + w_inter * q_c
        q_n = jnp.sum(qh.astype(F32) * n_scr[h], axis=1, keepdims=True)
        den = jnp.sum(s, axis=1, keepdims=True) + w_inter * q_n
        h_ref[:, cols] = num / jnp.maximum(jnp.abs(den), jnp.exp(-m_t))
        f_last = f_col[chunk - 1:chunk, :]
        g = f_last - f_col + i_col
        m_new = jnp.maximum(f_last + m_prev, jnp.max(g, axis=0, keepdims=True))
        decay = jnp.exp(f_last + m_prev - m_new)
        wk = jnp.exp(g - m_new) * kh.astype(F32)
        kv = lax.dot_general(wk.astype(BF16), vh, (((0,), (0,)), ((), ())),
                             preferred_element_type=F32)
        c_scr[h] = decay * c_scr[h] + kv
        n_scr[h] = decay * n_scr[h] + jnp.sum(wk, axis=0, keepdims=True)
        m_scr[h] = jnp.broadcast_to(m_new, (1, LANES))

    @pl.when(ci == pl.num_programs(1) - 1)
    def _():
        c_out_ref[...] = c_scr[...]
        n_out_ref[...] = n_scr[...]
        m_out_ref[...] = m_scr[...]


def _mlstm_prompt(q, k, v, row_tab, col_tab, batch, length, chunk):
    nc = length // chunk
    t = batch * length
    tok = pl.BlockSpec((chunk, A_INNER), lambda b, c: (b * nc + c, 0))
    return pl.pallas_call(
        _mlstm_kernel,
        grid=(batch, nc),
        in_specs=[tok, tok, tok,
                  pl.BlockSpec((None, SUBLANES, chunk), lambda b, c: (b, 0, c)),
                  pl.BlockSpec((chunk, LANES), lambda b, c: (b * nc + c, 0))],
        out_specs=[tok,
                   pl.BlockSpec((None, A_HEADS, A_HEAD_DIM, A_HEAD_DIM), lambda b, c: (b, 0, 0, 0)),
                   pl.BlockSpec((None, A_HEADS, 1, A_HEAD_DIM), lambda b, c: (b, 0, 0, 0)),
                   pl.BlockSpec((None, A_HEADS, 1, LANES), lambda b, c: (b, 0, 0, 0))],
        out_shape=[jax.ShapeDtypeStruct((t, A_INNER), F32),
                   jax.ShapeDtypeStruct((batch, A_HEADS, A_HEAD_DIM, A_HEAD_DIM), F32),
                   jax.ShapeDtypeStruct((batch, A_HEADS, 1, A_HEAD_DIM), F32),
                   jax.ShapeDtypeStruct((batch, A_HEADS, 1, LANES), F32)],
        scratch_shapes=[pltpu.VMEM((A_HEADS, A_HEAD_DIM, A_HEAD_DIM), F32),
                        pltpu.VMEM((A_HEADS, 1, A_HEAD_DIM), F32),
                        pltpu.VMEM((A_HEADS, 1, LANES), F32)],
        compiler_params=_params("parallel", "arbitrary"),
        name="mlstm_prompt",
    )(q, k, v, row_tab, col_tab)


def _sample_conv_qkv_kernel(proj_ref, buf_ref, cw_ref, cb_ref, wq_ref, wk_ref, wv_ref,
                            q_ref, k_ref, v_ref, qt_ref, kt_ref):
    xm = proj_ref[:, 0:A_INNER]
    acc = cb_ref[...] + xm * cw_ref[A_CONV_W - 1:A_CONV_W, :]
    for w in range(A_CONV_W - 1):
        acc = acc + buf_ref[:, w * A_INNER:(w + 1) * A_INNER] * cw_ref[w:w + 1, :]
    xc = (acc * jax.nn.sigmoid(acc)).astype(BF16)
    xmb = xm.astype(BF16)
    for h in range(A_HEADS):
        cols = slice(h * A_HEAD_DIM, (h + 1) * A_HEAD_DIM)
        q_ref[:, cols] = jnp.dot(xc[:, cols], wq_ref[h], preferred_element_type=F32)
        k_ref[:, cols] = (jnp.dot(xc[:, cols], wk_ref[h], preferred_element_type=F32)
                          * (A_HEAD_DIM ** -0.5))
        v_ref[:, cols] = jnp.dot(xmb[:, cols], wv_ref[h], preferred_element_type=F32)
    qt_ref[...] = q_ref[...].T
    kt_ref[...] = k_ref[...].T


def _sample_conv_qkv(proj, conv_buf_flat, cw, cb, wq, wk, wv):
    b = proj.shape[0]
    row = jax.ShapeDtypeStruct((b, A_INNER), F32)
    col = jax.ShapeDtypeStruct((A_INNER, b), F32)
    return pl.pallas_call(
        _sample_conv_qkv_kernel,
        out_shape=[row, row, row, col, col],
        compiler_params=pltpu.CompilerParams(vmem_limit_bytes=VMEM_LIMIT_BYTES),
        name="sample_conv_qkv",
    )(proj, conv_buf_flat, cw, cb, wq, wk, wv)


def _sample_mlstm_kernel(q_ref, k_ref, v_ref, qt_ref, kt_ref, g_ref, m0_ref, n0_ref, c0_ref,
                         h_ref, c_ref, n_ref, m_ref):
    b = pl.program_id(0)
    hd = A_HEAD_DIM
    is_b = lax.broadcasted_iota(jnp.int32, (hd, LANES), 1) == b
    gates = g_ref[...]
    m0_all = m0_ref[...]
    for h in range(A_HEADS):
        cols = slice(h * hd, (h + 1) * hd)
        q_col = jnp.sum(jnp.where(is_b, qt_ref[cols, :], 0.0), axis=1, keepdims=True)
        k_col = jnp.sum(jnp.where(is_b, kt_ref[cols, :], 0.0), axis=1, keepdims=True)
        q_mat = jnp.broadcast_to(q_col, (hd, hd))
        k_mat = jnp.broadcast_to(k_col, (hd, hd))
        q_row, k_row, v_row = q_ref[:, cols], k_ref[:, cols], v_ref[:, cols]
        log_i = gates[:, h:h + 1]
        log_f = _log_sigmoid(gates[:, A_HEADS + h:A_HEADS + h + 1])
        m0 = m0_all[:, h:h + 1]
        m_t = jnp.maximum(log_f + m0, log_i)
        w_inter = jnp.exp(log_f + m0 - m_t)
        w_new = jnp.exp(log_i - m_t)
        c0 = c0_ref[h]
        n0 = n0_ref[h]
        s = jnp.sum(q_row * k_row, axis=1, keepdims=True) * w_new
        q_c = jnp.sum(q_mat * c0, axis=0, keepdims=True)
        q_n = jnp.sum(q_row * n0, axis=1, keepdims=True)
        num = s * v_row + w_inter * q_c
        den = s + w_inter * q_n
        h_ref[:, cols] = num / jnp.maximum(jnp.abs(den), jnp.exp(-m_t))
        c_ref[h] = w_inter * c0 + (w_new * k_mat) * v_row
        n_ref[h] = w_inter * n0 + w_new * k_row
        m_ref[h] = jnp.broadcast_to(m_t, (1, LANES))


def _sample_mlstm(q, k, v, qt, kt, gates, m0, n0, c0):
    b = q.shape[0]
    hd = A_HEAD_DIM
    row3 = lambda a: a.reshape(b, 1, a.shape[-1])
    rspec = pl.BlockSpec((None, 1, A_INNER), lambda i: (i, 0, 0))
    cspec = pl.BlockSpec((A_INNER, LANES), lambda i: (0, 0))
    nspec = pl.BlockSpec((None, A_HEADS, 1, hd), lambda i: (i, 0, 0, 0))
    mspec = pl.BlockSpec((None, A_HEADS, 1, LANES), lambda i: (i, 0, 0, 0))
    big = pl.BlockSpec((None, A_HEADS, hd, hd), lambda i: (i, 0, 0, 0))
    return pl.pallas_call(
        _sample_mlstm_kernel,
        grid=(b,),
        in_specs=[rspec, rspec, rspec, cspec, cspec,
                  pl.BlockSpec((None, 1, LANES), lambda i: (i, 0, 0)),
                  pl.BlockSpec((None, 1, A_HEADS), lambda i: (i, 0, 0)),
                  nspec, big],
        out_specs=[rspec, big, nspec, mspec],
        out_shape=[jax.ShapeDtypeStruct((b, 1, A_INNER), F32),
                   jax.ShapeDtypeStruct((b, A_HEADS, hd, hd), F32),
                   jax.ShapeDtypeStruct((b, A_HEADS, 1, hd), F32),
                   jax.ShapeDtypeStruct((b, A_HEADS, 1, LANES), F32)],
        compiler_params=_params("parallel"),
        name="sample_mlstm",
    )(row3(q), row3(k), row3(v), qt, kt, row3(gates), row3(m0),
      n0.reshape(b, A_HEADS, 1, hd), c0)


def _bias_kernel(bucket_ref, rel_ref, bias_ref):
    h = pl.program_id(0)
    bucket = bucket_ref[...]
    acc = jnp.zeros(bucket.shape, F32)
    for n in range(N_BUCKETS):
        acc = jnp.where(bucket == n, rel_ref[n, h], acc)
    bias_ref[...] = acc


def _bias_tables(bucket, rel_bias):
    w, w2 = bucket.shape
    return pl.pallas_call(
        _bias_kernel,
        grid=(B_HEADS,),
        in_specs=[pl.BlockSpec((w, w2), lambda h: (0, 0)),
                  pl.BlockSpec(memory_space=pltpu.SMEM)],
        out_specs=pl.BlockSpec((None, w, w2), lambda h: (h, 0, 0)),
        out_shape=jax.ShapeDtypeStruct((B_HEADS, w, w2), F32),
        compiler_params=_params("arbitrary"),
        name="bias_tables",
    )(bucket, rel_bias)


def _swa_prompt_kernel(q_ref, kp_ref, kc_ref, vp_ref, vc_ref, bias_ref, sink_ref, o_ref):
    n = pl.program_id(1)
    w = WINDOW
    q = q_ref[...].astype(BF16)
    kk = jnp.concatenate([kp_ref[...], kc_ref[...]], axis=0).astype(BF16)
    vv = jnp.concatenate([vp_ref[...], vc_ref[...]], axis=0).astype(BF16)
    qi = lax.broadcasted_iota(jnp.int32, (w, 2 * w), 0)
    kj = lax.broadcasted_iota(jnp.int32, (w, 2 * w), 1)
    dist = qi + w - kj
    valid = (dist >= 0) & (dist < w) & ((kj >= w) | (n > 0))
    for h in range(B_HEADS):
        kvh = h // B_GROUP
        qh = q[:, h * B_HEAD_DIM:(h + 1) * B_HEAD_DIM]
        kh = kk[:, kvh * B_HEAD_DIM:(kvh + 1) * B_HEAD_DIM]
        vh = vv[:, kvh * B_HEAD_DIM:(kvh + 1) * B_HEAD_DIM]
        s = lax.dot_general(qh, kh, (((1,), (1,)), ((), ())), preferred_element_type=F32)
        s = jnp.where(valid, s * (B_HEAD_DIM ** -0.5) + bias_ref[h], NEG_INF)
        sink = sink_ref[0, h]
        m = jnp.maximum(jnp.max(s, axis=1, keepdims=True), sink)
        p = jnp.exp(s - m)
        den = jnp.sum(p, axis=1, keepdims=True) + jnp.exp(sink - m)
        p = (p / den).astype(BF16)
        o_ref[:, h * B_HEAD_DIM:(h + 1) * B_HEAD_DIM] = jnp.dot(p, vh, preferred_element_type=F32)


def _swa_prompt(qkv, bias, sinks, batch, length):
    nb = length // WINDOW
    kvw = B_KV_HEADS * B_HEAD_DIM
    kcol = D_MODEL // kvw
    cur = lambda b, n: b * nb + n
    prev = lambda b, n: b * nb + jnp.maximum(n - 1, 0)
    return pl.pallas_call(
        _swa_prompt_kernel,
        grid=(batch, nb),
        in_specs=[pl.BlockSpec((WINDOW, D_MODEL), lambda b, n: (cur(b, n), 0)),
                  pl.BlockSpec((WINDOW, kvw), lambda b, n: (prev(b, n), kcol)),
                  pl.BlockSpec((WINDOW, kvw), lambda b, n: (cur(b, n), kcol)),
                  pl.BlockSpec((WINDOW, kvw), lambda b, n: (prev(b, n), kcol + 1)),
                  pl.BlockSpec((WINDOW, kvw), lambda b, n: (cur(b, n), kcol + 1)),
                  pl.BlockSpec((B_HEADS, WINDOW, 2 * WINDOW), lambda b, n: (0, 0, 0)),
                  pl.BlockSpec(memory_space=pltpu.SMEM)],
        out_specs=pl.BlockSpec((WINDOW, D_MODEL), lambda b, n: (cur(b, n), 0)),
        out_shape=jax.ShapeDtypeStruct((batch * length, D_MODEL), F32),
        compiler_params=_params("parallel", "arbitrary"),
        name="swa_prompt",
    )(qkv, qkv, qkv, qkv, qkv, bias, sinks)


DECODE_BATCH_BLOCK = 8


def _swa_decode_kernel(qkv_ref, ck_ref, cv_ref, bias_ref, sink_ref, o_ref):
    w = WINDOW
    hd = B_HEAD_DIM
    kvw = B_KV_HEADS * hd
    scale = hd ** -0.5
    pos = lax.broadcasted_iota(jnp.int32, (B_GROUP, w), 1)
    for bi in range(DECODE_BATCH_BLOCK):
        row = qkv_ref[bi:bi + 1, :]
        for kvh in range(B_KV_HEADS):
            heads = slice(kvh * B_GROUP, (kvh + 1) * B_GROUP)
            kv_cols = slice(kvh * hd, (kvh + 1) * hd)
            q = jnp.concatenate([row[:, h * hd:(h + 1) * hd]
                                 for h in range(kvh * B_GROUP, (kvh + 1) * B_GROUP)], axis=0)
            k_new = row[:, D_MODEL + kvh * hd:D_MODEL + (kvh + 1) * hd]
            v_new = row[:, D_MODEL + kvw + kvh * hd:D_MODEL + kvw + (kvh + 1) * hd]
            kc = ck_ref[bi, :, kv_cols].astype(BF16)
            vc = cv_ref[bi, :, kv_cols].astype(BF16)
            s_c = lax.dot_general(q.astype(BF16), kc, (((1,), (1,)), ((), ())),
                                  preferred_element_type=F32)
            s_c = jnp.where(pos >= 1, s_c * scale + bias_ref[heads, 0:w], NEG_INF)
            s_n = jnp.sum(q * k_new, axis=1, keepdims=True) * scale + bias_ref[heads, w:w + 1]
            sink = sink_ref[heads, :]
            m = jnp.maximum(jnp.maximum(jnp.max(s_c, axis=1, keepdims=True), s_n), sink)
            p_c = jnp.exp(s_c - m)
            p_n = jnp.exp(s_n - m)
            den = jnp.sum(p_c, axis=1, keepdims=True) + p_n + jnp.exp(sink - m)
            o = (jnp.dot(p_c.astype(BF16), vc, preferred_element_type=F32) + p_n * v_new) / den
            for g in range(B_GROUP):
                h = kvh * B_GROUP + g
                o_ref[bi:bi + 1, h * hd:(h + 1) * hd] = o[g:g + 1, :]


def _swa_decode(qkv, cache_k, cache_v, bias_row, sinks_col):
    b = qkv.shape[0]
    kvw = B_KV_HEADS * B_HEAD_DIM
    bb = DECODE_BATCH_BLOCK
    return pl.pallas_call(
        _swa_decode_kernel,
        grid=(b // bb,),
        in_specs=[pl.BlockSpec((bb, qkv.shape[1]), lambda i: (i, 0)),
                  pl.BlockSpec((bb, WINDOW, kvw), lambda i: (i, 0, 0)),
                  pl.BlockSpec((bb, WINDOW, kvw), lambda i: (i, 0, 0)),
                  pl.BlockSpec((B_HEADS, 2 * WINDOW), lambda i: (0, 0)),
                  pl.BlockSpec((B_HEADS, 1), lambda i: (0, 0))],
        out_specs=pl.BlockSpec((bb, D_MODEL), lambda i: (i, 0)),
        out_shape=jax.ShapeDtypeStruct((b, D_MODEL), F32),
        compiler_params=_params("parallel"),
        name="swa_decode",
    )(qkv, cache_k, cache_v, bias_row, sinks_col)


_CANDIDATES = [(i, j) for i in range(P_TOPK) for j in range(P_TOPK)
               if (i + 1) * (j + 1) <= P_TOPK]
_BIG_POS = float(4 * P_TOPK * P_TOPK)
_HK = P_HEADS * N_KEYS


def _pack_bf16(x):
    return pltpu.bitcast(x.astype(BF16), jnp.uint32)


def _unpack_bf16(words):
    return pltpu.bitcast(words, BF16)


def _twice_bf16(x):
    high = pltpu.bitcast(x.astype(BF16).astype(F32), jnp.uint32)
    return high | (high >> 16)


def _extract_top(s_ref, r_ref, top_ref):
    ph = P_HEADS
    for k in range(N_KEYS):
        r_ref[k * ph:(k + 1) * ph, :] = jnp.full((ph, LANES), NOT_SELECTED_RANK, F32)

    def body(r, carry):
        best = _tree(jnp.maximum, [s_ref[k * ph:(k + 1) * ph, :] for k in range(N_KEYS)])
        first = _tree(jnp.minimum,
                      [jnp.where(s_ref[k * ph:(k + 1) * ph, :] == best, float(k), float(N_KEYS))
                       for k in range(N_KEYS)])
        rank = lax.convert_element_type(r, F32)
        for k in range(N_KEYS):
            rows = slice(k * ph, (k + 1) * ph)
            hit = first == float(k)
            s_ref[rows, :] = jnp.where(hit, -jnp.inf, s_ref[rows, :])
            r_ref[rows, :] = jnp.where(hit, rank, r_ref[rows, :])
        top_ref[r] = best
        return carry

    lax.fori_loop(0, P_TOPK, body, 0)


def _sort_desc(vals):
    a = list(vals)
    n = len(a)
    k = 2
    while k <= n:
        j = k // 2
        while j >= 1:
            for i in range(n):
                l = i ^ j
                if l > i:
                    hi, lo = jnp.maximum(a[i], a[l]), jnp.minimum(a[i], a[l])
                    a[i], a[l] = (hi, lo) if (i & k) == 0 else (lo, hi)
            j //= 2
        k *= 2
    return a


def _merge_top(top, grp):
    n = len(top)
    a = [jnp.maximum(top[i], grp[n - 1 - i]) for i in range(n)]
    j = n // 2
    while j >= 1:
        for i in range(n):
            l = i ^ j
            if l > i:
                a[i], a[l] = jnp.maximum(a[i], a[l]), jnp.minimum(a[i], a[l])
        j //= 2
    return a


def _top_values(read):
    top = None
    for k0 in range(0, N_KEYS, P_TOPK):
        grp = _sort_desc([read(k) for k in range(k0, k0 + P_TOPK)])
        top = grp if top is None else _merge_top(top, grp)
    return top


def _tie_flags(read, top):
    flags = _tree(jnp.add, [jnp.where(top[i] > top[i + 1], 0.0, 1.0) for i in range(P_TOPK - 1)])
    reach = _tree(jnp.add, [jnp.where(read(k) >= top[P_TOPK - 1], 1.0, 0.0) for k in range(N_KEYS)])
    return flags + jnp.where(reach == float(P_TOPK), 0.0, 1.0)


def _peer_topk_kernel(x_ref, wq_ref, sk_ref, l1_ref, w1_ref, r2_ref, e2_ref,
                      s_scr, sc_scr, r1_scr, r2_scr, top_scr):
    ph = P_HEADS
    n_tiles = s_scr.shape[0]
    xb = x_ref[...].astype(BF16)
    qt = lax.dot_general(wq_ref[...], xb, (((1,), (1,)), ((), ())), preferred_element_type=F32)
    for p in range(2):
        for h in range(ph):
            q_hp = qt[p * _HK + h * N_KEYS:p * _HK + (h + 1) * N_KEYS].astype(BF16)
            sc = jnp.dot(sk_ref[h * 2 + p], q_hp, preferred_element_type=F32)
            for lt in range(n_tiles):
                tile = s_scr.at[lt, p]
                tile[pl.ds(h, N_KEYS, stride=ph), :] = sc[:, lt * LANES:(lt + 1) * LANES]

    def tile_body(lt, carry):
        _peer_select_tile(lt, s_scr, sc_scr, r1_scr, r2_scr, top_scr,
                          l1_ref, w1_ref, r2_ref, e2_ref)
        return carry

    lax.fori_loop(0, n_tiles, tile_body, 0)


def _peer_select_tile(lt, s_scr, sc_scr, r1_scr, r2_scr, top_scr, l1_ref, w1_ref, r2_ref, e2_ref):
    ph = P_HEADS

    def key_rows(k):
        return slice(k * ph, (k + 1) * ph)

    def read(p):
        return lambda k: s_scr[lt, p, key_rows(k), :]

    a = _top_values(read(0))
    b = _top_values(read(1))
    undecided = jnp.max(_tie_flags(read(0), a) + _tie_flags(read(1), b))
    cand = {ij: a[ij[0]] + b[ij[1]] for ij in _CANDIDATES}
    pos = {ij: float(ij[0] * P_TOPK + ij[1]) for ij in _CANDIDATES}
    work = dict(cand)
    tau = tau_pos = None
    for r in range(P_TOPK):
        tau = _tree(jnp.maximum, [work[ij] for ij in _CANDIDATES])
        tau_pos = _tree(jnp.minimum,
                        [jnp.where(work[ij] == tau, pos[ij], _BIG_POS) for ij in _CANDIDATES])
        if r < P_TOPK - 1:
            for ij in _CANDIDATES:
                work[ij] = jnp.where(tau_pos == pos[ij], -jnp.inf, work[ij])
    ea = [jnp.exp(a[i] - a[0]) for i in range(P_TOPK)]
    eb = [jnp.exp(b[j] - b[0]) for j in range(P_TOPK)]
    count = [jnp.zeros((ph, LANES), F32) for _ in range(P_TOPK)]
    z = jnp.zeros((ph, LANES), F32)
    for ij in _CANDIDATES:
        i, j = ij
        chosen = jnp.where(cand[ij] > tau, 1.0,
                           jnp.where(cand[ij] == tau,
                                     jnp.where(tau_pos >= pos[ij], 1.0, 0.0), 0.0))
        count[i] = count[i] + chosen
        z = z + chosen * (ea[i] * eb[j])
    inv_z = 1.0 / z

    @pl.when(undecided == 0.0)
    def _():
        for k in range(N_KEYS):
            rows = key_rows(k)
            s1 = s_scr[lt, 0, rows, :]
            reach = jnp.zeros((ph, LANES), F32)
            for i in range(P_TOPK):
                reach = jnp.where(s1 == a[i], count[i], reach)
            l1_ref[lt, rows, :] = _twice_bf16(reach)
            s2 = s_scr[lt, 1, rows, :]
            r2_scr[rows, :] = _tree(jnp.add, [jnp.where(b[j] > s2, 1.0, 0.0)
                                              for j in range(P_TOPK)])

    @pl.when(undecided != 0.0)
    def _():
        sc_scr[...] = s_scr[lt, 0]
        _extract_top(sc_scr, r1_scr, top_scr)
        sc_scr[...] = s_scr[lt, 1]
        _extract_top(sc_scr, r2_scr, top_scr)
        for k in range(N_KEYS):
            rows = key_rows(k)
            r1 = r1_scr[rows, :]
            reach = jnp.zeros((ph, LANES), F32)
            for i in range(P_TOPK):
                reach = jnp.where(r1 == float(i), count[i], reach)
            l1_ref[lt, rows, :] = _twice_bf16(reach)

    half_inv_z = 0.5 * inv_z
    for k in range(N_KEYS):
        rows = key_rows(k)
        w1_ref[lt, rows, :] = _twice_bf16(jnp.exp(s_scr[lt, 0, rows, :] - a[0]) * half_inv_z)
        sc_scr[rows, :] = jnp.exp(s_scr[lt, 1, rows, :] - b[0])
    for h in range(ph):
        r2_ref[lt, h] = _pack_bf16(r2_scr[pl.ds(h, N_KEYS, stride=ph), :])
        e2_ref[lt, h] = _pack_bf16(sc_scr[pl.ds(h, N_KEYS, stride=ph), :])


def _peer_topk(x, wq_t, subkeys):
    t = x.shape[0]
    nt = t // LANES
    tm = PEER_TOPK_TOKENS if t % PEER_TOPK_TOKENS == 0 else LANES
    n_tiles = tm // LANES
    tile = pl.BlockSpec((n_tiles, _HK, LANES), lambda i: (i, 0, 0))
    tile3 = pl.BlockSpec((n_tiles, P_HEADS, N_KEYS // 2, LANES), lambda i: (i, 0, 0, 0))
    flat = jax.ShapeDtypeStruct((nt, _HK, LANES), jnp.uint32)
    cube = jax.ShapeDtypeStruct((nt, P_HEADS, N_KEYS // 2, LANES), jnp.uint32)
    return pl.pallas_call(
        _peer_topk_kernel,
        grid=(t // tm,),
        in_specs=[pl.BlockSpec((tm, D_MODEL), lambda i: (i, 0)),
                  pl.BlockSpec((2 * _HK, D_MODEL), lambda i: (0, 0)),
                  pl.BlockSpec((2 * P_HEADS, N_KEYS, D_KEY // 2), lambda i: (0, 0, 0))],
        out_specs=[tile, tile, tile3, tile3],
        out_shape=[flat, flat, cube, cube],
        scratch_shapes=[pltpu.VMEM((n_tiles, 2, _HK, LANES), F32),
                        pltpu.VMEM((_HK, LANES), F32),
                        pltpu.VMEM((_HK, LANES), F32),
                        pltpu.VMEM((_HK, LANES), F32),
                        pltpu.VMEM((P_TOPK, P_HEADS, LANES), F32)],
        compiler_params=_params("parallel"),
        name="peer_topk",
    )(x, wq_t, subkeys)


def _replicated_bf16(ref, tile, row):
    return _unpack_bf16(ref[tile, pl.ds(row, SUBLANES, stride=0), :])


def _peer_mix_kernel(x_ref, u_ref, u_next_ref, vt_ref, l1_ref, w1_ref, r2_ref, e2_ref, g_ref, b_ref,
                     out_ref, xt_scr, acc_scr, act_scr, p_scr):
    e = pl.program_id(1)
    n_groups, _, group = xt_scr.shape
    lookahead = min(2, n_groups)

    def project(gi, experts_ref):
        act_scr[gi] = jnp.dot(_unpack_bf16(experts_ref[...]), xt_scr[gi],
                              preferred_element_type=F32)

    def mix(gi):
        acc_scr[gi] += jnp.dot(_unpack_bf16(vt_ref[...]), p_scr[gi], preferred_element_type=F32)

    @pl.when(e == 0)
    def _():
        for gi in range(n_groups):
            xt_scr[gi] = x_ref[gi * group:(gi + 1) * group, :].T.astype(BF16)
        acc_scr[...] = jnp.zeros(acc_scr.shape, F32)
        for gi in range(lookahead):
            project(gi, u_ref)

    for gi in range(n_groups):
        for c0 in range(0, group, LANES):
            cols = slice(c0, c0 + LANES)
            ct = (gi * group + c0) // LANES
            for kb in range(0, N_KEYS, PEER_KEY_BLOCK):
                subs = range(kb, kb + PEER_KEY_BLOCK, BF16_ROWS)
                gates = {(jj, k0): jnp.zeros((BF16_ROWS, LANES), BF16)
                         for jj in range(PEER_ROWS_PER_STEP) for k0 in subs}
                for h in range(P_HEADS):
                    rows_h = [jj * P_HEADS + h for jj in range(PEER_ROWS_PER_STEP)]
                    reach = [_replicated_bf16(l1_ref, ct, r) for r in rows_h]
                    weight = [_replicated_bf16(w1_ref, ct, r) for r in rows_h]
                    for k0 in subs:
                        words = slice(k0 // 2, k0 // 2 + SUBLANES)
                        r2 = _unpack_bf16(r2_ref[ct, h, words, :])
                        e2 = _unpack_bf16(e2_ref[ct, h, words, :])
                        for jj in range(PEER_ROWS_PER_STEP):
                            picked = jnp.where(r2 < reach[jj], e2, jnp.zeros_like(e2))
                            gates[jj, k0] = gates[jj, k0] + picked * weight[jj]
                for jj in range(PEER_ROWS_PER_STEP):
                    for k0 in subs:
                        rows = slice(jj * N_KEYS + k0, jj * N_KEYS + k0 + BF16_ROWS)
                        a = act_scr[gi, rows, cols]
                        gelu = a * (1.0 + lax.erf(a * (2.0 ** -0.5)))
                        p_scr[gi, rows, cols] = gelu.astype(BF16) * gates[jj, k0]
        if gi >= 1:
            mix(gi - 1)
        ahead = gi + lookahead
        if ahead < n_groups:
            project(ahead, u_ref)
        else:
            project(ahead - n_groups, u_next_ref)
    mix(n_groups - 1)

    @pl.when(e == pl.num_programs(1) - 1)
    def _():
        for gi in range(n_groups):
            rows = slice(gi * group, (gi + 1) * group)
            y = DN_ALPHA * x_ref[rows, :] + acc_scr[gi].T
            out_ref[rows, :] = _layer_norm(y, g_ref[...], b_ref[...])


def _peer_mix(x, u, vt, l1, w1, r2, e2, g, b, tm):
    t = x.shape[0]
    te = PEER_ROWS_PER_STEP * N_KEYS
    tr = PEER_ROWS_PER_STEP * P_HEADS
    nt = tm // LANES
    group = min(tm, PEER_COLUMN_GROUP)
    n_groups = tm // group
    n_steps = N_EXPERTS // te
    return pl.pallas_call(
        _peer_mix_kernel,
        grid=(t // tm, n_steps),
        in_specs=[pl.BlockSpec((tm, D_MODEL), lambda i, e: (i, 0)),
                  pl.BlockSpec((te // 2, D_MODEL), lambda i, e: (e, 0)),
                  pl.BlockSpec((te // 2, D_MODEL), lambda i, e: (jnp.minimum(e + 1, n_steps - 1), 0)),
                  pl.BlockSpec((D_MODEL // 2, te), lambda i, e: (0, e)),
                  pl.BlockSpec((nt, tr, LANES), lambda i, e: (i, e, 0)),
                  pl.BlockSpec((nt, tr, LANES), lambda i, e: (i, e, 0)),
                  pl.BlockSpec((nt, P_HEADS, N_KEYS // 2, LANES), lambda i, e: (i, 0, 0, 0)),
                  pl.BlockSpec((nt, P_HEADS, N_KEYS // 2, LANES), lambda i, e: (i, 0, 0, 0)),
                  pl.BlockSpec((1, D_MODEL), lambda i, e: (0, 0)),
                  pl.BlockSpec((1, D_MODEL), lambda i, e: (0, 0))],
        out_specs=pl.BlockSpec((tm, D_MODEL), lambda i, e: (i, 0)),
        out_shape=jax.ShapeDtypeStruct((t, D_MODEL), F32),
        scratch_shapes=[pltpu.VMEM((n_groups, D_MODEL, group), BF16),
                        pltpu.VMEM((n_groups, D_MODEL, group), F32),
                        pltpu.VMEM((n_groups, te, group), F32),
                        pltpu.VMEM((n_groups, te, group), BF16)],
        compiler_params=_params("parallel", "arbitrary"),
        name="peer_mix",
    )(x, u, u, vt, l1, w1, r2, e2, g, b)


EXPERT_PREP_ROWS = 512


def _expert_prep_kernel(u_ref, v_ref, uw_ref, vtw_ref):
    uw_ref[...] = _pack_bf16(u_ref[...])
    vtw_ref[...] = _pack_bf16(v_ref[...].T)


def _expert_prep(u, v, layer):
    _, n, d = u.shape
    rows = EXPERT_PREP_ROWS
    return pl.pallas_call(
        _expert_prep_kernel,
        grid=(n // rows,),
        in_specs=[pl.BlockSpec((None, rows, d), lambda i: (layer, i, 0)),
                  pl.BlockSpec((None, rows, d), lambda i: (layer, i, 0))],
        out_specs=[pl.BlockSpec((rows // 2, d), lambda i: (i, 0)),
                   pl.BlockSpec((d // 2, rows), lambda i: (0, i))],
        out_shape=[jax.ShapeDtypeStruct((n // 2, d), jnp.uint32),
                   jax.ShapeDtypeStruct((d // 2, n), jnp.uint32)],
        compiler_params=_params("parallel"),
        name="expert_prep",
    )(u, v)


def _peer_layer(x, pw, g, b):
    t = x.shape[0]
    l1, w1, r2, e2 = _peer_topk(x, pw["wq_t"], pw["subkeys"])
    tm = PEER_TOKEN_TILE if t % PEER_TOKEN_TILE == 0 else LANES
    return _peer_mix(x, pw["u"], pw["vt"], l1, w1, r2, e2, g, b, tm)


def _t5_bucket(dist):
    max_exact = N_BUCKETS // 2
    d = jnp.maximum(dist, 0)
    df = jnp.maximum(d, 1).astype(F32)
    large = max_exact + (jnp.log(df / max_exact) / math.log(MAX_DISTANCE / max_exact)
                         * (N_BUCKETS - max_exact)).astype(jnp.int32)
    large = jnp.minimum(large, N_BUCKETS - 1)
    return jnp.where(d < max_exact, d, large)


def _row(a):
    return a.reshape(1, -1)


def _prepare(w):
    n_gate = 2 * A_HEADS
    w_in = w["a_w_in"][0]
    b_in = w["a_b_in"][0]
    prep = {
        "w_in_main": w_in[:, :2 * A_INNER].astype(BF16),
        "b_in_main": _row(b_in[:2 * A_INNER]),
        "w_in_gate": jnp.pad(w_in[:, 2 * A_INNER:], ((0, 0), (0, LANES - n_gate))).astype(BF16),
        "b_in_gate": _row(jnp.pad(b_in[2 * A_INNER:], (0, LANES - n_gate))),
        "conv_w": w["a_conv_w"][0],
        "conv_b": _row(w["a_conv_b"][0]),
        "wq": w["a_w_q"][0].astype(BF16),
        "wk": w["a_w_k"][0].astype(BF16),
        "wv": w["a_w_v"][0].astype(BF16),
        "w_out": w["a_w_out"][0].astype(BF16),
        "w_qkv": jnp.concatenate([w["b_w_q"][0], w["kv_w"]], axis=1).astype(BF16),
        "w_o": w["b_w_o"][0].astype(BF16),
        "sinks": _row(w["b_sinks"][0]),
        "rel_bias": w["rel_bias"],
    }
    peer = []
    half = D_KEY // 2
    for layer in range(DEPTH):
        wq = w["peer_w_q"][layer].reshape(D_MODEL, P_HEADS, 2, half)
        sk = w["peer_subkeys"][layer]
        peer.append({
            "wq_t": wq.transpose(2, 1, 3, 0).reshape(2 * _HK, D_MODEL).astype(BF16),
            "subkeys": sk.reshape(2 * P_HEADS, N_KEYS, half).astype(BF16),
        })
        peer[-1]["u"], peer[-1]["vt"] = _expert_prep(w["peer_u"], w["peer_v"], layer)
    prep["peer"] = peer
    return prep


def _zero_bias(n):
    return jnp.zeros((1, n), F32)


def _attention_tables(rel_bias):
    qi = jnp.arange(WINDOW)[:, None]
    kj = jnp.arange(2 * WINDOW)[None, :]
    bucket = _t5_bucket(qi + WINDOW - kj).astype(jnp.int32)
    return _bias_tables(bucket, rel_bias)


def _prompt_trunk(x, w, p, bias):
    batch, length, d = x.shape
    t = batch * length
    xt = x.reshape(t, d)
    proj = _linear(xt, p["w_in_main"], p["b_in_main"], 512, LINEAR_COLUMNS)
    gates = _linear(xt, p["w_in_gate"], p["b_in_gate"], 512, LANES)
    row_tab, col_tab = _gate_tables(gates, batch, length, MLSTM_CHUNK)
    q, k, v = _conv_qkv(proj, p["conv_w"], p["conv_b"], p["wq"], p["wk"], p["wv"], batch, length)
    h, c_new, n_new, m_new = _mlstm_prompt(q, k, v, row_tab, col_tab, batch, length, MLSTM_CHUNK)
    x1 = _outproj_ln(h, proj, p["w_out"], xt, _row(w["ln_mix_g"][0]), _row(w["ln_mix_b"][0]), 256)
    x2 = _peer_layer(x1, p["peer"][0], _row(w["ln_ffn_g"][0]), _row(w["ln_ffn_b"][0]))
    qkv = _linear(x2, p["w_qkv"], _zero_bias(p["w_qkv"].shape[1]), 512, p["w_qkv"].shape[1])
    o = _swa_prompt(qkv, bias, p["sinks"], batch, length)
    x3 = _outproj_ln(o, None, p["w_o"], x2, _row(w["ln_mix_g"][1]), _row(w["ln_mix_b"][1]), 512)
    x4 = _peer_layer(x3, p["peer"][1], _row(w["ln_ffn_g"][1]), _row(w["ln_ffn_b"][1]))

    kvw = B_KV_HEADS * B_HEAD_DIM
    qkv3 = qkv.reshape(batch, length, -1)
    k_win = qkv3[:, -WINDOW:, D_MODEL:D_MODEL + kvw].reshape(batch, WINDOW, B_KV_HEADS, B_HEAD_DIM)
    v_win = qkv3[:, -WINDOW:, D_MODEL + kvw:].reshape(batch, WINDOW, B_KV_HEADS, B_HEAD_DIM)
    conv = proj.reshape(batch, length, -1)[:, -(A_CONV_W - 1):, :A_INNER]
    return (x4.reshape(batch, length, d),
            c_new[None],
            n_new.reshape(1, batch, A_HEADS, A_HEAD_DIM),
            m_new[:, :, 0, 0][None],
            conv[None], k_win, v_win)


def _sample_trunk(x, conv0, c0, n0, m0, k_buf, v_buf, w, p, bias):
    batch, length, d = x.shape
    xt = x.reshape(batch, d)
    proj = _linear(xt, p["w_in_main"], p["b_in_main"], batch, 512)
    gates = _linear(xt, p["w_in_gate"], p["b_in_gate"], batch, LANES)
    buf = conv0[0]
    q, k, v, qt, kt = _sample_conv_qkv(proj, buf.reshape(batch, -1), p["conv_w"], p["conv_b"],
                                       p["wq"], p["wk"], p["wv"])
    h, c_new, n_new, m_new = _sample_mlstm(q, k, v, qt, kt, gates, m0[0], n0[0], c0[0])
    x1 = _outproj_ln(h.reshape(batch, A_INNER), proj, p["w_out"], xt,
                     _row(w["ln_mix_g"][0]), _row(w["ln_mix_b"][0]), batch)
    x2 = _peer_layer(x1, p["peer"][0], _row(w["ln_ffn_g"][0]), _row(w["ln_ffn_b"][0]))
    qkv = _linear(x2, p["w_qkv"], _zero_bias(p["w_qkv"].shape[1]), batch, 512)
    kvw = B_KV_HEADS * B_HEAD_DIM
    o = _swa_decode(qkv, k_buf.reshape(batch, WINDOW, kvw), v_buf.reshape(batch, WINDOW, kvw),
                    bias[:, 0, :], p["sinks"].reshape(B_HEADS, 1))
    x3 = _outproj_ln(o, None, p["w_o"], x2, _row(w["ln_mix_g"][1]), _row(w["ln_mix_b"][1]), batch)
    x4 = _peer_layer(x3, p["peer"][1], _row(w["ln_ffn_g"][1]), _row(w["ln_ffn_b"][1]))

    k_new = qkv[:, D_MODEL:D_MODEL + kvw].reshape(batch, 1, B_KV_HEADS, B_HEAD_DIM)
    v_new = qkv[:, D_MODEL + kvw:].reshape(batch, 1, B_KV_HEADS, B_HEAD_DIM)
    k_win = jnp.concatenate([k_buf[:, 1:], k_new], axis=1)
    v_win = jnp.concatenate([v_buf[:, 1:], v_new], axis=1)
    conv = jnp.concatenate([buf[:, 1:], proj[:, None, :A_INNER]], axis=1)
    return (x4.reshape(batch, length, d),
            c_new[None],
            n_new.reshape(1, batch, A_HEADS, A_HEAD_DIM),
            m_new[:, :, 0, 0][None],
            conv[None], k_win, v_win)


def kernel(x_prompt, x_sample, state_mlstm_C, state_mlstm_n, state_mlstm_m, state_mlstm_conv,
           cache_k_win, cache_v_win, a_w_in, a_b_in, a_conv_w, a_conv_b, a_w_q, a_w_k, a_w_v,
           a_w_out, kv_w, b_w_q, b_w_o, b_sinks, rel_bias, ln_mix_g, ln_mix_b, ln_ffn_g,
           ln_ffn_b, peer_w_q, peer_subkeys, peer_u, peer_v):
    w = {"a_w_in": a_w_in, "a_b_in": a_b_in, "a_conv_w": a_conv_w, "a_conv_b": a_conv_b,
         "a_w_q": a_w_q, "a_w_k": a_w_k, "a_w_v": a_w_v, "a_w_out": a_w_out, "kv_w": kv_w,
         "b_w_q": b_w_q, "b_w_o": b_w_o, "b_sinks": b_sinks, "rel_bias": rel_bias,
         "ln_mix_g": ln_mix_g, "ln_mix_b": ln_mix_b, "ln_ffn_g": ln_ffn_g, "ln_ffn_b": ln_ffn_b,
         "peer_w_q": peer_w_q, "peer_subkeys": peer_subkeys, "peer_u": peer_u, "peer_v": peer_v}
    p = _prepare(w)
    bias = _attention_tables(rel_bias)
    prompt = _prompt_trunk(x_prompt, w, p, bias)
    sample = _sample_trunk(x_sample, state_mlstm_conv, state_mlstm_C, state_mlstm_n,
                           state_mlstm_m, cache_k_win, cache_v_win, w, p, bias)
    return (prompt[0], sample[0]) + prompt[1:] + sample[1:]
```

```python
import functools
import math

import jax
import jax.numpy as jnp
from jax import lax
from jax.experimental import pallas as pl
from jax.experimental.pallas import tpu as pltpu

D_MODEL = 1024
DEPTH = 2
A_HEADS = 4
A_INNER = 2 * D_MODEL
A_HEAD_DIM = A_INNER // A_HEADS
A_CONV_W = 4
B_HEADS = 16
B_KV_HEADS = 4
B_GROUP = B_HEADS // B_KV_HEADS
B_HEAD_DIM = D_MODEL // B_HEADS
WINDOW = 128
N_BUCKETS = 32
MAX_DISTANCE = 128
P_HEADS = 8
N_KEYS = 128
N_EXPERTS = N_KEYS * N_KEYS
D_KEY = 256
P_TOPK = 16
DN_ALPHA = (2 * DEPTH) ** 0.25
LN_EPS = 1e-5
NEG_INF = -1e30

LANES = 128
SUBLANES = 8
BF16_ROWS = 2 * SUBLANES
VMEM_LIMIT_BYTES = 56 * 1024 * 1024

LINEAR_COLUMNS = 4096
MLSTM_CHUNK = 256
PEER_TOPK_TOKENS = 256
PEER_TOKEN_TILE = 1024
PEER_ROWS_PER_STEP = 8
PEER_COLUMN_GROUP = 256
PEER_KEY_BLOCK = 16
NOT_SELECTED_RANK = float(P_TOPK)

F32 = jnp.float32
BF16 = jnp.bfloat16


def _params(*sem):
    return pltpu.CompilerParams(dimension_semantics=sem, vmem_limit_bytes=VMEM_LIMIT_BYTES)


def _tree(op, vals):
    vals = list(vals)
    while len(vals) > 1:
        nxt = [op(vals[i], vals[i + 1]) for i in range(0, len(vals) - 1, 2)]
        if len(vals) % 2:
            nxt.append(vals[-1])
        vals = nxt
    return vals[0]


def _layer_norm(y, g, b):
    mu = jnp.mean(y, axis=-1, keepdims=True)
    yc = y - mu
    var = jnp.mean(yc * yc, axis=-1, keepdims=True)
    return yc * lax.rsqrt(var + LN_EPS) * g + b


def _linear_kernel(x_ref, w_ref, b_ref, o_ref):
    x = x_ref[...].astype(BF16)
    o_ref[...] = jnp.dot(x, w_ref[...], preferred_element_type=F32) + b_ref[...]


def _linear(x, w, b, tm, tn):
    t, k = x.shape
    n = w.shape[1]
    return pl.pallas_call(
        _linear_kernel,
        grid=(t // tm, n // tn),
        in_specs=[pl.BlockSpec((tm, k), lambda i, j: (i, 0)),
                  pl.BlockSpec((k, tn), lambda i, j: (0, j)),
                  pl.BlockSpec((1, tn), lambda i, j: (0, j))],
        out_specs=pl.BlockSpec((tm, tn), lambda i, j: (i, j)),
        out_shape=jax.ShapeDtypeStruct((t, n), F32),
        compiler_params=_params("parallel", "arbitrary"),
        name="linear",
    )(x, w, b)


def _outproj_ln_kernel(gated, *refs):
    if gated:
        a_ref, o_ref, w_ref, res_ref, g_ref, b_ref, out_ref = refs
        act = jax.nn.sigmoid(o_ref[...]) * a_ref[...]
    else:
        a_ref, w_ref, res_ref, g_ref, b_ref, out_ref = refs
        act = a_ref[...]
    sub = jnp.dot(act.astype(BF16), w_ref[...], preferred_element_type=F32)
    out_ref[...] = _layer_norm(DN_ALPHA * res_ref[...] + sub, g_ref[...], b_ref[...])


def _outproj_ln(act, gate_src, w, res, g, b, tm):
    t, k = act.shape
    d = w.shape[1]
    gated = gate_src is not None
    in_specs = [pl.BlockSpec((tm, k), lambda i: (i, 0))]
    args = [act]
    if gated:
        in_specs.append(pl.BlockSpec((tm, k), lambda i: (i, 1)))
        args.append(gate_src)
    in_specs += [pl.BlockSpec((k, d), lambda i: (0, 0)),
                 pl.BlockSpec((tm, d), lambda i: (i, 0)),
                 pl.BlockSpec((1, d), lambda i: (0, 0)),
                 pl.BlockSpec((1, d), lambda i: (0, 0))]
    args += [w, res, g, b]
    return pl.pallas_call(
        functools.partial(_outproj_ln_kernel, gated),
        grid=(t // tm,),
        in_specs=in_specs,
        out_specs=pl.BlockSpec((tm, d), lambda i: (i, 0)),
        out_shape=jax.ShapeDtypeStruct((t, d), F32),
        compiler_params=_params("parallel"),
        name="outproj_ln",
    )(*args)


def _log_sigmoid(x):
    return jnp.minimum(x, 0.0) - jnp.log1p(jnp.exp(-jnp.abs(x)))


def _gates_kernel(chunk, g_ref, row_ref, col_ref):
    length = g_ref.shape[0]
    gt = g_ref[...].T
    top = gt[0:SUBLANES]
    row = lax.broadcasted_iota(jnp.int32, top.shape, 0)
    pos = lax.broadcasted_iota(jnp.int32, top.shape, 1) % chunk
    is_f = row >= A_HEADS
    x = jnp.where(is_f, _log_sigmoid(top), 0.0)
    shift = 1
    while shift < chunk:
        x = x + jnp.where(pos >= shift, pltpu.roll(x, shift, 1), 0.0)
        shift *= 2
    table = jnp.where(is_f, x, top)
    row_ref[...] = table
    padded = jnp.concatenate([table, jnp.zeros((LANES - SUBLANES, length), F32)], axis=0)
    col_ref[...] = padded.T


def _gate_tables(gates, batch, length, chunk):
    return pl.pallas_call(
        functools.partial(_gates_kernel, chunk),
        grid=(batch,),
        in_specs=[pl.BlockSpec((length, LANES), lambda b: (b, 0))],
        out_specs=[pl.BlockSpec((None, SUBLANES, length), lambda b: (b, 0, 0)),
                   pl.BlockSpec((length, LANES), lambda b: (b, 0))],
        out_shape=[jax.ShapeDtypeStruct((batch, SUBLANES, length), F32),
                   jax.ShapeDtypeStruct((batch * length, LANES), F32)],
        compiler_params=_params("parallel"),
        name="gate_tables",
    )(gates)


CONV_ROW_CHUNK = 512
CONV_PAD = SUBLANES


def _conv_qkv_kernel(xm_ref, cw_ref, cb_ref, wq_ref, wk_ref, wv_ref,
                     q_ref, k_ref, v_ref, pad_ref):
    length = xm_ref.shape[0]
    pad_ref[0:CONV_PAD, :] = jnp.zeros((CONV_PAD, A_HEAD_DIM), F32)
    pad_ref[CONV_PAD:CONV_PAD + length, :] = xm_ref[...]
    first = CONV_PAD - (A_CONV_W - 1)
    for c0 in range(0, length, CONV_ROW_CHUNK):
        acc = cb_ref[...]
        for w in range(A_CONV_W):
            acc = acc + pad_ref[c0 + first + w:c0 + first + w + CONV_ROW_CHUNK, :] * cw_ref[w:w + 1, :]
        xc = (acc * jax.nn.sigmoid(acc)).astype(BF16)
        xm = xm_ref[c0:c0 + CONV_ROW_CHUNK, :].astype(BF16)
        rows = slice(c0, c0 + CONV_ROW_CHUNK)
        q_ref[rows, :] = jnp.dot(xc, wq_ref[...], preferred_element_type=F32).astype(BF16)
        k = jnp.dot(xc, wk_ref[...], preferred_element_type=F32) * (A_HEAD_DIM ** -0.5)
        k_ref[rows, :] = k.astype(BF16)
        v_ref[rows, :] = jnp.dot(xm, wv_ref[...], preferred_element_type=F32).astype(BF16)


def _conv_qkv(proj, cw, cb, wq, wk, wv, batch, length):
    t = batch * length
    hd = A_HEAD_DIM
    tok = pl.BlockSpec((length, hd), lambda b, h: (b, h))
    wspec = pl.BlockSpec((None, hd, hd), lambda b, h: (h, 0, 0))
    out = jax.ShapeDtypeStruct((t, A_INNER), BF16)
    return pl.pallas_call(
        _conv_qkv_kernel,
        grid=(batch, A_HEADS),
        in_specs=[tok,
                  pl.BlockSpec((A_CONV_W, hd), lambda b, h: (0, h)),
                  pl.BlockSpec((1, hd), lambda b, h: (0, h)),
                  wspec, wspec, wspec],
        out_specs=[tok, tok, tok],
        out_shape=[out, out, out],
        scratch_shapes=[pltpu.VMEM((length + CONV_PAD, hd), F32)],
        compiler_params=_params("parallel", "arbitrary"),
        name="conv_qkv",
    )(proj, cw, cb, wq, wk, wv)


def _mlstm_kernel(q_ref, k_ref, v_ref, row_ref, col_ref,
                  h_ref, c_out_ref, n_out_ref, m_out_ref,
                  c_scr, n_scr, m_scr):
    ci = pl.program_id(1)
    chunk = q_ref.shape[0]
    hd = A_HEAD_DIM

    @pl.when(ci == 0)
    def _():
        c_scr[...] = jnp.zeros(c_scr.shape, F32)
        n_scr[...] = jnp.zeros(n_scr.shape, F32)
        m_scr[...] = jnp.zeros(m_scr.shape, F32)

    t_idx = lax.broadcasted_iota(jnp.int32, (chunk, chunk), 0)
    s_idx = lax.broadcasted_iota(jnp.int32, (chunk, chunk), 1)
    causal = s_idx <= t_idx
    for h in range(A_HEADS):
        cols = slice(h * hd, (h + 1) * hd)
        qh, kh, vh = q_ref[:, cols], k_ref[:, cols], v_ref[:, cols]
        i_col = col_ref[:, h:h + 1]
        f_col = col_ref[:, A_HEADS + h:A_HEADS + h + 1]
        i_row = row_ref[h:h + 1, :]
        f_row = row_ref[A_HEADS + h:A_HEADS + h + 1, :]
        m_prev = m_scr[h][:, 0:1]
        d = jnp.where(causal, f_col - f_row + i_row, NEG_INF)
        b_inter = f_col + m_prev
        m_t = jnp.maximum(b_inter, jnp.max(d, axis=1, keepdims=True))
        qk = lax.dot_general(qh, kh, (((1,), (1,)), ((), ())), preferred_element_type=F32)
        s = qk * jnp.exp(d - m_t)
        w_inter = jnp.exp(b_inter - m_t)
        q_c = jnp.dot(qh, c_scr[h].astype(BF16), preferred_element_type=F32)
        num = jnp.dot(s.astype(BF16), vh, preferred_element_type=F32) + w_inter * q_c
        q_n = jnp.sum(qh.astype(F32) * n_scr[h], axis=1, keepdims=True)
        den = jnp.sum(s, axis=1, keepdims=True) + w_inter * q_n
        h_ref[:, cols] = num / jnp.maximum(jnp.abs(den), jnp.exp(-m_t))
        f_last = f_col[chunk - 1:chunk, :]
        g = f_last - f_col + i_col
        m_new = jnp.maximum(f_last + m_prev, jnp.max(g, axis=0, keepdims=True))
        decay = jnp.exp(f_last + m_prev - m_new)
        wk = jnp.exp(g - m_new) * kh.astype(F32)
        kv = lax.dot_general(wk.astype(BF16), vh, (((0,), (0,)), ((), ())),
                             preferred_element_type=F32)
        c_scr[h] = decay * c_scr[h] + kv
        n_scr[h] = decay * n_scr[h] + jnp.sum(wk, axis=0, keepdims=True)
        m_scr[h] = jnp.broadcast_to(m_new, (1, LANES))

    @pl.when(ci == pl.num_programs(1) - 1)
    def _():
        c_out_ref[...] = c_scr[...]
        n_out_ref[...] = n_scr[...]
        m_out_ref[...] = m_scr[...]


def _mlstm_prompt(q, k, v, row_tab, col_tab, batch, length, chunk):
    nc = length // chunk
    t = batch * length
    tok = pl.BlockSpec((chunk, A_INNER), lambda b, c: (b * nc + c, 0))
    return pl.pallas_call(
        _mlstm_kernel,
        grid=(batch, nc),
        in_specs=[tok, tok, tok,
                  pl.BlockSpec((None, SUBLANES, chunk), lambda b, c: (b, 0, c)),
                  pl.BlockSpec((chunk, LANES), lambda b, c: (b * nc + c, 0))],
        out_specs=[tok,
                   pl.BlockSpec((None, A_HEADS, A_HEAD_DIM, A_HEAD_DIM), lambda b, c: (b, 0, 0, 0)),
                   pl.BlockSpec((None, A_HEADS, 1, A_HEAD_DIM), lambda b, c: (b, 0, 0, 0)),
                   pl.BlockSpec((None, A_HEADS, 1, LANES), lambda b, c: (b, 0, 0, 0))],
        out_shape=[jax.ShapeDtypeStruct((t, A_INNER), F32),
                   jax.ShapeDtypeStruct((batch, A_HEADS, A_HEAD_DIM, A_HEAD_DIM), F32),
                   jax.ShapeDtypeStruct((batch, A_HEADS, 1, A_HEAD_DIM), F32),
                   jax.ShapeDtypeStruct((batch, A_HEADS, 1, LANES), F32)],
        scratch_shapes=[pltpu.VMEM((A_HEADS, A_HEAD_DIM, A_HEAD_DIM), F32),
                        pltpu.VMEM((A_HEADS, 1, A_HEAD_DIM), F32),
                        pltpu.VMEM((A_HEADS, 1, LANES), F32)],
        compiler_params=_params("parallel", "arbitrary"),
        name="mlstm_prompt",
    )(q, k, v, row_tab, col_tab)


def _sample_conv_qkv_kernel(proj_ref, buf_ref, cw_ref, cb_ref, wq_ref, wk_ref, wv_ref,
                            q_ref, k_ref, v_ref, qt_ref, kt_ref):
    xm = proj_ref[:, 0:A_INNER]
    acc = cb_ref[...] + xm * cw_ref[A_CONV_W - 1:A_CONV_W, :]
    for w in range(A_CONV_W - 1):
        acc = acc + buf_ref[:, w * A_INNER:(w + 1) * A_INNER] * cw_ref[w:w + 1, :]
    xc = (acc * jax.nn.sigmoid(acc)).astype(BF16)
    xmb = xm.astype(BF16)
    for h in range(A_HEADS):
        cols = slice(h * A_HEAD_DIM, (h + 1) * A_HEAD_DIM)
        q_ref[:, cols] = jnp.dot(xc[:, cols], wq_ref[h], preferred_element_type=F32)
        k_ref[:, cols] = (jnp.dot(xc[:, cols], wk_ref[h], preferred_element_type=F32)
                          * (A_HEAD_DIM ** -0.5))
        v_ref[:, cols] = jnp.dot(xmb[:, cols], wv_ref[h], preferred_element_type=F32)
    qt_ref[...] = q_ref[...].T
    kt_ref[...] = k_ref[...].T


def _sample_conv_qkv(proj, conv_buf_flat, cw, cb, wq, wk, wv):
    b = proj.shape[0]
    row = jax.ShapeDtypeStruct((b, A_INNER), F32)
    col = jax.ShapeDtypeStruct((A_INNER, b), F32)
    return pl.pallas_call(
        _sample_conv_qkv_kernel,
        out_shape=[row, row, row, col, col],
        compiler_params=pltpu.CompilerParams(vmem_limit_bytes=VMEM_LIMIT_BYTES),
        name="sample_conv_qkv",
    )(proj, conv_buf_flat, cw, cb, wq, wk, wv)


def _sample_mlstm_kernel(q_ref, k_ref, v_ref, qt_ref, kt_ref, g_ref, m0_ref, n0_ref, c0_ref,
                         h_ref, c_ref, n_ref, m_ref):
    b = pl.program_id(0)
    hd = A_HEAD_DIM
    is_b = lax.broadcasted_iota(jnp.int32, (hd, LANES), 1) == b
    gates = g_ref[...]
    m0_all = m0_ref[...]
    for h in range(A_HEADS):
        cols = slice(h * hd, (h + 1) * hd)
        q_col = jnp.sum(jnp.where(is_b, qt_ref[cols, :], 0.0), axis=1, keepdims=True)
        k_col = jnp.sum(jnp.where(is_b, kt_ref[cols, :], 0.0), axis=1, keepdims=True)
        q_mat = jnp.broadcast_to(q_col, (hd, hd))
        k_mat = jnp.broadcast_to(k_col, (hd, hd))
        q_row, k_row, v_row = q_ref[:, cols], k_ref[:, cols], v_ref[:, cols]
        log_i = gates[:, h:h + 1]
        log_f = _log_sigmoid(gates[:, A_HEADS + h:A_HEADS + h + 1])
        m0 = m0_all[:, h:h + 1]
        m_t = jnp.maximum(log_f + m0, log_i)
        w_inter = jnp.exp(log_f + m0 - m_t)
        w_new = jnp.exp(log_i - m_t)
        c0 = c0_ref[h]
        n0 = n0_ref[h]
        s = jnp.sum(q_row * k_row, axis=1, keepdims=True) * w_new
        q_c = jnp.sum(q_mat * c0, axis=0, keepdims=True)
        q_n = jnp.sum(q_row * n0, axis=1, keepdims=True)
        num = s * v_row + w_inter * q_c
        den = s + w_inter * q_n
        h_ref[:, cols] = num / jnp.maximum(jnp.abs(den), jnp.exp(-m_t))
        c_ref[h] = w_inter * c0 + (w_new * k_mat) * v_row
        n_ref[h] = w_inter * n0 + w_new * k_row
        m_ref[h] = jnp.broadcast_to(m_t, (1, LANES))


def _sample_mlstm(q, k, v, qt, kt, gates, m0, n0, c0):
    b = q.shape[0]
    hd = A_HEAD_DIM
    row3 = lambda a: a.reshape(b, 1, a.shape[-1])
    rspec = pl.BlockSpec((None, 1, A_INNER), lambda i: (i, 0, 0))
    cspec = pl.BlockSpec((A_INNER, LANES), lambda i: (0, 0))
    nspec = pl.BlockSpec((None, A_HEADS, 1, hd), lambda i: (i, 0, 0, 0))
    mspec = pl.BlockSpec((None, A_HEADS, 1, LANES), lambda i: (i, 0, 0, 0))
    big = pl.BlockSpec((None, A_HEADS, hd, hd), lambda i: (i, 0, 0, 0))
    return pl.pallas_call(
        _sample_mlstm_kernel,
        grid=(b,),
        in_specs=[rspec, rspec, rspec, cspec, cspec,
                  pl.BlockSpec((None, 1, LANES), lambda i: (i, 0, 0)),
                  pl.BlockSpec((None, 1, A_HEADS), lambda i: (i, 0, 0)),
                  nspec, big],
        out_specs=[rspec, big, nspec, mspec],
        out_shape=[jax.ShapeDtypeStruct((b, 1, A_INNER), F32),
                   jax.ShapeDtypeStruct((b, A_HEADS, hd, hd), F32),
                   jax.ShapeDtypeStruct((b, A_HEADS, 1, hd), F32),
                   jax.ShapeDtypeStruct((b, A_HEADS, 1, LANES), F32)],
        compiler_params=_params("parallel"),
        name="sample_mlstm",
    )(row3(q), row3(k), row3(v), qt, kt, row3(gates), row3(m0),
      n0.reshape(b, A_HEADS, 1, hd), c0)


def _bias_kernel(bucket_ref, rel_ref, bias_ref):
    h = pl.program_id(0)
    bucket = bucket_ref[...]
    acc = jnp.zeros(bucket.shape, F32)
    for n in range(N_BUCKETS):
        acc = jnp.where(bucket == n, rel_ref[n, h], acc)
    bias_ref[...] = acc


def _bias_tables(bucket, rel_bias):
    w, w2 = bucket.shape
    return pl.pallas_call(
        _bias_kernel,
        grid=(B_HEADS,),
        in_specs=[pl.BlockSpec((w, w2), lambda h: (0, 0)),
                  pl.BlockSpec(memory_space=pltpu.SMEM)],
        out_specs=pl.BlockSpec((None, w, w2), lambda h: (h, 0, 0)),
        out_shape=jax.ShapeDtypeStruct((B_HEADS, w, w2), F32),
        compiler_params=_params("arbitrary"),
        name="bias_tables",
    )(bucket, rel_bias)


def _swa_prompt_kernel(q_ref, kp_ref, kc_ref, vp_ref, vc_ref, bias_ref, sink_ref, o_ref):
    n = pl.program_id(1)
    w = WINDOW
    q = q_ref[...].astype(BF16)
    kk = jnp.concatenate([kp_ref[...], kc_ref[...]], axis=0).astype(BF16)
    vv = jnp.concatenate([vp_ref[...], vc_ref[...]], axis=0).astype(BF16)
    qi = lax.broadcasted_iota(jnp.int32, (w, 2 * w), 0)
    kj = lax.broadcasted_iota(jnp.int32, (w, 2 * w), 1)
    dist = qi + w - kj
    valid = (dist >= 0) & (dist < w) & ((kj >= w) | (n > 0))
    for h in range(B_HEADS):
        kvh = h // B_GROUP
        qh = q[:, h * B_HEAD_DIM:(h + 1) * B_HEAD_DIM]
        kh = kk[:, kvh * B_HEAD_DIM:(kvh + 1) * B_HEAD_DIM]
        vh = vv[:, kvh * B_HEAD_DIM:(kvh + 1) * B_HEAD_DIM]
        s = lax.dot_general(qh, kh, (((1,), (1,)), ((), ())), preferred_element_type=F32)
        s = jnp.where(valid, s * (B_HEAD_DIM ** -0.5) + bias_ref[h], NEG_INF)
        sink = sink_ref[0, h]
        m = jnp.maximum(jnp.max(s, axis=1, keepdims=True), sink)
        p = jnp.exp(s - m)
        den = jnp.sum(p, axis=1, keepdims=True) + jnp.exp(sink - m)
        p = (p / den).astype(BF16)
        o_ref[:, h * B_HEAD_DIM:(h + 1) * B_HEAD_DIM] = jnp.dot(p, vh, preferred_element_type=F32)


def _swa_prompt(qkv, bias, sinks, batch, length):
    nb = length // WINDOW
    kvw = B_KV_HEADS * B_HEAD_DIM
    kcol = D_MODEL // kvw
    cur = lambda b, n: b * nb + n
    prev = lambda b, n: b * nb + jnp.maximum(n - 1, 0)
    return pl.pallas_call(
        _swa_prompt_kernel,
        grid=(batch, nb),
        in_specs=[pl.BlockSpec((WINDOW, D_MODEL), lambda b, n: (cur(b, n), 0)),
                  pl.BlockSpec((WINDOW, kvw), lambda b, n: (prev(b, n), kcol)),
                  pl.BlockSpec((WINDOW, kvw), lambda b, n: (cur(b, n), kcol)),
                  pl.BlockSpec((WINDOW, kvw), lambda b, n: (prev(b, n), kcol + 1)),
                  pl.BlockSpec((WINDOW, kvw), lambda b, n: (cur(b, n), kcol + 1)),
                  pl.BlockSpec((B_HEADS, WINDOW, 2 * WINDOW), lambda b, n: (0, 0, 0)),
                  pl.BlockSpec(memory_space=pltpu.SMEM)],
        out_specs=pl.BlockSpec((WINDOW, D_MODEL), lambda b, n: (cur(b, n), 0)),
        out_shape=jax.ShapeDtypeStruct((batch * length, D_MODEL), F32),
        compiler_params=_params("parallel", "arbitrary"),
        name="swa_prompt",
    )(qkv, qkv, qkv, qkv, qkv, bias, sinks)


DECODE_BATCH_BLOCK = 8


def _swa_decode_kernel(qkv_ref, ck_ref, cv_ref, bias_ref, sink_ref, o_ref):
    w = WINDOW
    hd = B_HEAD_DIM
    kvw = B_KV_HEADS * hd
    scale = hd ** -0.5
    pos = lax.broadcasted_iota(jnp.int32, (B_GROUP, w), 1)
    for bi in range(DECODE_BATCH_BLOCK):
        row = qkv_ref[bi:bi + 1, :]
        for kvh in range(B_KV_HEADS):
            heads = slice(kvh * B_GROUP, (kvh + 1) * B_GROUP)
            kv_cols = slice(kvh * hd, (kvh + 1) * hd)
            q = jnp.concatenate([row[:, h * hd:(h + 1) * hd]
                                 for h in range(kvh * B_GROUP, (kvh + 1) * B_GROUP)], axis=0)
            k_new = row[:, D_MODEL + kvh * hd:D_MODEL + (kvh + 1) * hd]
            v_new = row[:, D_MODEL + kvw + kvh * hd:D_MODEL + kvw + (kvh + 1) * hd]
            kc = ck_ref[bi, :, kv_cols].astype(BF16)
            vc = cv_ref[bi, :, kv_cols].astype(BF16)
            s_c = lax.dot_general(q.astype(BF16), kc, (((1,), (1,)), ((), ())),
                                  preferred_element_type=F32)
            s_c = jnp.where(pos >= 1, s_c * scale + bias_ref[heads, 0:w], NEG_INF)
            s_n = jnp.sum(q * k_new, axis=1, keepdims=True) * scale + bias_ref[heads, w:w + 1]
            sink = sink_ref[heads, :]
            m = jnp.maximum(jnp.maximum(jnp.max(s_c, axis=1, keepdims=True), s_n), sink)
            p_c = jnp.exp(s_c - m)
            p_n = jnp.exp(s_n - m)
            den = jnp.sum(p_c, axis=1, keepdims=True) + p_n + jnp.exp(sink - m)
            o = (jnp.dot(p_c.astype(BF16), vc, preferred_element_type=F32) + p_n * v_new) / den
            for g in range(B_GROUP):
                h = kvh * B_GROUP + g
                o_ref[bi:bi + 1, h * hd:(h + 1) * hd] = o[g:g + 1, :]


def _swa_decode(qkv, cache_k, cache_v, bias_row, sinks_col):
    b = qkv.shape[0]
    kvw = B_KV_HEADS * B_HEAD_DIM
    bb = DECODE_BATCH_BLOCK
    return pl.pallas_call(
        _swa_decode_kernel,
        grid=(b // bb,),
        in_specs=[pl.BlockSpec((bb, qkv.shape[1]), lambda i: (i, 0)),
                  pl.BlockSpec((bb, WINDOW, kvw), lambda i: (i, 0, 0)),
                  pl.BlockSpec((bb, WINDOW, kvw), lambda i: (i, 0, 0)),
                  pl.BlockSpec((B_HEADS, 2 * WINDOW), lambda i: (0, 0)),
                  pl.BlockSpec((B_HEADS, 1), lambda i: (0, 0))],
        out_specs=pl.BlockSpec((bb, D_MODEL), lambda i: (i, 0)),
        out_shape=jax.ShapeDtypeStruct((b, D_MODEL), F32),
        compiler_params=_params("parallel"),
        name="swa_decode",
    )(qkv, cache_k, cache_v, bias_row, sinks_col)


_CANDIDATES = [(i, j) for i in range(P_TOPK) for j in range(P_TOPK)
               if (i + 1) * (j + 1) <= P_TOPK]
_HK = P_HEADS * N_KEYS


def _pack_bf16(x):
    return pltpu.bitcast(x.astype(BF16), jnp.uint32)


def _unpack_bf16(words):
    return pltpu.bitcast(words, BF16)


def _twice_bf16(x):
    high = pltpu.bitcast(x.astype(BF16).astype(F32), jnp.uint32)
    return high | (high >> 16)


def _extract_top(s_ref, r_ref, top_ref):
    ph = P_HEADS
    for k in range(N_KEYS):
        r_ref[k * ph:(k + 1) * ph, :] = jnp.full((ph, LANES), NOT_SELECTED_RANK, F32)

    def body(r, carry):
        best = _tree(jnp.maximum, [s_ref[k * ph:(k + 1) * ph, :] for k in range(N_KEYS)])
        first = _tree(jnp.minimum,
                      [jnp.where(s_ref[k * ph:(k + 1) * ph, :] == best, float(k), float(N_KEYS))
                       for k in range(N_KEYS)])
        rank = lax.convert_element_type(r, F32)
        for k in range(N_KEYS):
            rows = slice(k * ph, (k + 1) * ph)
            hit = first == float(k)
            s_ref[rows, :] = jnp.where(hit, -jnp.inf, s_ref[rows, :])
            r_ref[rows, :] = jnp.where(hit, rank, r_ref[rows, :])
        top_ref[r] = best
        return carry

    lax.fori_loop(0, P_TOPK, body, 0)


def _sort_desc(vals):
    a = list(vals)
    n = len(a)
    k = 2
    while k <= n:
        j = k // 2
        while j >= 1:
            for i in range(n):
                l = i ^ j
                if l > i:
                    hi, lo = jnp.maximum(a[i], a[l]), jnp.minimum(a[i], a[l])
                    a[i], a[l] = (hi, lo) if (i & k) == 0 else (lo, hi)
            j //= 2
        k *= 2
    return a


def _merge_top(top, grp):
    n = len(top)
    a = [jnp.maximum(top[i], grp[n - 1 - i]) for i in range(n)]
    j = n // 2
    while j >= 1:
        for i in range(n):
            l = i ^ j
            if l > i:
                a[i], a[l] = jnp.maximum(a[i], a[l]), jnp.minimum(a[i], a[l])
        j //= 2
    return a


def _top_values(read):
    top = None
    for k0 in range(0, N_KEYS, P_TOPK):
        grp = _sort_desc([read(k) for k in range(k0, k0 + P_TOPK)])
        top = grp if top is None else _merge_top(top, grp)
    return top


def _tie_flags(read, top):
    flags = _tree(jnp.add, [jnp.where(top[i] > top[i + 1], 0.0, 1.0) for i in range(P_TOPK - 1)])
    reach = _tree(jnp.add, [jnp.where(read(k) >= top[P_TOPK - 1], 1.0, 0.0) for k in range(N_KEYS)])
    return flags + jnp.where(reach == float(P_TOPK), 0.0, 1.0)


def _peer_topk_kernel(x_ref, wq_ref, sk_ref, l1_ref, w1_ref, r2_ref, e2_ref,
                      s_scr, sc_scr, r1_scr, r2_scr, top_scr):
    ph = P_HEADS
    n_tiles = s_scr.shape[0]
    xb = x_ref[...].astype(BF16)
    qt = lax.dot_general(wq_ref[...], xb, (((1,), (1,)), ((), ())), preferred_element_type=F32)
    for p in range(2):
        for h in range(ph):
            q_hp = qt[p * _HK + h * N_KEYS:p * _HK + (h + 1) * N_KEYS].astype(BF16)
            sc = jnp.dot(sk_ref[h * 2 + p], q_hp, preferred_element_type=F32)
            for lt in range(n_tiles):
                tile = s_scr.at[lt, p]
                tile[pl.ds(h, N_KEYS, stride=ph), :] = sc[:, lt * LANES:(lt + 1) * LANES]

    def tile_body(lt, carry):
        _peer_select_tile(lt, s_scr, sc_scr, r1_scr, r2_scr, top_scr,
                          l1_ref, w1_ref, r2_ref, e2_ref)
        return carry

    lax.fori_loop(0, n_tiles, tile_body, 0)


def _peer_select_tile(lt, s_scr, sc_scr, r1_scr, r2_scr, top_scr, l1_ref, w1_ref, r2_ref, e2_ref):
    ph = P_HEADS

    def key_rows(k):
        return slice(k * ph, (k + 1) * ph)

    def read(p):
        return lambda k: s_scr[lt, p, key_rows(k), :]

    a = _top_values(read(0))
    b = _top_values(read(1))
    undecided = jnp.max(_tie_flags(read(0), a) + _tie_flags(read(1), b))
    cand = {ij: a[ij[0]] + b[ij[1]] for ij in _CANDIDATES}
    padded = [cand[ij] for ij in _CANDIDATES]
    padded += [jnp.full((ph, LANES), -jnp.inf, F32)] * (-len(padded) % P_TOPK)
    best = None
    for c0 in range(0, len(padded), P_TOPK):
        grp = _sort_desc(padded[c0:c0 + P_TOPK])
        best = grp if best is None else _merge_top(best, grp)
    tau = best[P_TOPK - 1]
    above = {ij: jnp.where(cand[ij] > tau, 1.0, 0.0) for ij in _CANDIDATES}
    equal = {ij: jnp.where(cand[ij] == tau, 1.0, 0.0) for ij in _CANDIDATES}
    places = float(P_TOPK) - _tree(jnp.add, [above[ij] for ij in _CANDIDATES])
    ea = [jnp.exp(a[i] - a[0]) for i in range(P_TOPK)]
    eb = [jnp.exp(b[j] - b[0]) for j in range(P_TOPK)]
    count = [jnp.zeros((ph, LANES), F32) for _ in range(P_TOPK)]
    z = jnp.zeros((ph, LANES), F32)
    seen = jnp.zeros((ph, LANES), F32)
    for ij in _CANDIDATES:
        i, j = ij
        chosen = above[ij] + jnp.where(seen < places, equal[ij], 0.0)
        seen = seen + equal[ij]
        count[i] = count[i] + chosen
        z = z + chosen * (ea[i] * eb[j])
    inv_z = 1.0 / z

    @pl.when(undecided == 0.0)
    def _():
        for k in range(N_KEYS):
            rows = key_rows(k)
            s1 = s_scr[lt, 0, rows, :]
            reach = jnp.zeros((ph, LANES), F32)
            for i in range(P_TOPK):
                reach = jnp.where(s1 == a[i], count[i], reach)
            l1_ref[lt, rows, :] = _twice_bf16(reach)
            s2 = s_scr[lt, 1, rows, :]
            rank = jnp.zeros((ph, LANES), F32)
            for j in range(P_TOPK):
                rank = jnp.where(b[j] > s2, float(j + 1), rank)
            r2_scr[rows, :] = rank

    @pl.when(undecided != 0.0)
    def _():
        sc_scr[...] = s_scr[lt, 0]
        _extract_top(sc_scr, r1_scr, top_scr)
        sc_scr[...] = s_scr[lt, 1]
        _extract_top(sc_scr, r2_scr, top_scr)
        for k in range(N_KEYS):
            rows = key_rows(k)
            r1 = r1_scr[rows, :]
            reach = jnp.zeros((ph, LANES), F32)
            for i in range(P_TOPK):
                reach = jnp.where(r1 == float(i), count[i], reach)
            l1_ref[lt, rows, :] = _twice_bf16(reach)

    half_inv_z = 0.5 * inv_z
    for k in range(N_KEYS):
        rows = key_rows(k)
        w1_ref[lt, rows, :] = _twice_bf16(jnp.exp(s_scr[lt, 0, rows, :] - a[0]) * half_inv_z)
        sc_scr[rows, :] = jnp.exp(s_scr[lt, 1, rows, :] - b[0])
    for h in range(ph):
        r2_ref[lt, h] = _pack_bf16(r2_scr[pl.ds(h, N_KEYS, stride=ph), :])
        e2_ref[lt, h] = _pack_bf16(sc_scr[pl.ds(h, N_KEYS, stride=ph), :])


def _peer_topk(x, wq_t, subkeys):
    t = x.shape[0]
    nt = t // LANES
    tm = PEER_TOPK_TOKENS if t % PEER_TOPK_TOKENS == 0 else LANES
    n_tiles = tm // LANES
    tile = pl.BlockSpec((n_tiles, _HK, LANES), lambda i: (i, 0, 0))
    tile3 = pl.BlockSpec((n_tiles, P_HEADS, N_KEYS // 2, LANES), lambda i: (i, 0, 0, 0))
    flat = jax.ShapeDtypeStruct((nt, _HK, LANES), jnp.uint32)
    cube = jax.ShapeDtypeStruct((nt, P_HEADS, N_KEYS // 2, LANES), jnp.uint32)
    return pl.pallas_call(
        _peer_topk_kernel,
        grid=(t // tm,),
        in_specs=[pl.BlockSpec((tm, D_MODEL), lambda i: (i, 0)),
                  pl.BlockSpec((2 * _HK, D_MODEL), lambda i: (0, 0)),
                  pl.BlockSpec((2 * P_HEADS, N_KEYS, D_KEY // 2), lambda i: (0, 0, 0))],
        out_specs=[tile, tile, tile3, tile3],
        out_shape=[flat, flat, cube, cube],
        scratch_shapes=[pltpu.VMEM((n_tiles, 2, _HK, LANES), F32),
                        pltpu.VMEM((_HK, LANES), F32),
                        pltpu.VMEM((_HK, LANES), F32),
                        pltpu.VMEM((_HK, LANES), F32),
                        pltpu.VMEM((P_TOPK, P_HEADS, LANES), F32)],
        compiler_params=_params("parallel"),
        name="peer_topk",
    )(x, wq_t, subkeys)


def _replicated_bf16(ref, tile, row):
    return _unpack_bf16(ref[tile, pl.ds(row, SUBLANES, stride=0), :])


def _peer_mix_kernel(x_ref, u_ref, u_next_ref, vt_ref, l1_ref, w1_ref, r2_ref, e2_ref, g_ref, b_ref,
                     out_ref, xt_scr, acc_scr, act_scr, p_scr):
    e = pl.program_id(1)
    n_groups, _, group = xt_scr.shape
    lookahead = min(2, n_groups)

    def project(gi, experts_ref):
        act_scr[gi] = jnp.dot(_unpack_bf16(experts_ref[...]), xt_scr[gi],
                              preferred_element_type=F32)

    def mix(gi):
        acc_scr[gi] += jnp.dot(_unpack_bf16(vt_ref[...]), p_scr[gi], preferred_element_type=F32)

    @pl.when(e == 0)
    def _():
        for gi in range(n_groups):
            xt_scr[gi] = x_ref[gi * group:(gi + 1) * group, :].T.astype(BF16)
        acc_scr[...] = jnp.zeros(acc_scr.shape, F32)
        for gi in range(lookahead):
            project(gi, u_ref)

    for gi in range(n_groups):
        for c0 in range(0, group, LANES):
            cols = slice(c0, c0 + LANES)
            ct = (gi * group + c0) // LANES
            for kb in range(0, N_KEYS, PEER_KEY_BLOCK):
                subs = range(kb, kb + PEER_KEY_BLOCK, BF16_ROWS)
                gates = {(jj, k0): jnp.zeros((BF16_ROWS, LANES), BF16)
                         for jj in range(PEER_ROWS_PER_STEP) for k0 in subs}
                for h in range(P_HEADS):
                    rows_h = [jj * P_HEADS + h for jj in range(PEER_ROWS_PER_STEP)]
                    reach = [_replicated_bf16(l1_ref, ct, r) for r in rows_h]
                    weight = [_replicated_bf16(w1_ref, ct, r) for r in rows_h]
                    for k0 in subs:
                        words = slice(k0 // 2, k0 // 2 + SUBLANES)
                        r2 = _unpack_bf16(r2_ref[ct, h, words, :])
                        e2 = _unpack_bf16(e2_ref[ct, h, words, :])
                        for jj in range(PEER_ROWS_PER_STEP):
                            picked = jnp.where(r2 < reach[jj], e2, jnp.zeros_like(e2))
                            gates[jj, k0] = gates[jj, k0] + picked * weight[jj]
                for jj in range(PEER_ROWS_PER_STEP):
                    for k0 in subs:
                        rows = slice(jj * N_KEYS + k0, jj * N_KEYS + k0 + BF16_ROWS)
                        a = act_scr[gi, rows, cols]
                        gelu = a * (1.0 + lax.erf(a * (2.0 ** -0.5)))
                        p_scr[gi, rows, cols] = gelu.astype(BF16) * gates[jj, k0]
        if gi >= 1:
            mix(gi - 1)
        ahead = gi + lookahead
        if ahead < n_groups:
            project(ahead, u_ref)
        else:
            project(ahead - n_groups, u_next_ref)
    mix(n_groups - 1)

    @pl.when(e == pl.num_programs(1) - 1)
    def _():
        for gi in range(n_groups):
            rows = slice(gi * group, (gi + 1) * group)
            y = DN_ALPHA * x_ref[rows, :] + acc_scr[gi].T
            out_ref[rows, :] = _layer_norm(y, g_ref[...], b_ref[...])


def _peer_mix(x, u, vt, l1, w1, r2, e2, g, b, tm):
    t = x.shape[0]
    te = PEER_ROWS_PER_STEP * N_KEYS
    tr = PEER_ROWS_PER_STEP * P_HEADS
    nt = tm // LANES
    group = min(tm, PEER_COLUMN_GROUP)
    n_groups = tm // group
    n_steps = N_EXPERTS // te
    return pl.pallas_call(
        _peer_mix_kernel,
        grid=(t // tm, n_steps),
        in_specs=[pl.BlockSpec((tm, D_MODEL), lambda i, e: (i, 0)),
                  pl.BlockSpec((te // 2, D_MODEL), lambda i, e: (e, 0)),
                  pl.BlockSpec((te // 2, D_MODEL), lambda i, e: (jnp.minimum(e + 1, n_steps - 1), 0)),
                  pl.BlockSpec((D_MODEL // 2, te), lambda i, e: (0, e)),
                  pl.BlockSpec((nt, tr, LANES), lambda i, e: (i, e, 0)),
                  pl.BlockSpec((nt, tr, LANES), lambda i, e: (i, e, 0)),
                  pl.BlockSpec((nt, P_HEADS, N_KEYS // 2, LANES), lambda i, e: (i, 0, 0, 0)),
                  pl.BlockSpec((nt, P_HEADS, N_KEYS // 2, LANES), lambda i, e: (i, 0, 0, 0)),
                  pl.BlockSpec((1, D_MODEL), lambda i, e: (0, 0)),
                  pl.BlockSpec((1, D_MODEL), lambda i, e: (0, 0))],
        out_specs=pl.BlockSpec((tm, D_MODEL), lambda i, e: (i, 0)),
        out_shape=jax.ShapeDtypeStruct((t, D_MODEL), F32),
        scratch_shapes=[pltpu.VMEM((n_groups, D_MODEL, group), BF16),
                        pltpu.VMEM((n_groups, D_MODEL, group), F32),
                        pltpu.VMEM((n_groups, te, group), F32),
                        pltpu.VMEM((n_groups, te, group), BF16)],
        compiler_params=_params("parallel", "arbitrary"),
        name="peer_mix",
    )(x, u, u, vt, l1, w1, r2, e2, g, b)


EXPERT_PREP_ROWS = 512


def _expert_prep_kernel(u_ref, v_ref, uw_ref, vtw_ref):
    uw_ref[...] = _pack_bf16(u_ref[...])
    vtw_ref[...] = _pack_bf16(v_ref[...].T)


def _expert_prep(u, v, layer):
    _, n, d = u.shape
    rows = EXPERT_PREP_ROWS
    return pl.pallas_call(
        _expert_prep_kernel,
        grid=(n // rows,),
        in_specs=[pl.BlockSpec((None, rows, d), lambda i: (layer, i, 0)),
                  pl.BlockSpec((None, rows, d), lambda i: (layer, i, 0))],
        out_specs=[pl.BlockSpec((rows // 2, d), lambda i: (i, 0)),
                   pl.BlockSpec((d // 2, rows), lambda i: (0, i))],
        out_shape=[jax.ShapeDtypeStruct((n // 2, d), jnp.uint32),
                   jax.ShapeDtypeStruct((d // 2, n), jnp.uint32)],
        compiler_params=_params("parallel"),
        name="expert_prep",
    )(u, v)


def _peer_layer(x, pw, g, b):
    t = x.shape[0]
    l1, w1, r2, e2 = _peer_topk(x, pw["wq_t"], pw["subkeys"])
    tm = PEER_TOKEN_TILE if t % PEER_TOKEN_TILE == 0 else LANES
    return _peer_mix(x, pw["u"], pw["vt"], l1, w1, r2, e2, g, b, tm)


def _t5_bucket(dist):
    max_exact = N_BUCKETS // 2
    d = jnp.maximum(dist, 0)
    df = jnp.maximum(d, 1).astype(F32)
    large = max_exact + (jnp.log(df / max_exact) / math.log(MAX_DISTANCE / max_exact)
                         * (N_BUCKETS - max_exact)).astype(jnp.int32)
    large = jnp.minimum(large, N_BUCKETS - 1)
    return jnp.where(d < max_exact, d, large)


def _row(a):
    return a.reshape(1, -1)


def _prepare(w):
    n_gate = 2 * A_HEADS
    w_in = w["a_w_in"][0]
    b_in = w["a_b_in"][0]
    prep = {
        "w_in_main": w_in[:, :2 * A_INNER].astype(BF16),
        "b_in_main": _row(b_in[:2 * A_INNER]),
        "w_in_gate": jnp.pad(w_in[:, 2 * A_INNER:], ((0, 0), (0, LANES - n_gate))).astype(BF16),
        "b_in_gate": _row(jnp.pad(b_in[2 * A_INNER:], (0, LANES - n_gate))),
        "conv_w": w["a_conv_w"][0],
        "conv_b": _row(w["a_conv_b"][0]),
        "wq": w["a_w_q"][0].astype(BF16),
        "wk": w["a_w_k"][0].astype(BF16),
        "wv": w["a_w_v"][0].astype(BF16),
        "w_out": w["a_w_out"][0].astype(BF16),
        "w_qkv": jnp.concatenate([w["b_w_q"][0], w["kv_w"]], axis=1).astype(BF16),
        "w_o": w["b_w_o"][0].astype(BF16),
        "sinks": _row(w["b_sinks"][0]),
        "rel_bias": w["rel_bias"],
    }
    peer = []
    half = D_KEY // 2
    for layer in range(DEPTH):
        wq = w["peer_w_q"][layer].reshape(D_MODEL, P_HEADS, 2, half)
        sk = w["peer_subkeys"][layer]
        peer.append({
            "wq_t": wq.transpose(2, 1, 3, 0).reshape(2 * _HK, D_MODEL).astype(BF16),
            "subkeys": sk.reshape(2 * P_HEADS, N_KEYS, half).astype(BF16),
        })
        peer[-1]["u"], peer[-1]["vt"] = _expert_prep(w["peer_u"], w["peer_v"], layer)
    prep["peer"] = peer
    return prep


def _zero_bias(n):
    return jnp.zeros((1, n), F32)


def _attention_tables(rel_bias):
    qi = jnp.arange(WINDOW)[:, None]
    kj = jnp.arange(2 * WINDOW)[None, :]
    bucket = _t5_bucket(qi + WINDOW - kj).astype(jnp.int32)
    return _bias_tables(bucket, rel_bias)


def _prompt_trunk(x, w, p, bias):
    batch, length, d = x.shape
    t = batch * length
    xt = x.reshape(t, d)
    proj = _linear(xt, p["w_in_main"], p["b_in_main"], 512, LINEAR_COLUMNS)
    gates = _linear(xt, p["w_in_gate"], p["b_in_gate"], 512, LANES)
    row_tab, col_tab = _gate_tables(gates, batch, length, MLSTM_CHUNK)
    q, k, v = _conv_qkv(proj, p["conv_w"], p["conv_b"], p["wq"], p["wk"], p["wv"], batch, length)
    h, c_new, n_new, m_new = _mlstm_prompt(q, k, v, row_tab, col_tab, batch, length, MLSTM_CHUNK)
    x1 = _outproj_ln(h, proj, p["w_out"], xt, _row(w["ln_mix_g"][0]), _row(w["ln_mix_b"][0]), 256)
    x2 = _peer_layer(x1, p["peer"][0], _row(w["ln_ffn_g"][0]), _row(w["ln_ffn_b"][0]))
    qkv = _linear(x2, p["w_qkv"], _zero_bias(p["w_qkv"].shape[1]), 512, p["w_qkv"].shape[1])
    o = _swa_prompt(qkv, bias, p["sinks"], batch, length)
    x3 = _outproj_ln(o, None, p["w_o"], x2, _row(w["ln_mix_g"][1]), _row(w["ln_mix_b"][1]), 512)
    x4 = _peer_layer(x3, p["peer"][1], _row(w["ln_ffn_g"][1]), _row(w["ln_ffn_b"][1]))

    kvw = B_KV_HEADS * B_HEAD_DIM
    qkv3 = qkv.reshape(batch, length, -1)
    k_win = qkv3[:, -WINDOW:, D_MODEL:D_MODEL + kvw].reshape(batch, WINDOW, B_KV_HEADS, B_HEAD_DIM)
    v_win = qkv3[:, -WINDOW:, D_MODEL + kvw:].reshape(batch, WINDOW, B_KV_HEADS, B_HEAD_DIM)
    conv = proj.reshape(batch, length, -1)[:, -(A_CONV_W - 1):, :A_INNER]
    return (x4.reshape(batch, length, d),
            c_new[None],
            n_new.reshape(1, batch, A_HEADS, A_HEAD_DIM),
            m_new[:, :, 0, 0][None],
            conv[None], k_win, v_win)


def _sample_trunk(x, conv0, c0, n0, m0, k_buf, v_buf, w, p, bias):
    batch, length, d = x.shape
    xt = x.reshape(batch, d)
    proj = _linear(xt, p["w_in_main"], p["b_in_main"], batch, 512)
    gates = _linear(xt, p["w_in_gate"], p["b_in_gate"], batch, LANES)
    buf = conv0[0]
    q, k, v, qt, kt = _sample_conv_qkv(proj, buf.reshape(batch, -1), p["conv_w"], p["conv_b"],
                                       p["wq"], p["wk"], p["wv"])
    h, c_new, n_new, m_new = _sample_mlstm(q, k, v, qt, kt, gates, m0[0], n0[0], c0[0])
    x1 = _outproj_ln(h.reshape(batch, A_INNER), proj, p["w_out"], xt,
                     _row(w["ln_mix_g"][0]), _row(w["ln_mix_b"][0]), batch)
    x2 = _peer_layer(x1, p["peer"][0], _row(w["ln_ffn_g"][0]), _row(w["ln_ffn_b"][0]))
    qkv = _linear(x2, p["w_qkv"], _zero_bias(p["w_qkv"].shape[1]), batch, 512)
    kvw = B_KV_HEADS * B_HEAD_DIM
    o = _swa_decode(qkv, k_buf.reshape(batch, WINDOW, kvw), v_buf.reshape(batch, WINDOW, kvw),
                    bias[:, 0, :], p["sinks"].reshape(B_HEADS, 1))
    x3 = _outproj_ln(o, None, p["w_o"], x2, _row(w["ln_mix_g"][1]), _row(w["ln_mix_b"][1]), batch)
    x4 = _peer_layer(x3, p["peer"][1], _row(w["ln_ffn_g"][1]), _row(w["ln_ffn_b"][1]))

    k_new = qkv[:, D_MODEL:D_MODEL + kvw].reshape(batch, 1, B_KV_HEADS, B_HEAD_DIM)
    v_new = qkv[:, D_MODEL + kvw:].reshape(batch, 1, B_KV_HEADS, B_HEAD_DIM)
    k_win = jnp.concatenate([k_buf[:, 1:], k_new], axis=1)
    v_win = jnp.concatenate([v_buf[:, 1:], v_new], axis=1)
    conv = jnp.concatenate([buf[:, 1:], proj[:, None, :A_INNER]], axis=1)
    return (x4.reshape(batch, length, d),
            c_new[None],
            n_new.reshape(1, batch, A_HEADS, A_HEAD_DIM),
            m_new[:, :, 0, 0][None],
            conv[None], k_win, v_win)


def kernel(x_prompt, x_sample, state_mlstm_C, state_mlstm_n, state_mlstm_m, state_mlstm_conv,
           cache_k_win, cache_v_win, a_w_in, a_b_in, a_conv_w, a_conv_b, a_w_q, a_w_k, a_w_v,
           a_w_out, kv_w, b_w_q, b_w_o, b_sinks, rel_bias, ln_mix_g, ln_mix_b, ln_ffn_g,
           ln_ffn_b, peer_w_q, peer_subkeys, peer_u, peer_v):
    w = {"a_w_in": a_w_in, "a_b_in": a_b_in, "a_conv_w": a_conv_w, "a_conv_b": a_conv_b,
         "a_w_q": a_w_q, "a_w_k": a_w_k, "a_w_v": a_w_v, "a_w_out": a_w_out, "kv_w": kv_w,
         "b_w_q": b_w_q, "b_w_o": b_w_o, "b_sinks": b_sinks, "rel_bias": rel_bias,
         "ln_mix_g": ln_mix_g, "ln_mix_b": ln_mix_b, "ln_ffn_g": ln_ffn_g, "ln_ffn_b": ln_ffn_b,
         "peer_w_q": peer_w_q, "peer_subkeys": peer_subkeys, "peer_u": peer_u, "peer_v": peer_v}
    p = _prepare(w)
    bias = _attention_tables(rel_bias)
    prompt = _prompt_trunk(x_prompt, w, p, bias)
    sample = _sample_trunk(x_sample, state_mlstm_conv, state_mlstm_C, state_mlstm_n,
                           state_mlstm_m, cache_k_win, cache_v_win, w, p, bias)
    return (prompt[0], sample[0]) + prompt[1:] + sample[1:]
```

```python
import functools
import math

import jax
import jax.numpy as jnp
from jax import lax
from jax.experimental import pallas as pl
from jax.experimental.pallas import tpu as pltpu

D_MODEL = 1024
DEPTH = 2
A_HEADS = 4
A_INNER = 2 * D_MODEL
A_HEAD_DIM = A_INNER // A_HEADS
A_CONV_W = 4
B_HEADS = 16
B_KV_HEADS = 4
B_GROUP = B_HEADS // B_KV_HEADS
B_HEAD_DIM = D_MODEL // B_HEADS
WINDOW = 128
N_BUCKETS = 32
MAX_DISTANCE = 128
P_HEADS = 8
N_KEYS = 128
N_EXPERTS = N_KEYS * N_KEYS
D_KEY = 256
P_TOPK = 16
DN_ALPHA = (2 * DEPTH) ** 0.25
LN_EPS = 1e-5
NEG_INF = -1e30

LANES = 128
SUBLANES = 8
BF16_ROWS = 2 * SUBLANES
VMEM_LIMIT_BYTES = 56 * 1024 * 1024

LINEAR_COLUMNS = 4096
MLSTM_CHUNK = 512
PEER_TOPK_TOKENS = 256
PEER_TOKEN_TILE = 1024
PEER_ROWS_PER_STEP = 8
PEER_COLUMN_GROUP = 256
PEER_KEY_BLOCK = 16
NOT_SELECTED_RANK = float(P_TOPK)

F32 = jnp.float32
BF16 = jnp.bfloat16


def _params(*sem):
    return pltpu.CompilerParams(dimension_semantics=sem, vmem_limit_bytes=VMEM_LIMIT_BYTES)


def _tree(op, vals):
    vals = list(vals)
    while len(vals) > 1:
        nxt = [op(vals[i], vals[i + 1]) for i in range(0, len(vals) - 1, 2)]
        if len(vals) % 2:
            nxt.append(vals[-1])
        vals = nxt
    return vals[0]


def _layer_norm(y, g, b):
    mu = jnp.mean(y, axis=-1, keepdims=True)
    yc = y - mu
    var = jnp.mean(yc * yc, axis=-1, keepdims=True)
    return yc * lax.rsqrt(var + LN_EPS) * g + b


def _linear_kernel(x_ref, w_ref, b_ref, o_ref):
    x = x_ref[...].astype(BF16)
    o_ref[...] = jnp.dot(x, w_ref[...], preferred_element_type=F32) + b_ref[...]


def _linear(x, w, b, tm, tn):
    t, k = x.shape
    n = w.shape[1]
    return pl.pallas_call(
        _linear_kernel,
        grid=(t // tm, n // tn),
        in_specs=[pl.BlockSpec((tm, k), lambda i, j: (i, 0)),
                  pl.BlockSpec((k, tn), lambda i, j: (0, j)),
                  pl.BlockSpec((1, tn), lambda i, j: (0, j))],
        out_specs=pl.BlockSpec((tm, tn), lambda i, j: (i, j)),
        out_shape=jax.ShapeDtypeStruct((t, n), F32),
        compiler_params=_params("parallel", "arbitrary"),
        name="linear",
    )(x, w, b)


def _outproj_ln_kernel(gated, *refs):
    if gated:
        a_ref, o_ref, w_ref, res_ref, g_ref, b_ref, out_ref = refs
        act = jax.nn.sigmoid(o_ref[...]) * a_ref[...]
    else:
        a_ref, w_ref, res_ref, g_ref, b_ref, out_ref = refs
        act = a_ref[...]
    sub = jnp.dot(act.astype(BF16), w_ref[...], preferred_element_type=F32)
    out_ref[...] = _layer_norm(DN_ALPHA * res_ref[...] + sub, g_ref[...], b_ref[...])


def _outproj_ln(act, gate_src, w, res, g, b, tm):
    t, k = act.shape
    d = w.shape[1]
    gated = gate_src is not None
    in_specs = [pl.BlockSpec((tm, k), lambda i: (i, 0))]
    args = [act]
    if gated:
        in_specs.append(pl.BlockSpec((tm, k), lambda i: (i, 1)))
        args.append(gate_src)
    in_specs += [pl.BlockSpec((k, d), lambda i: (0, 0)),
                 pl.BlockSpec((tm, d), lambda i: (i, 0)),
                 pl.BlockSpec((1, d), lambda i: (0, 0)),
                 pl.BlockSpec((1, d), lambda i: (0, 0))]
    args += [w, res, g, b]
    return pl.pallas_call(
        functools.partial(_outproj_ln_kernel, gated),
        grid=(t // tm,),
        in_specs=in_specs,
        out_specs=pl.BlockSpec((tm, d), lambda i: (i, 0)),
        out_shape=jax.ShapeDtypeStruct((t, d), F32),
        compiler_params=_params("parallel"),
        name="outproj_ln",
    )(*args)


def _log_sigmoid(x):
    return jnp.minimum(x, 0.0) - jnp.log1p(jnp.exp(-jnp.abs(x)))


def _gates_kernel(chunk, g_ref, row_ref, col_ref):
    length = g_ref.shape[0]
    gt = g_ref[...].T
    top = gt[0:SUBLANES]
    row = lax.broadcasted_iota(jnp.int32, top.shape, 0)
    pos = lax.broadcasted_iota(jnp.int32, top.shape, 1) % chunk
    is_f = row >= A_HEADS
    x = jnp.where(is_f, _log_sigmoid(top), 0.0)
    shift = 1
    while shift < chunk:
        x = x + jnp.where(pos >= shift, pltpu.roll(x, shift, 1), 0.0)
        shift *= 2
    table = jnp.where(is_f, x, top)
    row_ref[...] = table
    padded = jnp.concatenate([table, jnp.zeros((LANES - SUBLANES, length), F32)], axis=0)
    col_ref[...] = padded.T


def _gate_tables(gates, batch, length, chunk):
    return pl.pallas_call(
        functools.partial(_gates_kernel, chunk),
        grid=(batch,),
        in_specs=[pl.BlockSpec((length, LANES), lambda b: (b, 0))],
        out_specs=[pl.BlockSpec((None, SUBLANES, length), lambda b: (b, 0, 0)),
                   pl.BlockSpec((length, LANES), lambda b: (b, 0))],
        out_shape=[jax.ShapeDtypeStruct((batch, SUBLANES, length), F32),
                   jax.ShapeDtypeStruct((batch * length, LANES), F32)],
        compiler_params=_params("parallel"),
        name="gate_tables",
    )(gates)


CONV_ROW_CHUNK = 512
CONV_PAD = SUBLANES


def _conv_qkv_kernel(xm_ref, cw_ref, cb_ref, wq_ref, wk_ref, wv_ref,
                     q_ref, k_ref, v_ref, pad_ref):
    length = xm_ref.shape[0]
    pad_ref[0:CONV_PAD, :] = jnp.zeros((CONV_PAD, A_HEAD_DIM), F32)
    pad_ref[CONV_PAD:CONV_PAD + length, :] = xm_ref[...]
    first = CONV_PAD - (A_CONV_W - 1)
    for c0 in range(0, length, CONV_ROW_CHUNK):
        acc = cb_ref[...]
        for w in range(A_CONV_W):
            acc = acc + pad_ref[c0 + first + w:c0 + first + w + CONV_ROW_CHUNK, :] * cw_ref[w:w + 1, :]
        xc = (acc * jax.nn.sigmoid(acc)).astype(BF16)
        xm = xm_ref[c0:c0 + CONV_ROW_CHUNK, :].astype(BF16)
        rows = slice(c0, c0 + CONV_ROW_CHUNK)
        q_ref[rows, :] = jnp.dot(xc, wq_ref[...], preferred_element_type=F32).astype(BF16)
        k = jnp.dot(xc, wk_ref[...], preferred_element_type=F32) * (A_HEAD_DIM ** -0.5)
        k_ref[rows, :] = k.astype(BF16)
        v_ref[rows, :] = jnp.dot(xm, wv_ref[...], preferred_element_type=F32).astype(BF16)


def _conv_qkv(proj, cw, cb, wq, wk, wv, batch, length):
    t = batch * length
    hd = A_HEAD_DIM
    tok = pl.BlockSpec((length, hd), lambda b, h: (b, h))
    wspec = pl.BlockSpec((None, hd, hd), lambda b, h: (h, 0, 0))
    out = jax.ShapeDtypeStruct((t, A_INNER), BF16)
    return pl.pallas_call(
        _conv_qkv_kernel,
        grid=(batch, A_HEADS),
        in_specs=[tok,
                  pl.BlockSpec((A_CONV_W, hd), lambda b, h: (0, h)),
                  pl.BlockSpec((1, hd), lambda b, h: (0, h)),
                  wspec, wspec, wspec],
        out_specs=[tok, tok, tok],
        out_shape=[out, out, out],
        scratch_shapes=[pltpu.VMEM((length + CONV_PAD, hd), F32)],
        compiler_params=_params("parallel", "arbitrary"),
        name="conv_qkv",
    )(proj, cw, cb, wq, wk, wv)


def _mlstm_kernel(q_ref, k_ref, v_ref, row_ref, col_ref,
                  h_ref, c_out_ref, n_out_ref, m_out_ref,
                  c_scr, n_scr, m_scr):
    ci = pl.program_id(1)
    chunk = q_ref.shape[0]
    hd = A_HEAD_DIM

    @pl.when(ci == 0)
    def _():
        c_scr[...] = jnp.zeros(c_scr.shape, F32)
        n_scr[...] = jnp.zeros(n_scr.shape, F32)
        m_scr[...] = jnp.zeros(m_scr.shape, F32)

    t_idx = lax.broadcasted_iota(jnp.int32, (chunk, chunk), 0)
    s_idx = lax.broadcasted_iota(jnp.int32, (chunk, chunk), 1)
    causal = s_idx <= t_idx
    for h in range(A_HEADS):
        cols = slice(h * hd, (h + 1) * hd)
        qh, kh, vh = q_ref[:, cols], k_ref[:, cols], v_ref[:, cols]
        i_col = col_ref[:, h:h + 1]
        f_col = col_ref[:, A_HEADS + h:A_HEADS + h + 1]
        i_row = row_ref[h:h + 1, :]
        f_row = row_ref[A_HEADS + h:A_HEADS + h + 1, :]
        m_prev = m_scr[h][:, 0:1]
        d = jnp.where(causal, f_col - f_row + i_row, NEG_INF)
        b_inter = f_col + m_prev
        m_t = jnp.maximum(b_inter, jnp.max(d, axis=1, keepdims=True))
        qk = lax.dot_general(qh, kh, (((1,), (1,)), ((), ())), preferred_element_type=F32)
        s = qk * jnp.exp(d - m_t)
        w_inter = jnp.exp(b_inter - m_t)
        q_c = jnp.dot(qh, c_scr[h].astype(BF16), preferred_element_type=F32)
        num = jnp.dot(s.astype(BF16), vh, preferred_element_type=F32) + w_inter * q_c
        q_n = jnp.sum(qh.astype(F32) * n_scr[h], axis=1, keepdims=True)
        den = jnp.sum(s, axis=1, keepdims=True) + w_inter * q_n
        h_ref[:, cols] = num / jnp.maximum(jnp.abs(den), jnp.exp(-m_t))
        f_last = f_col[chunk - 1:chunk, :]
        g = f_last - f_col + i_col
        m_new = jnp.maximum(f_last + m_prev, jnp.max(g, axis=0, keepdims=True))
        decay = jnp.exp(f_last + m_prev - m_new)
        wk = jnp.exp(g - m_new) * kh.astype(F32)
        kv = lax.dot_general(wk.astype(BF16), vh, (((0,), (0,)), ((), ())),
                             preferred_element_type=F32)
        c_scr[h] = decay * c_scr[h] + kv
        n_scr[h] = decay * n_scr[h] + jnp.sum(wk, axis=0, keepdims=True)
        m_scr[h] = jnp.broadcast_to(m_new, (1, LANES))

    @pl.when(ci == pl.num_programs(1) - 1)
    def _():
        c_out_ref[...] = c_scr[...]
        n_out_ref[...] = n_scr[...]
        m_out_ref[...] = m_scr[...]


def _mlstm_prompt(q, k, v, row_tab, col_tab, batch, length, chunk):
    nc = length // chunk
    t = batch * length
    tok = pl.BlockSpec((chunk, A_INNER), lambda b, c: (b * nc + c, 0))
    return pl.pallas_call(
        _mlstm_kernel,
        grid=(batch, nc),
        in_specs=[tok, tok, tok,
                  pl.BlockSpec((None, SUBLANES, chunk), lambda b, c: (b, 0, c)),
                  pl.BlockSpec((chunk, LANES), lambda b, c: (b * nc + c, 0))],
        out_specs=[tok,
                   pl.BlockSpec((None, A_HEADS, A_HEAD_DIM, A_HEAD_DIM), lambda b, c: (b, 0, 0, 0)),
                   pl.BlockSpec((None, A_HEADS, 1, A_HEAD_DIM), lambda b, c: (b, 0, 0, 0)),
                   pl.BlockSpec((None, A_HEADS, 1, LANES), lambda b, c: (b, 0, 0, 0))],
        out_shape=[jax.ShapeDtypeStruct((t, A_INNER), F32),
                   jax.ShapeDtypeStruct((batch, A_HEADS, A_HEAD_DIM, A_HEAD_DIM), F32),
                   jax.ShapeDtypeStruct((batch, A_HEADS, 1, A_HEAD_DIM), F32),
                   jax.ShapeDtypeStruct((batch, A_HEADS, 1, LANES), F32)],
        scratch_shapes=[pltpu.VMEM((A_HEADS, A_HEAD_DIM, A_HEAD_DIM), F32),
                        pltpu.VMEM((A_HEADS, 1, A_HEAD_DIM), F32),
                        pltpu.VMEM((A_HEADS, 1, LANES), F32)],
        compiler_params=_params("parallel", "arbitrary"),
        name="mlstm_prompt",
    )(q, k, v, row_tab, col_tab)


def _sample_conv_qkv_kernel(proj_ref, buf_ref, cw_ref, cb_ref, wq_ref, wk_ref, wv_ref,
                            q_ref, k_ref, v_ref, qt_ref, kt_ref):
    xm = proj_ref[:, 0:A_INNER]
    acc = cb_ref[...] + xm * cw_ref[A_CONV_W - 1:A_CONV_W, :]
    for w in range(A_CONV_W - 1):
        acc = acc + buf_ref[:, w * A_INNER:(w + 1) * A_INNER] * cw_ref[w:w + 1, :]
    xc = (acc * jax.nn.sigmoid(acc)).astype(BF16)
    xmb = xm.astype(BF16)
    for h in range(A_HEADS):
        cols = slice(h * A_HEAD_DIM, (h + 1) * A_HEAD_DIM)
        q_ref[:, cols] = jnp.dot(xc[:, cols], wq_ref[h], preferred_element_type=F32)
        k_ref[:, cols] = (jnp.dot(xc[:, cols], wk_ref[h], preferred_element_type=F32)
                          * (A_HEAD_DIM ** -0.5))
        v_ref[:, cols] = jnp.dot(xmb[:, cols], wv_ref[h], preferred_element_type=F32)
    qt_ref[...] = q_ref[...].T
    kt_ref[...] = k_ref[...].T


def _sample_conv_qkv(proj, conv_buf_flat, cw, cb, wq, wk, wv):
    b = proj.shape[0]
    row = jax.ShapeDtypeStruct((b, A_INNER), F32)
    col = jax.ShapeDtypeStruct((A_INNER, b), F32)
    return pl.pallas_call(
        _sample_conv_qkv_kernel,
        out_shape=[row, row, row, col, col],
        compiler_params=pltpu.CompilerParams(vmem_limit_bytes=VMEM_LIMIT_BYTES),
        name="sample_conv_qkv",
    )(proj, conv_buf_flat, cw, cb, wq, wk, wv)


def _sample_mlstm_kernel(q_ref, k_ref, v_ref, qt_ref, kt_ref, g_ref, m0_ref, n0_ref, c0_ref,
                         h_ref, c_ref, n_ref, m_ref):
    b = pl.program_id(0)
    hd = A_HEAD_DIM
    is_b = lax.broadcasted_iota(jnp.int32, (hd, LANES), 1) == b
    gates = g_ref[...]
    m0_all = m0_ref[...]
    for h in range(A_HEADS):
        cols = slice(h * hd, (h + 1) * hd)
        q_col = jnp.sum(jnp.where(is_b, qt_ref[cols, :], 0.0), axis=1, keepdims=True)
        k_col = jnp.sum(jnp.where(is_b, kt_ref[cols, :], 0.0), axis=1, keepdims=True)
        q_mat = jnp.broadcast_to(q_col, (hd, hd))
        k_mat = jnp.broadcast_to(k_col, (hd, hd))
        q_row, k_row, v_row = q_ref[:, cols], k_ref[:, cols], v_ref[:, cols]
        log_i = gates[:, h:h + 1]
        log_f = _log_sigmoid(gates[:, A_HEADS + h:A_HEADS + h + 1])
        m0 = m0_all[:, h:h + 1]
        m_t = jnp.maximum(log_f + m0, log_i)
        w_inter = jnp.exp(log_f + m0 - m_t)
        w_new = jnp.exp(log_i - m_t)
        c0 = c0_ref[h]
        n0 = n0_ref[h]
        s = jnp.sum(q_row * k_row, axis=1, keepdims=True) * w_new
        q_c = jnp.sum(q_mat * c0, axis=0, keepdims=True)
        q_n = jnp.sum(q_row * n0, axis=1, keepdims=True)
        num = s * v_row + w_inter * q_c
        den = s + w_inter * q_n
        h_ref[:, cols] = num / jnp.maximum(jnp.abs(den), jnp.exp(-m_t))
        c_ref[h] = w_inter * c0 + (w_new * k_mat) * v_row
        n_ref[h] = w_inter * n0 + w_new * k_row
        m_ref[h] = jnp.broadcast_to(m_t, (1, LANES))


def _sample_mlstm(q, k, v, qt, kt, gates, m0, n0, c0):
    b = q.shape[0]
    hd = A_HEAD_DIM
    row3 = lambda a: a.reshape(b, 1, a.shape[-1])
    rspec = pl.BlockSpec((None, 1, A_INNER), lambda i: (i, 0, 0))
    cspec = pl.BlockSpec((A_INNER, LANES), lambda i: (0, 0))
    nspec = pl.BlockSpec((None, A_HEADS, 1, hd), lambda i: (i, 0, 0, 0))
    mspec = pl.BlockSpec((None, A_HEADS, 1, LANES), lambda i: (i, 0, 0, 0))
    big = pl.BlockSpec((None, A_HEADS, hd, hd), lambda i: (i, 0, 0, 0))
    return pl.pallas_call(
        _sample_mlstm_kernel,
        grid=(b,),
        in_specs=[rspec, rspec, rspec, cspec, cspec,
                  pl.BlockSpec((None, 1, LANES), lambda i: (i, 0, 0)),
                  pl.BlockSpec((None, 1, A_HEADS), lambda i: (i, 0, 0)),
                  nspec, big],
        out_specs=[rspec, big, nspec, mspec],
        out_shape=[jax.ShapeDtypeStruct((b, 1, A_INNER), F32),
                   jax.ShapeDtypeStruct((b, A_HEADS, hd, hd), F32),
                   jax.ShapeDtypeStruct((b, A_HEADS, 1, hd), F32),
                   jax.ShapeDtypeStruct((b, A_HEADS, 1, LANES), F32)],
        compiler_params=_params("parallel"),
        name="sample_mlstm",
    )(row3(q), row3(k), row3(v), qt, kt, row3(gates), row3(m0),
      n0.reshape(b, A_HEADS, 1, hd), c0)


def _bias_kernel(bucket_ref, rel_ref, bias_ref):
    h = pl.program_id(0)
    bucket = bucket_ref[...]
    acc = jnp.zeros(bucket.shape, F32)
    for n in range(N_BUCKETS):
        acc = jnp.where(bucket == n, rel_ref[n, h], acc)
    bias_ref[...] = acc


def _bias_tables(bucket, rel_bias):
    w, w2 = bucket.shape
    return pl.pallas_call(
        _bias_kernel,
        grid=(B_HEADS,),
        in_specs=[pl.BlockSpec((w, w2), lambda h: (0, 0)),
                  pl.BlockSpec(memory_space=pltpu.SMEM)],
        out_specs=pl.BlockSpec((None, w, w2), lambda h: (h, 0, 0)),
        out_shape=jax.ShapeDtypeStruct((B_HEADS, w, w2), F32),
        compiler_params=_params("arbitrary"),
        name="bias_tables",
    )(bucket, rel_bias)


def _swa_prompt_kernel(q_ref, kp_ref, kc_ref, vp_ref, vc_ref, bias_ref, sink_ref, o_ref):
    n = pl.program_id(1)
    w = WINDOW
    q = q_ref[...].astype(BF16)
    kk = jnp.concatenate([kp_ref[...], kc_ref[...]], axis=0).astype(BF16)
    vv = jnp.concatenate([vp_ref[...], vc_ref[...]], axis=0).astype(BF16)
    qi = lax.broadcasted_iota(jnp.int32, (w, 2 * w), 0)
    kj = lax.broadcasted_iota(jnp.int32, (w, 2 * w), 1)
    dist = qi + w - kj
    valid = (dist >= 0) & (dist < w) & ((kj >= w) | (n > 0))
    for h in range(B_HEADS):
        kvh = h // B_GROUP
        qh = q[:, h * B_HEAD_DIM:(h + 1) * B_HEAD_DIM]
        kh = kk[:, kvh * B_HEAD_DIM:(kvh + 1) * B_HEAD_DIM]
        vh = vv[:, kvh * B_HEAD_DIM:(kvh + 1) * B_HEAD_DIM]
        s = lax.dot_general(qh, kh, (((1,), (1,)), ((), ())), preferred_element_type=F32)
        s = jnp.where(valid, s * (B_HEAD_DIM ** -0.5) + bias_ref[h], NEG_INF)
        sink = sink_ref[0, h]
        m = jnp.maximum(jnp.max(s, axis=1, keepdims=True), sink)
        p = jnp.exp(s - m)
        den = jnp.sum(p, axis=1, keepdims=True) + jnp.exp(sink - m)
        p = (p / den).astype(BF16)
        o_ref[:, h * B_HEAD_DIM:(h + 1) * B_HEAD_DIM] = jnp.dot(p, vh, preferred_element_type=F32)


def _swa_prompt(qkv, bias, sinks, batch, length):
    nb = length // WINDOW
    kvw = B_KV_HEADS * B_HEAD_DIM
    kcol = D_MODEL // kvw
    cur = lambda b, n: b * nb + n
    prev = lambda b, n: b * nb + jnp.maximum(n - 1, 0)
    return pl.pallas_call(
        _swa_prompt_kernel,
        grid=(batch, nb),
        in_specs=[pl.BlockSpec((WINDOW, D_MODEL), lambda b, n: (cur(b, n), 0)),
                  pl.BlockSpec((WINDOW, kvw), lambda b, n: (prev(b, n), kcol)),
                  pl.BlockSpec((WINDOW, kvw), lambda b, n: (cur(b, n), kcol)),
                  pl.BlockSpec((WINDOW, kvw), lambda b, n: (prev(b, n), kcol + 1)),
                  pl.BlockSpec((WINDOW, kvw), lambda b, n: (cur(b, n), kcol + 1)),
                  pl.BlockSpec((B_HEADS, WINDOW, 2 * WINDOW), lambda b, n: (0, 0, 0)),
                  pl.BlockSpec(memory_space=pltpu.SMEM)],
        out_specs=pl.BlockSpec((WINDOW, D_MODEL), lambda b, n: (cur(b, n), 0)),
        out_shape=jax.ShapeDtypeStruct((batch * length, D_MODEL), F32),
        compiler_params=_params("parallel", "arbitrary"),
        name="swa_prompt",
    )(qkv, qkv, qkv, qkv, qkv, bias, sinks)


DECODE_BATCH_BLOCK = 8


def _swa_decode_kernel(qkv_ref, ck_ref, cv_ref, bias_ref, sink_ref, o_ref):
    w = WINDOW
    hd = B_HEAD_DIM
    kvw = B_KV_HEADS * hd
    scale = hd ** -0.5
    pos = lax.broadcasted_iota(jnp.int32, (B_GROUP, w), 1)
    for bi in range(DECODE_BATCH_BLOCK):
        row = qkv_ref[bi:bi + 1, :]
        for kvh in range(B_KV_HEADS):
            heads = slice(kvh * B_GROUP, (kvh + 1) * B_GROUP)
            kv_cols = slice(kvh * hd, (kvh + 1) * hd)
            q = jnp.concatenate([row[:, h * hd:(h + 1) * hd]
                                 for h in range(kvh * B_GROUP, (kvh + 1) * B_GROUP)], axis=0)
            k_new = row[:, D_MODEL + kvh * hd:D_MODEL + (kvh + 1) * hd]
            v_new = row[:, D_MODEL + kvw + kvh * hd:D_MODEL + kvw + (kvh + 1) * hd]
            kc = ck_ref[bi, :, kv_cols].astype(BF16)
            vc = cv_ref[bi, :, kv_cols].astype(BF16)
            s_c = lax.dot_general(q.astype(BF16), kc, (((1,), (1,)), ((), ())),
                                  preferred_element_type=F32)
            s_c = jnp.where(pos >= 1, s_c * scale + bias_ref[heads, 0:w], NEG_INF)
            s_n = jnp.sum(q * k_new, axis=1, keepdims=True) * scale + bias_ref[heads, w:w + 1]
            sink = sink_ref[heads, :]
            m = jnp.maximum(jnp.maximum(jnp.max(s_c, axis=1, keepdims=True), s_n), sink)
            p_c = jnp.exp(s_c - m)
            p_n = jnp.exp(s_n - m)
            den = jnp.sum(p_c, axis=1, keepdims=True) + p_n + jnp.exp(sink - m)
            o = (jnp.dot(p_c.astype(BF16), vc, preferred_element_type=F32) + p_n * v_new) / den
            for g in range(B_GROUP):
                h = kvh * B_GROUP + g
                o_ref[bi:bi + 1, h * hd:(h + 1) * hd] = o[g:g + 1, :]


def _swa_decode(qkv, cache_k, cache_v, bias_row, sinks_col):
    b = qkv.shape[0]
    kvw = B_KV_HEADS * B_HEAD_DIM
    bb = DECODE_BATCH_BLOCK
    return pl.pallas_call(
        _swa_decode_kernel,
        grid=(b // bb,),
        in_specs=[pl.BlockSpec((bb, qkv.shape[1]), lambda i: (i, 0)),
                  pl.BlockSpec((bb, WINDOW, kvw), lambda i: (i, 0, 0)),
                  pl.BlockSpec((bb, WINDOW, kvw), lambda i: (i, 0, 0)),
                  pl.BlockSpec((B_HEADS, 2 * WINDOW), lambda i: (0, 0)),
                  pl.BlockSpec((B_HEADS, 1), lambda i: (0, 0))],
        out_specs=pl.BlockSpec((bb, D_MODEL), lambda i: (i, 0)),
        out_shape=jax.ShapeDtypeStruct((b, D_MODEL), F32),
        compiler_params=_params("parallel"),
        name="swa_decode",
    )(qkv, cache_k, cache_v, bias_row, sinks_col)


_CANDIDATES = [(i, j) for i in range(P_TOPK) for j in range(P_TOPK)
               if (i + 1) * (j + 1) <= P_TOPK]
_HK = P_HEADS * N_KEYS


def _pack_bf16(x):
    return pltpu.bitcast(x.astype(BF16), jnp.uint32)


def _unpack_bf16(words):
    return pltpu.bitcast(words, BF16)


def _twice_bf16(x):
    high = pltpu.bitcast(x.astype(BF16).astype(F32), jnp.uint32)
    return high | (high >> 16)


def _extract_top(s_ref, r_ref, top_ref):
    ph = P_HEADS
    for k in range(N_KEYS):
        r_ref[k * ph:(k + 1) * ph, :] = jnp.full((ph, LANES), NOT_SELECTED_RANK, F32)

    def body(r, carry):
        best = _tree(jnp.maximum, [s_ref[k * ph:(k + 1) * ph, :] for k in range(N_KEYS)])
        first = _tree(jnp.minimum,
                      [jnp.where(s_ref[k * ph:(k + 1) * ph, :] == best, float(k), float(N_KEYS))
                       for k in range(N_KEYS)])
        rank = lax.convert_element_type(r, F32)
        for k in range(N_KEYS):
            rows = slice(k * ph, (k + 1) * ph)
            hit = first == float(k)
            s_ref[rows, :] = jnp.where(hit, -jnp.inf, s_ref[rows, :])
            r_ref[rows, :] = jnp.where(hit, rank, r_ref[rows, :])
        top_ref[r] = best
        return carry

    lax.fori_loop(0, P_TOPK, body, 0)


def _sort_desc(vals):
    a = list(vals)
    n = len(a)
    k = 2
    while k <= n:
        j = k // 2
        while j >= 1:
            for i in range(n):
                l = i ^ j
                if l > i:
                    hi, lo = jnp.maximum(a[i], a[l]), jnp.minimum(a[i], a[l])
                    a[i], a[l] = (hi, lo) if (i & k) == 0 else (lo, hi)
            j //= 2
        k *= 2
    return a


def _merge_top(top, grp):
    n = len(top)
    a = [jnp.maximum(top[i], grp[n - 1 - i]) for i in range(n)]
    j = n // 2
    while j >= 1:
        for i in range(n):
            l = i ^ j
            if l > i:
                a[i], a[l] = jnp.maximum(a[i], a[l]), jnp.minimum(a[i], a[l])
        j //= 2
    return a


def _top_values(read):
    top = None
    for k0 in range(0, N_KEYS, P_TOPK):
        grp = _sort_desc([read(k) for k in range(k0, k0 + P_TOPK)])
        top = grp if top is None else _merge_top(top, grp)
    return top


def _tie_flags(read, top):
    flags = _tree(jnp.add, [jnp.where(top[i] > top[i + 1], 0.0, 1.0) for i in range(P_TOPK - 1)])
    reach = _tree(jnp.add, [jnp.where(read(k) >= top[P_TOPK - 1], 1.0, 0.0) for k in range(N_KEYS)])
    return flags + jnp.where(reach == float(P_TOPK), 0.0, 1.0)


def _peer_topk_kernel(x_ref, wq_ref, sk_ref, l1_ref, w1_ref, r2_ref, e2_ref,
                      s_scr, sc_scr, r1_scr, r2_scr, top_scr):
    ph = P_HEADS
    n_tiles = s_scr.shape[0]
    xb = x_ref[...].astype(BF16)
    qt = lax.dot_general(wq_ref[...], xb, (((1,), (1,)), ((), ())), preferred_element_type=F32)
    for p in range(2):
        for h in range(ph):
            q_hp = qt[p * _HK + h * N_KEYS:p * _HK + (h + 1) * N_KEYS].astype(BF16)
            sc = jnp.dot(sk_ref[h * 2 + p], q_hp, preferred_element_type=F32)
            for lt in range(n_tiles):
                tile = s_scr.at[lt, p]
                tile[pl.ds(h, N_KEYS, stride=ph), :] = sc[:, lt * LANES:(lt + 1) * LANES]

    def tile_body(lt, carry):
        _peer_select_tile(lt, s_scr, sc_scr, r1_scr, r2_scr, top_scr,
                          l1_ref, w1_ref, r2_ref, e2_ref)
        return carry

    lax.fori_loop(0, n_tiles, tile_body, 0)


def _peer_select_tile(lt, s_scr, sc_scr, r1_scr, r2_scr, top_scr, l1_ref, w1_ref, r2_ref, e2_ref):
    ph = P_HEADS

    def key_rows(k):
        return slice(k * ph, (k + 1) * ph)

    def read(p):
        return lambda k: s_scr[lt, p, key_rows(k), :]

    a = _top_values(read(0))
    b = _top_values(read(1))
    undecided = jnp.max(_tie_flags(read(0), a) + _tie_flags(read(1), b))
    cand = {ij: a[ij[0]] + b[ij[1]] for ij in _CANDIDATES}
    padded = [cand[ij] for ij in _CANDIDATES]
    padded += [jnp.full((ph, LANES), -jnp.inf, F32)] * (-len(padded) % P_TOPK)
    best = None
    for c0 in range(0, len(padded), P_TOPK):
        grp = _sort_desc(padded[c0:c0 + P_TOPK])
        best = grp if best is None else _merge_top(best, grp)
    tau = best[P_TOPK - 1]
    above = {ij: jnp.where(cand[ij] > tau, 1.0, 0.0) for ij in _CANDIDATES}
    equal = {ij: jnp.where(cand[ij] == tau, 1.0, 0.0) for ij in _CANDIDATES}
    places = float(P_TOPK) - _tree(jnp.add, [above[ij] for ij in _CANDIDATES])
    ea = [jnp.exp(a[i] - a[0]) for i in range(P_TOPK)]
    eb = [jnp.exp(b[j] - b[0]) for j in range(P_TOPK)]
    count = [jnp.zeros((ph, LANES), F32) for _ in range(P_TOPK)]
    z = jnp.zeros((ph, LANES), F32)
    seen = jnp.zeros((ph, LANES), F32)
    for ij in _CANDIDATES:
        i, j = ij
        chosen = above[ij] + jnp.where(seen < places, equal[ij], 0.0)
        seen = seen + equal[ij]
        count[i] = count[i] + chosen
        z = z + chosen * (ea[i] * eb[j])
    inv_z = 1.0 / z

    @pl.when(undecided == 0.0)
    def _():
        for k in range(N_KEYS):
            rows = key_rows(k)
            s1 = s_scr[lt, 0, rows, :]
            reach = jnp.zeros((ph, LANES), F32)
            for i in range(P_TOPK):
                reach = jnp.where(s1 == a[i], count[i], reach)
            l1_ref[lt, rows, :] = _twice_bf16(reach)
            s2 = s_scr[lt, 1, rows, :]
            rank = jnp.zeros((ph, LANES), F32)
            for j in range(P_TOPK):
                rank = jnp.where(b[j] > s2, float(j + 1), rank)
            r2_scr[rows, :] = rank

    @pl.when(undecided != 0.0)
    def _():
        sc_scr[...] = s_scr[lt, 0]
        _extract_top(sc_scr, r1_scr, top_scr)
        sc_scr[...] = s_scr[lt, 1]
        _extract_top(sc_scr, r2_scr, top_scr)
        for k in range(N_KEYS):
            rows = key_rows(k)
            r1 = r1_scr[rows, :]
            reach = jnp.zeros((ph, LANES), F32)
            for i in range(P_TOPK):
                reach = jnp.where(r1 == float(i), count[i], reach)
            l1_ref[lt, rows, :] = _twice_bf16(reach)

    half_inv_z = 0.5 * inv_z
    for k in range(N_KEYS):
        rows = key_rows(k)
        w1_ref[lt, rows, :] = _twice_bf16(jnp.exp(s_scr[lt, 0, rows, :] - a[0]) * half_inv_z)
        sc_scr[rows, :] = jnp.exp(s_scr[lt, 1, rows, :] - b[0])
    for h in range(ph):
        r2_ref[lt, h] = _pack_bf16(r2_scr[pl.ds(h, N_KEYS, stride=ph), :])
        e2_ref[lt, h] = _pack_bf16(sc_scr[pl.ds(h, N_KEYS, stride=ph), :])


def _peer_topk(x, wq_t, subkeys):
    t = x.shape[0]
    nt = t // LANES
    tm = PEER_TOPK_TOKENS if t % PEER_TOPK_TOKENS == 0 else LANES
    n_tiles = tm // LANES
    tile = pl.BlockSpec((n_tiles, _HK, LANES), lambda i: (i, 0, 0))
    tile3 = pl.BlockSpec((n_tiles, P_HEADS, N_KEYS // 2, LANES), lambda i: (i, 0, 0, 0))
    flat = jax.ShapeDtypeStruct((nt, _HK, LANES), jnp.uint32)
    cube = jax.ShapeDtypeStruct((nt, P_HEADS, N_KEYS // 2, LANES), jnp.uint32)
    return pl.pallas_call(
        _peer_topk_kernel,
        grid=(t // tm,),
        in_specs=[pl.BlockSpec((tm, D_MODEL), lambda i: (i, 0)),
                  pl.BlockSpec((2 * _HK, D_MODEL), lambda i: (0, 0)),
                  pl.BlockSpec((2 * P_HEADS, N_KEYS, D_KEY // 2), lambda i: (0, 0, 0))],
        out_specs=[tile, tile, tile3, tile3],
        out_shape=[flat, flat, cube, cube],
        scratch_shapes=[pltpu.VMEM((n_tiles, 2, _HK, LANES), F32),
                        pltpu.VMEM((_HK, LANES), F32),
                        pltpu.VMEM((_HK, LANES), F32),
                        pltpu.VMEM((_HK, LANES), F32),
                        pltpu.VMEM((P_TOPK, P_HEADS, LANES), F32)],
        compiler_params=_params("parallel"),
        name="peer_topk",
    )(x, wq_t, subkeys)


def _replicated_bf16(ref, tile, row):
    return _unpack_bf16(ref[tile, pl.ds(row, SUBLANES, stride=0), :])


def _peer_mix_kernel(x_ref, u_ref, u_next_ref, vt_ref, l1_ref, w1_ref, r2_ref, e2_ref, g_ref, b_ref,
                     out_ref, xt_scr, acc_scr, act_scr, p_scr):
    e = pl.program_id(1)
    n_groups, _, group = xt_scr.shape
    lookahead = min(2, n_groups)

    def project(gi, experts_ref):
        act_scr[gi] = jnp.dot(_unpack_bf16(experts_ref[...]), xt_scr[gi],
                              preferred_element_type=F32)

    def mix(gi):
        acc_scr[gi] += jnp.dot(_unpack_bf16(vt_ref[...]), p_scr[gi], preferred_element_type=F32)

    @pl.when(e == 0)
    def _():
        for gi in range(n_groups):
            xt_scr[gi] = x_ref[gi * group:(gi + 1) * group, :].T.astype(BF16)
        acc_scr[...] = jnp.zeros(acc_scr.shape, F32)
        for gi in range(lookahead):
            project(gi, u_ref)

    for gi in range(n_groups):
        for c0 in range(0, group, LANES):
            cols = slice(c0, c0 + LANES)
            ct = (gi * group + c0) // LANES
            for kb in range(0, N_KEYS, PEER_KEY_BLOCK):
                subs = range(kb, kb + PEER_KEY_BLOCK, BF16_ROWS)
                gates = {(jj, k0): jnp.zeros((BF16_ROWS, LANES), BF16)
                         for jj in range(PEER_ROWS_PER_STEP) for k0 in subs}
                for h in range(P_HEADS):
                    rows_h = [jj * P_HEADS + h for jj in range(PEER_ROWS_PER_STEP)]
                    reach = [_replicated_bf16(l1_ref, ct, r) for r in rows_h]
                    weight = [_replicated_bf16(w1_ref, ct, r) for r in rows_h]
                    for k0 in subs:
                        words = slice(k0 // 2, k0 // 2 + SUBLANES)
                        r2 = _unpack_bf16(r2_ref[ct, h, words, :])
                        e2 = _unpack_bf16(e2_ref[ct, h, words, :])
                        for jj in range(PEER_ROWS_PER_STEP):
                            picked = jnp.where(r2 < reach[jj], e2, jnp.zeros_like(e2))
                            gates[jj, k0] = gates[jj, k0] + picked * weight[jj]
                for jj in range(PEER_ROWS_PER_STEP):
                    for k0 in subs:
                        rows = slice(jj * N_KEYS + k0, jj * N_KEYS + k0 + BF16_ROWS)
                        a = act_scr[gi, rows, cols]
                        gelu = a * (1.0 + lax.erf(a * (2.0 ** -0.5)))
                        p_scr[gi, rows, cols] = gelu.astype(BF16) * gates[jj, k0]
        if gi >= 1:
            mix(gi - 1)
        ahead = gi + lookahead
        if ahead < n_groups:
            project(ahead, u_ref)
        else:
            project(ahead - n_groups, u_next_ref)
    mix(n_groups - 1)

    @pl.when(e == pl.num_programs(1) - 1)
    def _():
        for gi in range(n_groups):
            rows = slice(gi * group, (gi + 1) * group)
            y = DN_ALPHA * x_ref[rows, :] + acc_scr[gi].T
            out_ref[rows, :] = _layer_norm(y, g_ref[...], b_ref[...])


def _peer_mix(x, u, vt, l1, w1, r2, e2, g, b, tm):
    t = x.shape[0]
    te = PEER_ROWS_PER_STEP * N_KEYS
    tr = PEER_ROWS_PER_STEP * P_HEADS
    nt = tm // LANES
    group = min(tm, PEER_COLUMN_GROUP)
    n_groups = tm // group
    n_steps = N_EXPERTS // te
    return pl.pallas_call(
        _peer_mix_kernel,
        grid=(t // tm, n_steps),
        in_specs=[pl.BlockSpec((tm, D_MODEL), lambda i, e: (i, 0)),
                  pl.BlockSpec((te // 2, D_MODEL), lambda i, e: (e, 0)),
                  pl.BlockSpec((te // 2, D_MODEL), lambda i, e: (jnp.minimum(e + 1, n_steps - 1), 0)),
                  pl.BlockSpec((D_MODEL // 2, te), lambda i, e: (0, e)),
                  pl.BlockSpec((nt, tr, LANES), lambda i, e: (i, e, 0)),
                  pl.BlockSpec((nt, tr, LANES), lambda i, e: (i, e, 0)),
                  pl.BlockSpec((nt, P_HEADS, N_KEYS // 2, LANES), lambda i, e: (i, 0, 0, 0)),
                  pl.BlockSpec((nt, P_HEADS, N_KEYS // 2, LANES), lambda i, e: (i, 0, 0, 0)),
                  pl.BlockSpec((1, D_MODEL), lambda i, e: (0, 0)),
                  pl.BlockSpec((1, D_MODEL), lambda i, e: (0, 0))],
        out_specs=pl.BlockSpec((tm, D_MODEL), lambda i, e: (i, 0)),
        out_shape=jax.ShapeDtypeStruct((t, D_MODEL), F32),
        scratch_shapes=[pltpu.VMEM((n_groups, D_MODEL, group), BF16),
                        pltpu.VMEM((n_groups, D_MODEL, group), F32),
                        pltpu.VMEM((n_groups, te, group), F32),
                        pltpu.VMEM((n_groups, te, group), BF16)],
        compiler_params=_params("parallel", "arbitrary"),
        name="peer_mix",
    )(x, u, u, vt, l1, w1, r2, e2, g, b)


EXPERT_PREP_ROWS = 512


def _expert_prep_kernel(u_ref, v_ref, uw_ref, vtw_ref):
    uw_ref[...] = _pack_bf16(u_ref[...])
    vtw_ref[...] = _pack_bf16(v_ref[...].T)


def _expert_prep(u, v, layer):
    _, n, d = u.shape
    rows = EXPERT_PREP_ROWS
    return pl.pallas_call(
        _expert_prep_kernel,
        grid=(n // rows,),
        in_specs=[pl.BlockSpec((None, rows, d), lambda i: (layer, i, 0)),
                  pl.BlockSpec((None, rows, d), lambda i: (layer, i, 0))],
        out_specs=[pl.BlockSpec((rows // 2, d), lambda i: (i, 0)),
                   pl.BlockSpec((d // 2, rows), lambda i: (0, i))],
        out_shape=[jax.ShapeDtypeStruct((n // 2, d), jnp.uint32),
                   jax.ShapeDtypeStruct((d // 2, n), jnp.uint32)],
        compiler_params=_params("parallel"),
        name="expert_prep",
    )(u, v)


def _peer_layer(x, pw, g, b):
    t = x.shape[0]
    l1, w1, r2, e2 = _peer_topk(x, pw["wq_t"], pw["subkeys"])
    tm = PEER_TOKEN_TILE if t % PEER_TOKEN_TILE == 0 else LANES
    return _peer_mix(x, pw["u"], pw["vt"], l1, w1, r2, e2, g, b, tm)


def _t5_bucket(dist):
    max_exact = N_BUCKETS // 2
    d = jnp.maximum(dist, 0)
    df = jnp.maximum(d, 1).astype(F32)
    large = max_exact + (jnp.log(df / max_exact) / math.log(MAX_DISTANCE / max_exact)
                         * (N_BUCKETS - max_exact)).astype(jnp.int32)
    large = jnp.minimum(large, N_BUCKETS - 1)
    return jnp.where(d < max_exact, d, large)


def _row(a):
    return a.reshape(1, -1)


def _prepare(w):
    n_gate = 2 * A_HEADS
    w_in = w["a_w_in"][0]
    b_in = w["a_b_in"][0]
    prep = {
        "w_in_main": w_in[:, :2 * A_INNER].astype(BF16),
        "b_in_main": _row(b_in[:2 * A_INNER]),
        "w_in_gate": jnp.pad(w_in[:, 2 * A_INNER:], ((0, 0), (0, LANES - n_gate))).astype(BF16),
        "b_in_gate": _row(jnp.pad(b_in[2 * A_INNER:], (0, LANES - n_gate))),
        "conv_w": w["a_conv_w"][0],
        "conv_b": _row(w["a_conv_b"][0]),
        "wq": w["a_w_q"][0].astype(BF16),
        "wk": w["a_w_k"][0].astype(BF16),
        "wv": w["a_w_v"][0].astype(BF16),
        "w_out": w["a_w_out"][0].astype(BF16),
        "w_qkv": jnp.concatenate([w["b_w_q"][0], w["kv_w"]], axis=1).astype(BF16),
        "w_o": w["b_w_o"][0].astype(BF16),
        "sinks": _row(w["b_sinks"][0]),
        "rel_bias": w["rel_bias"],
    }
    peer = []
    half = D_KEY // 2
    for layer in range(DEPTH):
        wq = w["peer_w_q"][layer].reshape(D_MODEL, P_HEADS, 2, half)
        sk = w["peer_subkeys"][layer]
        peer.append({
            "wq_t": wq.transpose(2, 1, 3, 0).reshape(2 * _HK, D_MODEL).astype(BF16),
            "subkeys": sk.reshape(2 * P_HEADS, N_KEYS, half).astype(BF16),
        })
        peer[-1]["u"], peer[-1]["vt"] = _expert_prep(w["peer_u"], w["peer_v"], layer)
    prep["peer"] = peer
    return prep


def _zero_bias(n):
    return jnp.zeros((1, n), F32)


def _attention_tables(rel_bias):
    qi = jnp.arange(WINDOW)[:, None]
    kj = jnp.arange(2 * WINDOW)[None, :]
    bucket = _t5_bucket(qi + WINDOW - kj).astype(jnp.int32)
    return _bias_tables(bucket, rel_bias)


def _prompt_trunk(x, w, p, bias):
    batch, length, d = x.shape
    t = batch * length
    xt = x.reshape(t, d)
    proj = _linear(xt, p["w_in_main"], p["b_in_main"], 512, LINEAR_COLUMNS)
    gates = _linear(xt, p["w_in_gate"], p["b_in_gate"], 512, LANES)
    row_tab, col_tab = _gate_tables(gates, batch, length, MLSTM_CHUNK)
    q, k, v = _conv_qkv(proj, p["conv_w"], p["conv_b"], p["wq"], p["wk"], p["wv"], batch, length)
    h, c_new, n_new, m_new = _mlstm_prompt(q, k, v, row_tab, col_tab, batch, length, MLSTM_CHUNK)
    x1 = _outproj_ln(h, proj, p["w_out"], xt, _row(w["ln_mix_g"][0]), _row(w["ln_mix_b"][0]), 256)
    x2 = _peer_layer(x1, p["peer"][0], _row(w["ln_ffn_g"][0]), _row(w["ln_ffn_b"][0]))
    qkv = _linear(x2, p["w_qkv"], _zero_bias(p["w_qkv"].shape[1]), 512, p["w_qkv"].shape[1])
    o = _swa_prompt(qkv, bias, p["sinks"], batch, length)
    x3 = _outproj_ln(o, None, p["w_o"], x2, _row(w["ln_mix_g"][1]), _row(w["ln_mix_b"][1]), 512)
    x4 = _peer_layer(x3, p["peer"][1], _row(w["ln_ffn_g"][1]), _row(w["ln_ffn_b"][1]))

    kvw = B_KV_HEADS * B_HEAD_DIM
    qkv3 = qkv.reshape(batch, length, -1)
    k_win = qkv3[:, -WINDOW:, D_MODEL:D_MODEL + kvw].reshape(batch, WINDOW, B_KV_HEADS, B_HEAD_DIM)
    v_win = qkv3[:, -WINDOW:, D_MODEL + kvw:].reshape(batch, WINDOW, B_KV_HEADS, B_HEAD_DIM)
    conv = proj.reshape(batch, length, -1)[:, -(A_CONV_W - 1):, :A_INNER]
    return (x4.reshape(batch, length, d),
            c_new[None],
            n_new.reshape(1, batch, A_HEADS, A_HEAD_DIM),
            m_new[:, :, 0, 0][None],
            conv[None], k_win, v_win)


def _sample_trunk(x, conv0, c0, n0, m0, k_buf, v_buf, w, p, bias):
    batch, length, d = x.shape
    xt = x.reshape(batch, d)
    proj = _linear(xt, p["w_in_main"], p["b_in_main"], batch, 512)
    gates = _linear(xt, p["w_in_gate"], p["b_in_gate"], batch, LANES)
    buf = conv0[0]
    q, k, v, qt, kt = _sample_conv_qkv(proj, buf.reshape(batch, -1), p["conv_w"], p["conv_b"],
                                       p["wq"], p["wk"], p["wv"])
    h, c_new, n_new, m_new = _sample_mlstm(q, k, v, qt, kt, gates, m0[0], n0[0], c0[0])
    x1 = _outproj_ln(h.reshape(batch, A_INNER), proj, p["w_out"], xt,
                     _row(w["ln_mix_g"][0]), _row(w["ln_mix_b"][0]), batch)
    x2 = _peer_layer(x1, p["peer"][0], _row(w["ln_ffn_g"][0]), _row(w["ln_ffn_b"][0]))
    qkv = _linear(x2, p["w_qkv"], _zero_bias(p["w_qkv"].shape[1]), batch, 512)
    kvw = B_KV_HEADS * B_HEAD_DIM
    o = _swa_decode(qkv, k_buf.reshape(batch, WINDOW, kvw), v_buf.reshape(batch, WINDOW, kvw),
                    bias[:, 0, :], p["sinks"].reshape(B_HEADS, 1))
    x3 = _outproj_ln(o, None, p["w_o"], x2, _row(w["ln_mix_g"][1]), _row(w["ln_mix_b"][1]), batch)
    x4 = _peer_layer(x3, p["peer"][1], _row(w["ln_ffn_g"][1]), _row(w["ln_ffn_b"][1]))

    k_new = qkv[:, D_MODEL:D_MODEL + kvw].reshape(batch, 1, B_KV_HEADS, B_HEAD_DIM)
    v_new = qkv[:, D_MODEL + kvw:].reshape(batch, 1, B_KV_HEADS, B_HEAD_DIM)
    k_win = jnp.concatenate([k_buf[:, 1:], k_new], axis=1)
    v_win = jnp.concatenate([v_buf[:, 1:], v_new], axis=1)
    conv = jnp.concatenate([buf[:, 1:], proj[:, None, :A_INNER]], axis=1)
    return (x4.reshape(batch, length, d),
            c_new[None],
            n_new.reshape(1, batch, A_HEADS, A_HEAD_DIM),
            m_new[:, :, 0, 0][None],
            conv[None], k_win, v_win)


def kernel(x_prompt, x_sample, state_mlstm_C, state_mlstm_n, state_mlstm_m, state_mlstm_conv,
           cache_k_win, cache_v_win, a_w_in, a_b_in, a_conv_w, a_conv_b, a_w_q, a_w_k, a_w_v,
           a_w_out, kv_w, b_w_q, b_w_o, b_sinks, rel_bias, ln_mix_g, ln_mix_b, ln_ffn_g,
           ln_ffn_b, peer_w_q, peer_subkeys, peer_u, peer_v):
    w = {"a_w_in": a_w_in, "a_b_in": a_b_in, "a_conv_w": a_conv_w, "a_conv_b": a_conv_b,
         "a_w_q": a_w_q, "a_w_k": a_w_k, "a_w_v": a_w_v, "a_w_out": a_w_out, "kv_w": kv_w,
         "b_w_q": b_w_q, "b_w_o": b_w_o, "b_sinks": b_sinks, "rel_bias": rel_bias,
         "ln_mix_g": ln_mix_g, "ln_mix_b": ln_mix_b, "ln_ffn_g": ln_ffn_g, "ln_ffn_b": ln_ffn_b,
         "peer_w_q": peer_w_q, "peer_subkeys": peer_subkeys, "peer_u": peer_u, "peer_v": peer_v}
    p = _prepare(w)
    bias = _attention_tables(rel_bias)
    prompt = _prompt_trunk(x_prompt, w, p, bias)
    sample = _sample_trunk(x_sample, state_mlstm_conv, state_mlstm_C, state_mlstm_n,
                           state_mlstm_m, cache_k_win, cache_v_win, w, p, bias)
    return (prompt[0], sample[0]) + prompt[1:] + sample[1:]
```
